```python
import jax
import jax.numpy as jnp
from jax import lax
import numpy as np

D_MODEL = 1024
BATCH = 8
SEQ = 4096
DEPTH = 2

HEAD_DIM = 64
D_MIX = D_MODEL
A_WIDTH = D_MIX // 4
A_GROUPS = A_WIDTH // HEAD_DIM
CHUNK = 128
B_WIDTH = D_MIX // 2
B_HEADS = B_WIDTH // HEAD_DIM
DILATED_PATTERNS = ((128, 1), (512, 4), (2048, 16))
C_WIDTH = D_MIX - A_WIDTH - B_WIDTH
C_HEADS = C_WIDTH // HEAD_DIM
RET_CHUNK = 128
ROPE_BASE = 10000.0
IN_COLS = 2 * A_WIDTH + 3 * B_WIDTH + 4 * C_WIDTH
N_GROUPS = 4
EXPERTS_PER_GROUP = 4
N_EXPERTS = N_GROUPS * EXPERTS_PER_GROUP
TOP_K_INNER = 2
EXPERT_HIDDEN = D_MODEL // 2
EPS = 1e-6

kernel_name = 'hybrid_gmlp_dilated_retention_hmoe'


def rms_norm(x, g):
    xf = x.astype(jnp.float32)
    y = xf * lax.rsqrt(jnp.mean(xf * xf, axis=-1, keepdims=True) + EPS)
    return (y * g.astype(jnp.float32)).astype(x.dtype)


def causal_spatial_gating(z, ln_g, ln_b, w_s, b_s):
    bsz, seq, _ = z.shape
    u, v = jnp.split(z, 2, axis=-1)
    vf = v.astype(jnp.float32)
    mu = jnp.mean(vf, -1, keepdims=True)
    var = jnp.mean(jnp.square(vf - mu), -1, keepdims=True)
    vn = (vf - mu) * lax.rsqrt(var + EPS) * ln_g.astype(jnp.float32) + ln_b.astype(jnp.float32)
    vn = vn.reshape(bsz, seq // CHUNK, CHUNK, A_GROUPS, HEAD_DIM)
    causal = jnp.tril(jnp.ones((CHUNK, CHUNK), dtype=bool))
    w = jnp.where(causal, w_s.astype(jnp.float32), 0.0)
    s = jnp.einsum('gts,bcsgd->bctgd', w, vn) + b_s.astype(jnp.float32).T[:, :, None]
    return (u.astype(jnp.float32) * s.reshape(bsz, seq, A_WIDTH)).astype(z.dtype)


def dilated_branch(q, k, v, window, dilation):
    bsz, seq, nh, dh = q.shape
    back = window // dilation
    blk = back
    n_str = -(-seq // dilation)
    n_pad = -(-n_str // blk) * blk
    nb = n_pad // blk

    def to_streams(t):
        t = jnp.pad(t, ((0, 0), (0, n_str * dilation - seq), (0, 0), (0, 0)))
        t = t.reshape(bsz, n_str, dilation, nh, dh).transpose(0, 2, 1, 3, 4)
        t = jnp.pad(t, ((0, 0), (0, 0), (0, n_pad - n_str), (0, 0), (0, 0)))
        return t.reshape(bsz, dilation, nb, blk, nh, dh)

    def with_prev(t):
        prev = jnp.pad(t, ((0, 0), (0, 0), (1, 0), (0, 0), (0, 0), (0, 0)))[:, :, :-1]
        return jnp.concatenate([prev, t], axis=3)

    qs = to_streams(q)
    kb = with_prev(to_streams(k))
    vb = with_prev(to_streams(v))
    s = jnp.einsum('brnqhd,brnkhd->brnhqk', qs, kb,
                   preferred_element_type=jnp.float32) * (dh ** -0.5)
    qi = jnp.arange(blk)[:, None]
    ki = jnp.arange(2 * blk)[None, :]
    dist = blk + qi - ki
    band = (dist >= 0) & (dist <= back)
    first = (jnp.arange(nb) == 0)[:, None, None] & (ki < blk)[None]
    valid = band[None] & ~first
    s = jnp.where(valid[None, None, :, None], s, -jnp.inf)
    m = jnp.max(s, -1, keepdims=True)
    p = jnp.exp(s - m)
    l = jnp.sum(p, -1, keepdims=True)
    o = jnp.einsum('brnhqk,brnkhd->brnqhd', p / l, vb.astype(jnp.float32))
    lse = jnp.swapaxes((m + jnp.log(l))[..., 0], -1, -2)

    def from_streams(t):
        t = t.reshape((bsz, dilation, n_pad) + t.shape[4:])[:, :, :n_str]
        t = jnp.swapaxes(t, 1, 2)
        return t.reshape((bsz, n_str * dilation) + t.shape[3:])[:, :seq]

    return from_streams(o), from_streams(lse)


def dilated_attention(q, k, v):
    outs, lses = [], []
    for window, dilation in DILATED_PATTERNS:
        o, lse = dilated_branch(q, k, v, window, dilation)
        outs.append(o)
        lses.append(lse)
    wts = jax.nn.softmax(jnp.stack(lses, 0), axis=0)
    return jnp.einsum('pbsh,pbshd->bshd', wts, jnp.stack(outs, 0))


def rotary(x, pos):
    half = x.shape[-1] // 2
    inv = ROPE_BASE ** (-jnp.arange(half, dtype=jnp.float32) / half)
    ang = pos.astype(jnp.float32)[:, None] * inv[None]
    cos = jnp.cos(ang)[None, :, None, :]
    sin = jnp.sin(ang)[None, :, None, :]
    x1 = x[..., :half].astype(jnp.float32)
    x2 = x[..., half:].astype(jnp.float32)
    return jnp.concatenate([x1 * cos - x2 * sin, x1 * sin + x2 * cos], -1).astype(x.dtype)


def retention(q, k, v):
    bsz, seq, nh, dh = q.shape
    log_g = jnp.log(1.0 - 2.0 ** (-5.0 - jnp.arange(nh, dtype=jnp.float32)))
    c = RET_CHUNK
    nc = seq // c

    def chunks(t):
        return t.astype(jnp.float32).reshape(bsz, nc, c, nh, dh).transpose(1, 0, 3, 2, 4)

    qc, kc, vc = chunks(q), chunks(k * (dh ** -0.5)), chunks(v)
    idx = jnp.arange(c)
    diff = idx[:, None] - idx[None, :]
    decay_in = jnp.where(diff >= 0,
                         jnp.exp(log_g[:, None, None] * jnp.maximum(diff, 0)[None]), 0.0)
    k_decay = jnp.exp(log_g[:, None] * (c - 1 - idx)[None])[..., None]
    q_decay = jnp.exp(log_g[:, None] * (idx + 1)[None])[..., None]
    chunk_decay = jnp.exp(log_g * c)[:, None, None]

    def step(state, xs):
        qi, ki, vi = xs
        inner = jnp.einsum('bhqd,bhkd->bhqk', qi, ki) * decay_in
        o = (jnp.einsum('bhqk,bhkd->bhqd', inner, vi)
             + jnp.einsum('bhqd,bhde->bhqe', qi * q_decay, state))
        state = state * chunk_decay + jnp.einsum('bhkd,bhke->bhde', ki * k_decay, vi)
        return state, o

    state0 = jnp.zeros((bsz, nh, dh, dh), jnp.float32)
    _, o = lax.scan(step, state0, (qc, kc, vc))
    return o.transpose(1, 0, 3, 2, 4).reshape(bsz, seq, nh, dh)


def hier_moe(h, router_g, router_gb, router_e, router_eb, w_gate, w_up, w_down):
    bsz, seq, d = h.shape
    t = h.reshape(-1, d)
    n_tok = t.shape[0]
    g_logits = (t @ router_g).astype(jnp.float32) + router_gb.astype(jnp.float32)
    g_prob = jax.nn.softmax(g_logits, axis=-1)
    g_idx = jnp.argmax(g_logits, axis=-1)
    g_w = jnp.take_along_axis(g_prob, g_idx[:, None], axis=1)
    e_logits = ((t @ router_e).astype(jnp.float32) + router_eb.astype(jnp.float32)
                ).reshape(n_tok, N_GROUPS, EXPERTS_PER_GROUP)
    e_logits = jnp.take_along_axis(e_logits, g_idx[:, None, None], axis=1)[:, 0]
    top_v, top_i = lax.top_k(e_logits, TOP_K_INNER)
    top_w = jax.nn.softmax(top_v, axis=-1) * g_w
    expert_id = g_idx[:, None] * EXPERTS_PER_GROUP + top_i
    combine = jnp.sum(jax.nn.one_hot(expert_id, N_EXPERTS, dtype=jnp.float32)
                      * top_w[..., None], axis=1)
    y = jnp.zeros((n_tok, d), jnp.float32)
    for e in range(N_EXPERTS):
        he = jax.nn.silu(t @ w_gate[e]) * (t @ w_up[e])
        y = y + combine[:, e:e + 1] * (he @ w_down[e]).astype(jnp.float32)
    return y.astype(h.dtype).reshape(bsz, seq, d)


def hybrid_layer(x, norm1, w_in, a_ln_g, a_ln_b, a_ws, a_bs, ret_gn, w_out, norm2,
                 router_g, router_gb, router_e, router_eb, w_gate, w_up, w_down):
    bsz, seq, _ = x.shape
    h = rms_norm(x, norm1)
    z = h @ w_in
    sizes = [2 * A_WIDTH, B_WIDTH, B_WIDTH, B_WIDTH, C_WIDTH, C_WIDTH, C_WIDTH, C_WIDTH]
    za, bq, bk, bv, cq, ck, cv, cg = jnp.split(z, np.cumsum(sizes)[:-1].tolist(), axis=-1)

    def heads(t, n):
        return t.reshape(bsz, seq, n, HEAD_DIM)

    a_out = causal_spatial_gating(jax.nn.gelu(za), a_ln_g, a_ln_b, a_ws, a_bs)
    b_out = dilated_attention(heads(bq, B_HEADS), heads(bk, B_HEADS), heads(bv, B_HEADS))
    b_out = b_out.reshape(bsz, seq, B_WIDTH).astype(x.dtype)
    pos = jnp.arange(seq)
    ro = retention(rotary(heads(cq, C_HEADS), pos), rotary(heads(ck, C_HEADS), pos),
                   heads(cv, C_HEADS))
    mu = jnp.mean(ro, -1, keepdims=True)
    var = jnp.mean(jnp.square(ro - mu), -1, keepdims=True)
    on = ((ro - mu) * lax.rsqrt(var + EPS)).reshape(bsz, seq, C_WIDTH) * ret_gn.astype(jnp.float32)
    c_out = (jax.nn.silu(cg.astype(jnp.float32)) * on).astype(x.dtype)

    x = x + jnp.concatenate([a_out, b_out, c_out], axis=-1) @ w_out
    x = x + hier_moe(rms_norm(x, norm2), router_g, router_gb, router_e, router_eb,
                     w_gate, w_up, w_down)
    return x


def setup_inputs(seed: int = 0) -> dict:
    key = jax.random.key(seed)
    ks = jax.random.split(key, 20)
    L, D = DEPTH, D_MODEL

    def nrm(k, shape, scale):
        return scale * jax.random.normal(k, shape, jnp.float32)

    return {
        'x': nrm(ks[0], (BATCH, SEQ, D), 1.0),
        'norm1': 1.0 + nrm(ks[1], (L, D), 0.05),
        'w_in': nrm(ks[2], (L, D, IN_COLS), D ** -0.5),
        'a_ln_g': 1.0 + nrm(ks[3], (L, A_WIDTH), 0.05),
        'a_ln_b': nrm(ks[4], (L, A_WIDTH), 0.02),
        'a_ws': nrm(ks[5], (L, A_GROUPS, CHUNK, CHUNK), CHUNK ** -0.5),
        'a_bs': 1.0 + nrm(ks[6], (L, A_GROUPS, CHUNK), 0.1),
        'ret_gn': 1.0 + nrm(ks[7], (L, C_WIDTH), 0.05),
        'w_out': nrm(ks[8], (L, D_MIX, D), D_MIX ** -0.5),
        'norm2': 1.0 + nrm(ks[9], (L, D), 0.05),
        'router_g': nrm(ks[10], (L, D, N_GROUPS), D ** -0.5),
        'router_gb': nrm(ks[11], (L, N_GROUPS), 0.01),
        'router_e': nrm(ks[12], (L, D, N_EXPERTS), D ** -0.5),
        'router_eb': nrm(ks[13], (L, N_EXPERTS), 0.01),
        'w_gate': nrm(ks[14], (L, N_EXPERTS, D, EXPERT_HIDDEN), D ** -0.5),
        'w_up': nrm(ks[15], (L, N_EXPERTS, D, EXPERT_HIDDEN), D ** -0.5),
        'w_down': nrm(ks[16], (L, N_EXPERTS, EXPERT_HIDDEN, D), EXPERT_HIDDEN ** -0.5),
        'final_norm': 1.0 + nrm(ks[17], (D,), 0.05),
    }


def reference(x, norm1, w_in, a_ln_g, a_ln_b, a_ws, a_bs, ret_gn, w_out, norm2,
              router_g, router_gb, router_e, router_eb, w_gate, w_up, w_down, final_norm):
    for l in range(DEPTH):
        x = hybrid_layer(x, norm1[l], w_in[l], a_ln_g[l], a_ln_b[l], a_ws[l], a_bs[l],
                         ret_gn[l], w_out[l], norm2[l], router_g[l], router_gb[l],
                         router_e[l], router_eb[l], w_gate[l], w_up[l], w_down[l])
    return rms_norm(x, final_norm)
```

```python
import functools

import numpy as np
import jax
import jax.numpy as jnp
from jax import lax
from jax.experimental import pallas as pl
from jax.experimental.pallas import tpu as pltpu

F32 = jnp.float32
BF16 = jnp.bfloat16
I32 = jnp.int32

EPS = 1e-6
HEAD_DIM = 64
SGU_CHUNK = 128
RET_CHUNK = 128
ATTN_BLOCK = 128
DILATIONS = (1, 4, 16)
ROPE_BASE = 10000.0
N_GROUPS = 4
EXPERTS_PER_GROUP = 4
N_EXPERTS = N_GROUPS * EXPERTS_PER_GROUP
LANES = 128
SUBLANES = 8
MOE_TILE = 256
VMEM_LIMIT = 56 * 1024 * 1024


def _params(sem, vmem=VMEM_LIMIT):
    return pltpu.CompilerParams(dimension_semantics=sem, vmem_limit_bytes=vmem)


def _rms(x, g):
    return x * lax.rsqrt(jnp.mean(x * x, axis=-1, keepdims=True) + EPS) * g


def _split_bf16(a):
    hi = a.astype(BF16)
    lo = (a - hi.astype(F32)).astype(BF16)
    return hi, lo


def _inproj_body(x_ref, g_ref, w_ref, z_ref, *, n_chunk):
    hb = _rms(x_ref[...], g_ref[...]).astype(BF16)
    for n0 in range(0, w_ref.shape[1], n_chunk):
        z_ref[:, n0:n0 + n_chunk] = jnp.dot(hb, w_ref[:, n0:n0 + n_chunk], preferred_element_type=F32)


def _inproj(x, g, w_bf16, tm=512, n_chunk=512):
    t, d = x.shape
    n = w_bf16.shape[1]
    return pl.pallas_call(
        functools.partial(_inproj_body, n_chunk=n_chunk),
        grid=(t // tm,),
        in_specs=[pl.BlockSpec((tm, d), lambda i: (i, 0)),
                  pl.BlockSpec((1, d), lambda i: (0, 0)),
                  pl.BlockSpec((d, n), lambda i: (0, 0))],
        out_specs=pl.BlockSpec((tm, n), lambda i: (i, 0)),
        out_shape=jax.ShapeDtypeStruct((t, n), F32),
        compiler_params=_params(("parallel",)),
        name="inproj",
    )(x, g.reshape(1, d), w_bf16)


def _gelu_tanh(x):
    return 0.5 * x * (1.0 + jnp.tanh(np.sqrt(2.0 / np.pi).astype(np.float32) * (x + 0.044715 * (x * x * x))))


def _sgu_body(za_ref, lng_ref, lnb_ref, wcat_ref, bias_ref, o_ref):
    ga = _gelu_tanh(za_ref[...])
    aw = ga.shape[1] // 2
    u = ga[:, :aw]
    v = ga[:, aw:]
    mu = jnp.mean(v, axis=-1, keepdims=True)
    dv = v - mu
    var = jnp.mean(dv * dv, axis=-1, keepdims=True)
    vn = dv * lax.rsqrt(var + EPS) * lng_ref[...] + lnb_ref[...]
    group = lax.broadcasted_iota(I32, (1, aw), 1) // HEAD_DIM
    n_groups = aw // HEAD_DIM
    for c in range(ga.shape[0] // SGU_CHUNK):
        rows = slice(c * SGU_CHUNK, (c + 1) * SGU_CHUNK)
        vc = vn[rows]
        stack = jnp.concatenate([jnp.where(group == g, vc, 0.0) for g in range(n_groups)], axis=0)
        s = jnp.dot(wcat_ref[...], stack.astype(BF16), preferred_element_type=F32) + bias_ref[...]
        o_ref[rows, :] = u[rows] * s


def _sgu(z, ln_g, ln_b, w_s, b_s, tm=512):
    t = z.shape[0]
    n_groups = w_s.shape[0]
    aw = n_groups * HEAD_DIM
    causal = jnp.tril(jnp.ones((SGU_CHUNK, SGU_CHUNK), dtype=bool))
    wcat = jnp.where(causal[None], w_s, 0.0).transpose(1, 0, 2).reshape(SGU_CHUNK, n_groups * SGU_CHUNK)
    bias = jnp.repeat(b_s.T, HEAD_DIM, axis=1)
    return pl.pallas_call(
        _sgu_body,
        grid=(t // tm,),
        in_specs=[pl.BlockSpec((tm, 2 * aw), lambda i: (i, 0)),
                  pl.BlockSpec((1, aw), lambda i: (0, 0)),
                  pl.BlockSpec((1, aw), lambda i: (0, 0)),
                  pl.BlockSpec(wcat.shape, lambda i: (0, 0)),
                  pl.BlockSpec(bias.shape, lambda i: (0, 0))],
        out_specs=pl.BlockSpec((tm, aw), lambda i: (i, 0)),
        out_shape=jax.ShapeDtypeStruct((t, aw), F32),
        compiler_params=_params(("parallel",)),
        name="sgu",
    )(z, ln_g.reshape(1, aw), ln_b.reshape(1, aw), wcat.astype(BF16), bias)


def _attn_body(q_ref, k_ref, v_ref, o_ref, ob_scr, lse_scr, *, seq):
    q2, k2, v2 = q_ref.at[0], k_ref.at[0], v_ref.at[0]
    blk = ATTN_BLOCK
    head0 = lax.broadcasted_iota(I32, (1, LANES), 1) < HEAD_DIM
    ri = lax.broadcasted_iota(I32, (blk, blk), 0)
    ci = lax.broadcasted_iota(I32, (blk, blk), 1)
    cur_ok = ci <= ri
    prev_ok = ci >= ri
    both_ok = jnp.concatenate([prev_ok, cur_ok], axis=1)
    scale = HEAD_DIM ** -0.5

    def rows(start, n, d):
        return pl.ds(start, n) if d == 1 else pl.ds(start, n, stride=d)

    def block(p, d, base, has_prev):
        qb = q2[rows(base, blk, d), :] * scale
        if has_prev:
            kk = k2[rows(base - blk * d, 2 * blk, d), :]
            vv = v2[rows(base - blk * d, 2 * blk, d), :]
            ok = both_ok
        else:
            kk = k2[rows(base, blk, d), :]
            vv = v2[rows(base, blk, d), :]
            ok = cur_ok
        kb = kk.astype(BF16)
        vb = vv.astype(BF16)
        outs, lses = [], []
        for sel in (head0, jnp.logical_not(head0)):
            qh = jnp.where(sel, qb, 0.0).astype(BF16)
            s = lax.dot_general(qh, kb, (((1,), (1,)), ((), ())), preferred_element_type=F32)
            s = jnp.where(ok, s, -jnp.inf)
            m = jnp.max(s, axis=-1, keepdims=True)
            e = jnp.exp(s - m)
            l = jnp.sum(e, axis=-1, keepdims=True)
            outs.append(jnp.dot(e.astype(BF16), vb, preferred_element_type=F32) / l)
            lses.append(m + jnp.log(l))
        ob_scr[p, rows(base, blk, d), :] = jnp.where(head0, outs[0], outs[1])
        lse_scr[p, rows(base, blk, d), :] = jnp.where(head0, lses[0], lses[1])

    for p, d in enumerate(DILATIONS):
        n_blocks = seq // (blk * d)

        def stream(r, carry, p=p, d=d, n_blocks=n_blocks):
            block(p, d, r, False)

            def later(n, c):
                base = n * (blk * d) + r
                if d == 1:
                    base = pl.multiple_of(base, blk)
                block(p, d, base, True)
                return c

            return lax.fori_loop(1, n_blocks, later, carry)

        lax.fori_loop(0, d, stream, 0)

    step = 256

    def mix(i, carry):
        sl = pl.ds(pl.multiple_of(i * step, step), step)
        ls = [lse_scr[p, sl, :] for p in range(len(DILATIONS))]
        m = functools.reduce(jnp.maximum, ls)
        es = [jnp.exp(l - m) for l in ls]
        num = sum(e * ob_scr[p, sl, :] for p, e in enumerate(es))
        o_ref[0, sl, :] = num / sum(es)
        return carry

    lax.fori_loop(0, seq // step, mix, 0)


def _attn(z3, q_col, n_heads):
    b, s, _ = z3.shape
    n_pairs = n_heads * HEAD_DIM // LANES

    def spec(off):
        return pl.BlockSpec((1, s, LANES), lambda i, j: (i, 0, off + j))

    return pl.pallas_call(
        functools.partial(_attn_body, seq=s),
        grid=(b, n_pairs),
        in_specs=[spec(q_col), spec(q_col + n_pairs), spec(q_col + 2 * n_pairs)],
        out_specs=pl.BlockSpec((1, s, LANES), lambda i, j: (i, 0, j)),
        out_shape=jax.ShapeDtypeStruct((b, s, n_pairs * LANES), F32),
        scratch_shapes=[pltpu.VMEM((len(DILATIONS), s, LANES), F32),
                        pltpu.VMEM((len(DILATIONS), s, LANES), F32)],
        compiler_params=_params(("parallel", "parallel")),
        name="dilated_attn",
    )(z3, z3, z3)


def _ret_body(q_ref, k_ref, v_ref, g_ref, cos_ref, sa_ref, sb_ref, dec_ref, qdec_ref, kdec_ref, cdec_ref,
              avg_ref, gn_ref, o_ref, state):
    cw = q_ref.shape[2]
    n_heads = cw // HEAD_DIM
    head = lax.broadcasted_iota(I32, (1, cw), 1) // HEAD_DIM
    hr = lax.broadcasted_iota(I32, (cw, cw), 0) // HEAD_DIM
    hc = lax.broadcasted_iota(I32, (cw, cw), 1) // HEAD_DIM
    same_head = hr == hc

    @pl.when(pl.program_id(1) == 0)
    def _():
        state[...] = jnp.zeros_like(state)

    def rotary(x, cos, sa, sb):
        half = HEAD_DIM // 2
        return x * cos + pltpu.roll(x, half, 1) * sa + pltpu.roll(x, cw - half, 1) * sb

    def head_mean(a):
        hi, lo = _split_bf16(a)
        return (jnp.dot(hi, avg_ref[...], preferred_element_type=F32)
                + jnp.dot(lo, avg_ref[...], preferred_element_type=F32))

    for c in range(q_ref.shape[1] // RET_CHUNK):
        rows = slice(c * RET_CHUNK, (c + 1) * RET_CHUNK)
        cos, sa, sb = cos_ref[rows, :], sa_ref[rows, :], sb_ref[rows, :]
        qr = rotary(q_ref[0, rows, :], cos, sa, sb)
        kr = rotary(k_ref[0, rows, :], cos, sa, sb) * (HEAD_DIM ** -0.5)
        v = v_ref[0, rows, :]
        krb = kr.astype(BF16)
        inner = [lax.dot_general(jnp.where(head == h, qr, 0.0).astype(BF16), krb, (((1,), (1,)), ((), ())),
                                 preferred_element_type=F32) for h in range(n_heads)]
        inner = jnp.concatenate(inner, axis=1) * dec_ref[...]
        vstack = jnp.concatenate([jnp.where(head == h, v, 0.0) for h in range(n_heads)], axis=0)
        st = state[...]
        o = (jnp.dot(inner.astype(BF16), vstack.astype(BF16), preferred_element_type=F32)
             + jnp.dot((qr * qdec_ref[...]).astype(BF16), st.astype(BF16), preferred_element_type=F32))
        ktv = lax.dot_general((kr * kdec_ref[...]).astype(BF16), v.astype(BF16), (((0,), (0,)), ((), ())),
                              preferred_element_type=F32)
        state[...] = st * cdec_ref[...] + jnp.where(same_head, ktv, 0.0)
        mu = head_mean(o)
        dd = o - mu
        var = head_mean(dd * dd)
        on = dd * lax.rsqrt(var + EPS) * gn_ref[...]
        gate = g_ref[0, rows, :]
        o_ref[0, rows, :] = gate * jax.nn.sigmoid(gate) * on


def _ret_tables(seq, n_heads):
    half = HEAD_DIM // 2
    inv = ROPE_BASE ** (-jnp.arange(half, dtype=F32) / half)
    ang = jnp.arange(seq, dtype=F32)[:, None] * inv[None]
    cos, sin = jnp.cos(ang), jnp.sin(ang)
    zero = jnp.zeros_like(sin)
    tile = lambda a: jnp.tile(a, (1, n_heads))
    cos_t = tile(jnp.concatenate([cos, cos], axis=1))
    sa_t = tile(jnp.concatenate([zero, sin], axis=1))
    sb_t = tile(jnp.concatenate([-sin, zero], axis=1))
    log_g = jnp.log(1.0 - 2.0 ** (-5.0 - jnp.arange(n_heads, dtype=F32)))
    c = RET_CHUNK
    idx = jnp.arange(c)
    diff = idx[:, None] - idx[None, :]
    decay_in = jnp.where(diff >= 0, jnp.exp(log_g[:, None, None] * jnp.maximum(diff, 0)[None]), 0.0)
    dec = decay_in.transpose(1, 0, 2).reshape(c, n_heads * c)
    lane_head = jnp.repeat(jnp.arange(n_heads), HEAD_DIM)
    qdec = jnp.exp(log_g[lane_head][None, :] * (idx + 1)[:, None].astype(F32))
    kdec = jnp.exp(log_g[lane_head][None, :] * (c - 1 - idx)[:, None].astype(F32))
    same = lane_head[:, None] == lane_head[None, :]
    cdec = jnp.where(same, jnp.exp(log_g * c)[lane_head][:, None], 0.0)
    avg = jnp.where(same, 1.0 / HEAD_DIM, 0.0).astype(BF16)
    return cos_t, sa_t, sb_t, dec, qdec, kdec, cdec, avg


def _retention(z3, col, n_heads, gn, tc=512):
    b, s, _ = z3.shape
    cw = n_heads * HEAD_DIM
    cos_t, sa_t, sb_t, dec, qdec, kdec, cdec, avg = _ret_tables(s, n_heads)

    def zspec(off):
        return pl.BlockSpec((1, tc, cw), lambda i, j: (i, j, off))

    tab = pl.BlockSpec((tc, cw), lambda i, j: (j, 0))
    full = lambda a: pl.BlockSpec(a.shape, lambda i, j: (0, 0))
    gn2 = gn.reshape(1, cw)
    return pl.pallas_call(
        _ret_body,
        grid=(b, s // tc),
        in_specs=[zspec(col), zspec(col + 1), zspec(col + 2), zspec(col + 3), tab, tab, tab,
                  full(dec), full(qdec), full(kdec), full(cdec), full(avg), full(gn2)],
        out_specs=pl.BlockSpec((1, tc, cw), lambda i, j: (i, j, 0)),
        out_shape=jax.ShapeDtypeStruct((b, s, cw), F32),
        scratch_shapes=[pltpu.VMEM((cw, cw), F32)],
        compiler_params=_params(("parallel", "arbitrary")),
        name="retention",
    )(z3, z3, z3, z3, cos_t, sa_t, sb_t, dec, qdec, kdec, cdec, avg, gn2)


def _outproj_body(x_ref, a_ref, b_ref, c_ref, wo_ref, g2_ref, rhi_ref, rlo_ref, rb_ref, x1_ref, route_ref):
    aw, bw = a_ref.shape[1], b_ref.shape[1]
    y = (jnp.dot(a_ref[...].astype(BF16), wo_ref[0:aw, :], preferred_element_type=F32)
         + jnp.dot(b_ref[...].astype(BF16), wo_ref[aw:aw + bw, :], preferred_element_type=F32)
         + jnp.dot(c_ref[...].astype(BF16), wo_ref[aw + bw:, :], preferred_element_type=F32))
    x1 = x_ref[...] + y
    x1_ref[...] = x1
    hi, lo = _split_bf16(_rms(x1, g2_ref[...]))
    logits = (jnp.dot(hi, rhi_ref[...], preferred_element_type=F32)
              + jnp.dot(lo, rhi_ref[...], preferred_element_type=F32)
              + jnp.dot(hi, rlo_ref[...], preferred_element_type=F32)) + rb_ref[...]
    lt = logits.T
    row = lambda i: lt[i:i + 1, :]
    best, gi = row(0), jnp.zeros_like(row(0), dtype=I32)
    for i in range(1, N_GROUPS):
        up = row(i) > best
        best = jnp.where(up, row(i), best)
        gi = jnp.where(up, i, gi)
    g_w = 1.0 / sum(jnp.exp(row(i) - best) for i in range(N_GROUPS))
    el = []
    for j in range(EXPERTS_PER_GROUP):
        e = row(N_GROUPS + (N_GROUPS - 1) * EXPERTS_PER_GROUP + j)
        for g in range(N_GROUPS - 2, -1, -1):
            e = jnp.where(gi == g, row(N_GROUPS + g * EXPERTS_PER_GROUP + j), e)
        el.append(e)
    v1, i1 = el[0], jnp.zeros_like(gi)
    for j in range(1, EXPERTS_PER_GROUP):
        up = el[j] > v1
        v1 = jnp.where(up, el[j], v1)
        i1 = jnp.where(up, j, i1)
    v2, i2 = jnp.full_like(v1, -jnp.inf), jnp.zeros_like(gi)
    for j in range(EXPERTS_PER_GROUP):
        up = jnp.logical_and(i1 != j, el[j] > v2)
        v2 = jnp.where(up, el[j], v2)
        i2 = jnp.where(up, j, i2)
    e21 = jnp.exp(v2 - v1)
    w1 = g_w / (1.0 + e21)
    w2 = g_w * e21 / (1.0 + e21)
    e1 = (gi * EXPERTS_PER_GROUP + i1).astype(F32)
    e2 = (gi * EXPERTS_PER_GROUP + i2).astype(F32)
    zero = jnp.zeros_like(w1)
    route_ref[...] = jnp.concatenate([e1, e2, w1, w2, zero, zero, zero, zero], axis=0)


def _outproj(x, a, b, c, wo_bf16, g2, router_g, router_gb, router_e, router_eb, tm=512):
    t, d = x.shape
    r = jnp.concatenate([router_g, router_e], axis=1)
    r = jnp.pad(r, ((0, 0), (0, LANES - r.shape[1])))
    rhi = r.astype(BF16)
    rlo = (r - rhi.astype(F32)).astype(BF16)
    rb = jnp.pad(jnp.concatenate([router_gb, router_eb]), (0, LANES - N_GROUPS - N_EXPERTS)).reshape(1, LANES)
    row_spec = lambda w: pl.BlockSpec((tm, w), lambda i: (i, 0))
    full = lambda arr: pl.BlockSpec(arr.shape, lambda i: (0, 0))
    g2r = g2.reshape(1, d)
    return pl.pallas_call(
        _outproj_body,
        grid=(t // tm,),
        in_specs=[row_spec(d), row_spec(a.shape[1]), row_spec(b.shape[1]), row_spec(c.shape[1]),
                  full(wo_bf16), full(g2r), full(rhi), full(rlo), full(rb)],
        out_specs=[row_spec(d), pl.BlockSpec((8, tm), lambda i: (0, i))],
        out_shape=[jax.ShapeDtypeStruct((t, d), F32), jax.ShapeDtypeStruct((8, t), F32)],
        compiler_params=_params(("parallel",)),
        name="outproj_route",
    )(x, a, b, c, wo_bf16, g2r, rhi, rlo, rb)


def _moe_plan(eid, n_tiles):
    flat = eid.reshape(-1)
    onehot = (flat[:, None] == jnp.arange(N_EXPERTS, dtype=I32)[None, :]).astype(I32)
    csum = jnp.cumsum(onehot, axis=0)
    counts = csum[-1]
    padded = ((counts + MOE_TILE - 1) // MOE_TILE) * MOE_TILE
    ends = jnp.cumsum(padded)
    starts = ends - padded
    pos = jnp.sum(onehot * (csum - 1 + starts[None, :]), axis=1).reshape(eid.shape)
    tile_start = jnp.arange(n_tiles, dtype=I32) * MOE_TILE
    tile_expert = jnp.sum((ends[None, :] <= tile_start[:, None]).astype(I32), axis=1)
    tile_expert = jnp.minimum(tile_expert, N_EXPERTS - 1).astype(I32)
    used = (ends[-1] // MOE_TILE).astype(I32).reshape(1)
    pad_start = (starts + counts).astype(I32)
    pad_count = (padded - counts).astype(I32)
    return pos.astype(I32), tile_expert, used, pad_start, pad_count


def _row_copy(src, src_row, dst, dst_row, sem):
    return pltpu.make_async_copy(src.at[pl.ds(src_row, 1)], dst.at[pl.ds(dst_row, 1)], sem)


def _dispatch_body(pad_start_ref, pad_count_ref, used_ref, pos_ref, x_hbm, xs_hbm, zeros, sem, fill_sem,
                   *, tm, n_tiles):
    i = pl.program_id(0)

    def fill(act):
        def expert_pad(e, c):
            start, n = pad_start_ref[e], pad_count_ref[e]
            head = (SUBLANES - (start & (SUBLANES - 1))) & (SUBLANES - 1)
            for j in range(SUBLANES - 1):
                @pl.when(j < head)
                def _(j=j):
                    act(_row_copy(zeros, 0, xs_hbm, start + j, fill_sem))

            off, body = start + head, n - head
            bit = MOE_TILE // 2
            while bit >= SUBLANES:
                take = (body & bit) != 0

                @pl.when(take)
                def _(off=off, bit=bit):
                    dst = xs_hbm.at[pl.ds(pl.multiple_of(off, SUBLANES), bit)]
                    act(pltpu.make_async_copy(zeros.at[pl.ds(0, bit)], dst, fill_sem))

                off = off + jnp.where(take, bit, 0)
                bit //= 2
            return c

        lax.fori_loop(0, N_EXPERTS, expert_pad, 0)

        def unused_tile(j, c):
            dst = xs_hbm.at[pl.ds(pl.multiple_of(j * MOE_TILE, MOE_TILE), MOE_TILE)]
            act(pltpu.make_async_copy(zeros, dst, fill_sem))
            return c

        lax.fori_loop(used_ref[0], n_tiles, unused_tile, 0)

    @pl.when(i == 0)
    def _():
        zeros[...] = jnp.zeros_like(zeros)
        fill(lambda cp: cp.start())
        fill(lambda cp: cp.wait())

    def issue(r, c):
        for k in range(2):
            _row_copy(x_hbm, i * tm + r, xs_hbm, pos_ref[0, k, r], sem).start()
        return c

    lax.fori_loop(0, tm, issue, 0)

    def drain(r, c):
        for k in range(2):
            _row_copy(x_hbm, 0, xs_hbm, 0, sem).wait()
        return c

    lax.fori_loop(0, tm, drain, 0)


def _dispatch(x1, pos, n_tiles, pad_start, pad_count, used, tm=256):
    t, d = x1.shape
    pos3 = pos.reshape(2, t // tm, tm).transpose(1, 0, 2)
    grid_spec = pltpu.PrefetchScalarGridSpec(
        num_scalar_prefetch=3,
        grid=(t // tm,),
        in_specs=[pl.BlockSpec((1, 2, tm), lambda i, *_: (i, 0, 0), memory_space=pltpu.SMEM),
                  pl.BlockSpec(memory_space=pl.ANY)],
        out_specs=pl.BlockSpec(memory_space=pl.ANY),
        scratch_shapes=[pltpu.VMEM((MOE_TILE, d), F32), pltpu.SemaphoreType.DMA, pltpu.SemaphoreType.DMA],
    )
    return pl.pallas_call(
        functools.partial(_dispatch_body, tm=tm, n_tiles=n_tiles),
        grid_spec=grid_spec,
        out_shape=jax.ShapeDtypeStruct((n_tiles * MOE_TILE, d), F32),
        compiler_params=_params(("arbitrary",)),
        name="moe_dispatch",
    )(pad_start, pad_count, used, pos3, x1)


def _ffn_body(te_ref, used_ref, xs_ref, g2_ref, wgu_ref, wd_ref, ys_ref):
    live = pl.program_id(0) < used_ref[0]

    @pl.when(live)
    def _():
        hb = _rms(xs_ref[...], g2_ref[...]).astype(BF16)
        gu = jnp.dot(hb, wgu_ref[...], preferred_element_type=F32)
        hid = wd_ref.shape[0]
        gate, up = gu[:, :hid], gu[:, hid:]
        act = (gate * jax.nn.sigmoid(gate) * up).astype(BF16)
        ys_ref[...] = jnp.dot(act, wd_ref[...], preferred_element_type=F32)

    @pl.when(jnp.logical_not(live))
    def _():
        ys_ref[...] = jnp.zeros_like(ys_ref)


def _ffn(xs, g2, wgu_bf16, wd_bf16, tile_expert, used):
    n_rows, d = xs.shape
    n_tiles = n_rows // MOE_TILE
    hid = wd_bf16.shape[1]
    live = lambda i, te, used: jnp.minimum(i, used[0] - 1)
    grid_spec = pltpu.PrefetchScalarGridSpec(
        num_scalar_prefetch=2,
        grid=(n_tiles,),
        in_specs=[pl.BlockSpec((MOE_TILE, d), lambda i, te, used: (live(i, te, used), 0)),
                  pl.BlockSpec((1, d), lambda i, te, used: (0, 0)),
                  pl.BlockSpec((None, d, 2 * hid), lambda i, te, used: (te[i], 0, 0)),
                  pl.BlockSpec((None, hid, d), lambda i, te, used: (te[i], 0, 0))],
        out_specs=pl.BlockSpec((MOE_TILE, d), lambda i, te, used: (i, 0)),
    )
    return pl.pallas_call(
        _ffn_body,
        grid_spec=grid_spec,
        out_shape=jax.ShapeDtypeStruct((n_rows, d), F32),
        compiler_params=_params(("arbitrary",)),
        name="moe_experts",
    )(tile_expert, used, xs, g2.reshape(1, d), wgu_bf16, wd_bf16)


def _combine_body(pos_ref, ys_hbm, x1_ref, w_ref, gf_ref, o_ref, buf, sem, *, tm, final_norm):
    def issue(r, c):
        for k in range(2):
            pltpu.make_async_copy(ys_hbm.at[pl.ds(pos_ref[0, k, r], 1)], buf.at[k, pl.ds(r, 1)], sem).start()
        return c

    lax.fori_loop(0, tm, issue, 0)

    def drain(r, c):
        for k in range(2):
            pltpu.make_async_copy(ys_hbm.at[pl.ds(0, 1)], buf.at[k, pl.ds(0, 1)], sem).wait()
        return c

    lax.fori_loop(0, tm, drain, 0)
    w = w_ref[...]
    y = x1_ref[...] + w[:, 0:1] * buf[0] + w[:, 1:2] * buf[1]
    if final_norm:
        y = _rms(y, gf_ref[...])
    o_ref[...] = y


def _combine(ys, x1, pos, w2, gf, final_norm, tm=256):
    t, d = x1.shape
    pos3 = pos.reshape(2, t // tm, tm).transpose(1, 0, 2)
    return pl.pallas_call(
        functools.partial(_combine_body, tm=tm, final_norm=final_norm),
        grid=(t // tm,),
        in_specs=[pl.BlockSpec((1, 2, tm), lambda i: (i, 0, 0), memory_space=pltpu.SMEM),
                  pl.BlockSpec(memory_space=pl.ANY),
                  pl.BlockSpec((tm, d), lambda i: (i, 0)),
                  pl.BlockSpec((tm, 2), lambda i: (i, 0)),
                  pl.BlockSpec((1, d), lambda i: (0, 0))],
        out_specs=pl.BlockSpec((tm, d), lambda i: (i, 0)),
        out_shape=jax.ShapeDtypeStruct((t, d), F32),
        scratch_shapes=[pltpu.VMEM((2, tm, d), F32), pltpu.SemaphoreType.DMA],
        compiler_params=_params(("arbitrary",)),
        name="moe_combine",
    )(pos3, ys, x1, w2, gf.reshape(1, d))


def _layer(x, batch, seq, norm1, w_in, a_ln_g, a_ln_b, a_ws, a_bs, ret_gn, w_out, norm2,
           router_g, router_gb, router_e, router_eb, w_gate, w_up, w_down, final_gain, is_last):
    t, d = x.shape
    a_groups = a_ws.shape[0]
    aw = a_groups * HEAD_DIM
    c_heads = ret_gn.shape[0] // HEAD_DIM
    cw = c_heads * HEAD_DIM
    bw = w_in.shape[1] - 2 * aw - 4 * cw
    b_heads = (bw // 3) // HEAD_DIM
    z = _inproj(x, norm1, w_in.astype(BF16))
    z3 = z.reshape(batch, seq, z.shape[1])
    a_out = _sgu(z, a_ln_g, a_ln_b, a_ws, a_bs)
    b_out = _attn(z3, (2 * aw) // LANES, b_heads).reshape(t, bw // 3)
    c_out = _retention(z3, (2 * aw + bw) // cw, c_heads, ret_gn).reshape(t, cw)
    x1, route = _outproj(x, a_out, b_out, c_out, w_out.astype(BF16), norm2,
                         router_g, router_gb, router_e, router_eb)
    eid = route[0:2].astype(I32)
    w2 = route[2:4].T
    n_tiles = (2 * t) // MOE_TILE + N_EXPERTS
    pos, tile_expert, used, pad_start, pad_count = _moe_plan(eid, n_tiles)
    xs = _dispatch(x1, pos, n_tiles, pad_start, pad_count, used)
    wgu = jnp.concatenate([w_gate, w_up], axis=2).astype(BF16)
    ys = _ffn(xs, norm2, wgu, w_down.astype(BF16), tile_expert, used)
    return _combine(ys, x1, pos, w2, final_gain, is_last)


def kernel(x, norm1, w_in, a_ln_g, a_ln_b, a_ws, a_bs, ret_gn, w_out, norm2, router_g, router_gb, router_e,
           router_eb, w_gate, w_up, w_down, final_norm):
    batch, seq, d = x.shape
    depth = norm1.shape[0]
    h = x.reshape(batch * seq, d)
    for l in range(depth):
        h = _layer(h, batch, seq, norm1[l], w_in[l], a_ln_g[l], a_ln_b[l], a_ws[l], a_bs[l], ret_gn[l],
                   w_out[l], norm2[l], router_g[l], router_gb[l], router_e[l], router_eb[l],
                   w_gate[l], w_up[l], w_down[l], final_norm, l == depth - 1)
    return h.reshape(batch, seq, d)
```

```python
import functools

import numpy as np
import jax
import jax.numpy as jnp
from jax import lax
from jax.experimental import pallas as pl
from jax.experimental.pallas import tpu as pltpu

F32 = jnp.float32
BF16 = jnp.bfloat16
I32 = jnp.int32

EPS = 1e-6
HEAD_DIM = 64
SGU_CHUNK = 128
RET_CHUNK = 128
ATTN_BLOCK = 128
DILATIONS = (1, 4, 16)
ATTN_UNROLL = 4
ROPE_BASE = 10000.0
N_GROUPS = 4
EXPERTS_PER_GROUP = 4
N_EXPERTS = N_GROUPS * EXPERTS_PER_GROUP
LANES = 128
SUBLANES = 8
MOE_TILE = 256
ROW_UNROLL = 8
VMEM_LIMIT = 56 * 1024 * 1024


def _params(sem, vmem=VMEM_LIMIT):
    return pltpu.CompilerParams(dimension_semantics=sem, vmem_limit_bytes=vmem)


def _rms(x, g):
    return x * lax.rsqrt(jnp.mean(x * x, axis=-1, keepdims=True) + EPS) * g


def _split_bf16(a):
    hi = a.astype(BF16)
    lo = (a - hi.astype(F32)).astype(BF16)
    return hi, lo


def _inproj_body(x_ref, g_ref, w_ref, z_ref, *, n_chunk):
    hb = _rms(x_ref[...], g_ref[...]).astype(BF16)
    for n0 in range(0, w_ref.shape[1], n_chunk):
        z_ref[:, n0:n0 + n_chunk] = jnp.dot(hb, w_ref[:, n0:n0 + n_chunk], preferred_element_type=F32)


def _inproj(x, g, w_bf16, tm=512, n_chunk=512):
    t, d = x.shape
    n = w_bf16.shape[1]
    return pl.pallas_call(
        functools.partial(_inproj_body, n_chunk=n_chunk),
        grid=(t // tm,),
        in_specs=[pl.BlockSpec((tm, d), lambda i: (i, 0)),
                  pl.BlockSpec((1, d), lambda i: (0, 0)),
                  pl.BlockSpec((d, n), lambda i: (0, 0))],
        out_specs=pl.BlockSpec((tm, n), lambda i: (i, 0)),
        out_shape=jax.ShapeDtypeStruct((t, n), F32),
        compiler_params=_params(("parallel",)),
        name="inproj",
    )(x, g.reshape(1, d), w_bf16)


def _gelu_tanh(x):
    return 0.5 * x * (1.0 + jnp.tanh(np.sqrt(2.0 / np.pi).astype(np.float32) * (x + 0.044715 * (x * x * x))))


def _sgu_body(za_ref, lng_ref, lnb_ref, wcat_ref, bias_ref, o_ref):
    ga = _gelu_tanh(za_ref[...])
    aw = ga.shape[1] // 2
    u = ga[:, :aw]
    v = ga[:, aw:]
    mu = jnp.mean(v, axis=-1, keepdims=True)
    dv = v - mu
    var = jnp.mean(dv * dv, axis=-1, keepdims=True)
    vn = dv * lax.rsqrt(var + EPS) * lng_ref[...] + lnb_ref[...]
    group = lax.broadcasted_iota(I32, (1, aw), 1) // HEAD_DIM
    n_groups = aw // HEAD_DIM
    for c in range(ga.shape[0] // SGU_CHUNK):
        rows = slice(c * SGU_CHUNK, (c + 1) * SGU_CHUNK)
        vc = vn[rows]
        stack = jnp.concatenate([jnp.where(group == g, vc, 0.0) for g in range(n_groups)], axis=0)
        s = jnp.dot(wcat_ref[...], stack.astype(BF16), preferred_element_type=F32) + bias_ref[...]
        o_ref[rows, :] = u[rows] * s


def _sgu(z, ln_g, ln_b, w_s, b_s, tm=512):
    t = z.shape[0]
    n_groups = w_s.shape[0]
    aw = n_groups * HEAD_DIM
    causal = jnp.tril(jnp.ones((SGU_CHUNK, SGU_CHUNK), dtype=bool))
    wcat = jnp.where(causal[None], w_s, 0.0).transpose(1, 0, 2).reshape(SGU_CHUNK, n_groups * SGU_CHUNK)
    bias = jnp.repeat(b_s.T, HEAD_DIM, axis=1)
    return pl.pallas_call(
        _sgu_body,
        grid=(t // tm,),
        in_specs=[pl.BlockSpec((tm, 2 * aw), lambda i: (i, 0)),
                  pl.BlockSpec((1, aw), lambda i: (0, 0)),
                  pl.BlockSpec((1, aw), lambda i: (0, 0)),
                  pl.BlockSpec(wcat.shape, lambda i: (0, 0)),
                  pl.BlockSpec(bias.shape, lambda i: (0, 0))],
        out_specs=pl.BlockSpec((tm, aw), lambda i: (i, 0)),
        out_shape=jax.ShapeDtypeStruct((t, aw), F32),
        compiler_params=_params(("parallel",)),
        name="sgu",
    )(z, ln_g.reshape(1, aw), ln_b.reshape(1, aw), wcat.astype(BF16), bias)


def _attn_bias():
    blk = ATTN_BLOCK
    qi = np.arange(2 * blk)[:, None] % blk
    ci = np.arange(2 * blk)[None, :]
    first = (ci < blk) & (ci <= qi)
    later = np.where(ci < blk, ci >= qi, ci - blk <= qi)
    return np.where(np.stack([first, later]), 0.0, -np.inf).astype(np.float32)


def _attn_body(q_ref, k_ref, v_ref, bias_ref, o_ref, ob_scr, lse_scr, *, seq, unroll):
    q2, k2, v2 = q_ref.at[0], k_ref.at[0], v_ref.at[0]
    blk = ATTN_BLOCK
    head0 = lax.broadcasted_iota(I32, (1, LANES), 1) < HEAD_DIM
    scale = HEAD_DIM ** -0.5

    def rows(start, n, d):
        return pl.ds(start, n) if d == 1 else pl.ds(start, n, stride=d)

    def block(p, d, r, n):
        base = n * (blk * d) + r
        kstart = jnp.maximum(base - blk * d, r)
        if d == 1:
            base, kstart = pl.multiple_of(base, blk), pl.multiple_of(kstart, blk)
        qb = q2[rows(base, blk, d), :] * scale
        qs = jnp.concatenate([jnp.where(head0, qb, 0.0), jnp.where(head0, 0.0, qb)], axis=0).astype(BF16)
        kb = k2[rows(kstart, 2 * blk, d), :].astype(BF16)
        vb = v2[rows(kstart, 2 * blk, d), :].astype(BF16)
        s = lax.dot_general(qs, kb, (((1,), (1,)), ((), ())), preferred_element_type=F32)
        s = s + bias_ref[jnp.minimum(n, 1)]
        m = jnp.max(s, axis=-1, keepdims=True)
        e = jnp.exp(s - m)
        l = jnp.sum(e, axis=-1, keepdims=True)
        o = jnp.dot(e.astype(BF16), vb, preferred_element_type=F32) / l
        lse = m + jnp.log(l)
        ob_scr[p, rows(base, blk, d), :] = jnp.where(head0, o[:blk], o[blk:])
        lse_scr[p, rows(base, blk, d), :] = jnp.where(head0, lse[:blk], lse[blk:])

    for p, d in enumerate(DILATIONS):
        def group(i, carry, p=p, d=d):
            for u in range(unroll):
                b = i * unroll + u
                block(p, d, b & (d - 1), b >> (d.bit_length() - 1))
            return carry

        lax.fori_loop(0, seq // (blk * unroll), group, 0)

    step = 256

    def mix(i, carry):
        sl = pl.ds(pl.multiple_of(i * step, step), step)
        ls = [lse_scr[p, sl, :] for p in range(len(DILATIONS))]
        m = functools.reduce(jnp.maximum, ls)
        es = [jnp.exp(l - m) for l in ls]
        num = sum(e * ob_scr[p, sl, :] for p, e in enumerate(es))
        o_ref[0, sl, :] = num / sum(es)
        return carry

    lax.fori_loop(0, seq // step, mix, 0)


def _attn(z3, q_col, n_heads):
    b, s, _ = z3.shape
    n_pairs = n_heads * HEAD_DIM // LANES
    assert s % (ATTN_BLOCK * ATTN_UNROLL) == 0 and s % (2 * ATTN_BLOCK * max(DILATIONS)) == 0
    assert all(d & (d - 1) == 0 for d in DILATIONS)
    bias = jnp.asarray(_attn_bias())

    def spec(off):
        return pl.BlockSpec((1, s, LANES), lambda i, j: (i, 0, off + j))

    return pl.pallas_call(
        functools.partial(_attn_body, seq=s, unroll=ATTN_UNROLL),
        grid=(b, n_pairs),
        in_specs=[spec(q_col), spec(q_col + n_pairs), spec(q_col + 2 * n_pairs),
                  pl.BlockSpec(bias.shape, lambda i, j: (0, 0, 0))],
        out_specs=pl.BlockSpec((1, s, LANES), lambda i, j: (i, 0, j)),
        out_shape=jax.ShapeDtypeStruct((b, s, n_pairs * LANES), F32),
        scratch_shapes=[pltpu.VMEM((len(DILATIONS), s, LANES), F32),
                        pltpu.VMEM((len(DILATIONS), s, LANES), F32)],
        compiler_params=_params(("parallel", "parallel")),
        name="dilated_attn",
    )(z3, z3, z3, bias)


def _ret_body(q_ref, k_ref, v_ref, g_ref, cos_ref, sa_ref, sb_ref, dec_ref, qdec_ref, kdec_ref, cdec_ref,
              avg_ref, gn_ref, o_ref, state):
    cw = q_ref.shape[2]
    n_heads = cw // HEAD_DIM
    head = lax.broadcasted_iota(I32, (1, cw), 1) // HEAD_DIM
    hr = lax.broadcasted_iota(I32, (cw, cw), 0) // HEAD_DIM
    hc = lax.broadcasted_iota(I32, (cw, cw), 1) // HEAD_DIM
    same_head = hr == hc

    @pl.when(pl.program_id(1) == 0)
    def _():
        state[...] = jnp.zeros_like(state)

    def rotary(x, cos, sa, sb):
        half = HEAD_DIM // 2
        return x * cos + pltpu.roll(x, half, 1) * sa + pltpu.roll(x, cw - half, 1) * sb

    def head_mean(a):
        hi, lo = _split_bf16(a)
        return (jnp.dot(hi, avg_ref[...], preferred_element_type=F32)
                + jnp.dot(lo, avg_ref[...], preferred_element_type=F32))

    for c in range(q_ref.shape[1] // RET_CHUNK):
        rows = slice(c * RET_CHUNK, (c + 1) * RET_CHUNK)
        cos, sa, sb = cos_ref[rows, :], sa_ref[rows, :], sb_ref[rows, :]
        qr = rotary(q_ref[0, rows, :], cos, sa, sb)
        kr = rotary(k_ref[0, rows, :], cos, sa, sb) * (HEAD_DIM ** -0.5)
        v = v_ref[0, rows, :]
        krb = kr.astype(BF16)
        inner = [lax.dot_general(jnp.where(head == h, qr, 0.0).astype(BF16), krb, (((1,), (1,)), ((), ())),
                                 preferred_element_type=F32) for h in range(n_heads)]
        inner = jnp.concatenate(inner, axis=1) * dec_ref[...]
        vstack = jnp.concatenate([jnp.where(head == h, v, 0.0) for h in range(n_heads)], axis=0)
        st = state[...]
        o = (jnp.dot(inner.astype(BF16), vstack.astype(BF16), preferred_element_type=F32)
             + jnp.dot((qr * qdec_ref[...]).astype(BF16), st.astype(BF16), preferred_element_type=F32))
        ktv = lax.dot_general((kr * kdec_ref[...]).astype(BF16), v.astype(BF16), (((0,), (0,)), ((), ())),
                              preferred_element_type=F32)
        state[...] = st * cdec_ref[...] + jnp.where(same_head, ktv, 0.0)
        mu = head_mean(o)
        dd = o - mu
        var = head_mean(dd * dd)
        on = dd * lax.rsqrt(var + EPS) * gn_ref[...]
        gate = g_ref[0, rows, :]
        o_ref[0, rows, :] = gate * jax.nn.sigmoid(gate) * on


def _ret_tables(seq, n_heads):
    half = HEAD_DIM // 2
    inv = ROPE_BASE ** (-jnp.arange(half, dtype=F32) / half)
    ang = jnp.arange(seq, dtype=F32)[:, None] * inv[None]
    cos, sin = jnp.cos(ang), jnp.sin(ang)
    zero = jnp.zeros_like(sin)
    tile = lambda a: jnp.tile(a, (1, n_heads))
    cos_t = tile(jnp.concatenate([cos, cos], axis=1))
    sa_t = tile(jnp.concatenate([zero, sin], axis=1))
    sb_t = tile(jnp.concatenate([-sin, zero], axis=1))
    log_g = jnp.log(1.0 - 2.0 ** (-5.0 - jnp.arange(n_heads, dtype=F32)))
    c = RET_CHUNK
    idx = jnp.arange(c)
    diff = idx[:, None] - idx[None, :]
    decay_in = jnp.where(diff >= 0, jnp.exp(log_g[:, None, None] * jnp.maximum(diff, 0)[None]), 0.0)
    dec = decay_in.transpose(1, 0, 2).reshape(c, n_heads * c)
    lane_head = jnp.repeat(jnp.arange(n_heads), HEAD_DIM)
    qdec = jnp.exp(log_g[lane_head][None, :] * (idx + 1)[:, None].astype(F32))
    kdec = jnp.exp(log_g[lane_head][None, :] * (c - 1 - idx)[:, None].astype(F32))
    same = lane_head[:, None] == lane_head[None, :]
    cdec = jnp.where(same, jnp.exp(log_g * c)[lane_head][:, None], 0.0)
    avg = jnp.where(same, 1.0 / HEAD_DIM, 0.0).astype(BF16)
    return cos_t, sa_t, sb_t, dec, qdec, kdec, cdec, avg


def _retention(z3, col, n_heads, gn, tc=512):
    b, s, _ = z3.shape
    cw = n_heads * HEAD_DIM
    cos_t, sa_t, sb_t, dec, qdec, kdec, cdec, avg = _ret_tables(s, n_heads)

    def zspec(off):
        return pl.BlockSpec((1, tc, cw), lambda i, j: (i, j, off))

    tab = pl.BlockSpec((tc, cw), lambda i, j: (j, 0))
    full = lambda a: pl.BlockSpec(a.shape, lambda i, j: (0, 0))
    gn2 = gn.reshape(1, cw)
    return pl.pallas_call(
        _ret_body,
        grid=(b, s // tc),
        in_specs=[zspec(col), zspec(col + 1), zspec(col + 2), zspec(col + 3), tab, tab, tab,
                  full(dec), full(qdec), full(kdec), full(cdec), full(avg), full(gn2)],
        out_specs=pl.BlockSpec((1, tc, cw), lambda i, j: (i, j, 0)),
        out_shape=jax.ShapeDtypeStruct((b, s, cw), F32),
        scratch_shapes=[pltpu.VMEM((cw, cw), F32)],
        compiler_params=_params(("parallel", "arbitrary")),
        name="retention",
    )(z3, z3, z3, z3, cos_t, sa_t, sb_t, dec, qdec, kdec, cdec, avg, gn2)


def _outproj_body(x_ref, a_ref, b_ref, c_ref, wo_ref, g2_ref, rhi_ref, rlo_ref, rb_ref, x1_ref, route_ref):
    aw, bw = a_ref.shape[1], b_ref.shape[1]
    y = (jnp.dot(a_ref[...].astype(BF16), wo_ref[0:aw, :], preferred_element_type=F32)
         + jnp.dot(b_ref[...].astype(BF16), wo_ref[aw:aw + bw, :], preferred_element_type=F32)
         + jnp.dot(c_ref[...].astype(BF16), wo_ref[aw + bw:, :], preferred_element_type=F32))
    x1 = x_ref[...] + y
    x1_ref[...] = x1
    hi, lo = _split_bf16(_rms(x1, g2_ref[...]))
    logits = (jnp.dot(hi, rhi_ref[...], preferred_element_type=F32)
              + jnp.dot(lo, rhi_ref[...], preferred_element_type=F32)
              + jnp.dot(hi, rlo_ref[...], preferred_element_type=F32)) + rb_ref[...]
    lt = logits.T
    row = lambda i: lt[i:i + 1, :]
    best, gi = row(0), jnp.zeros_like(row(0), dtype=I32)
    for i in range(1, N_GROUPS):
        up = row(i) > best
        best = jnp.where(up, row(i), best)
        gi = jnp.where(up, i, gi)
    g_w = 1.0 / sum(jnp.exp(row(i) - best) for i in range(N_GROUPS))
    el = []
    for j in range(EXPERTS_PER_GROUP):
        e = row(N_GROUPS + (N_GROUPS - 1) * EXPERTS_PER_GROUP + j)
        for g in range(N_GROUPS - 2, -1, -1):
            e = jnp.where(gi == g, row(N_GROUPS + g * EXPERTS_PER_GROUP + j), e)
        el.append(e)
    v1, i1 = el[0], jnp.zeros_like(gi)
    for j in range(1, EXPERTS_PER_GROUP):
        up = el[j] > v1
        v1 = jnp.where(up, el[j], v1)
        i1 = jnp.where(up, j, i1)
    v2, i2 = jnp.full_like(v1, -jnp.inf), jnp.zeros_like(gi)
    for j in range(EXPERTS_PER_GROUP):
        up = jnp.logical_and(i1 != j, el[j] > v2)
        v2 = jnp.where(up, el[j], v2)
        i2 = jnp.where(up, j, i2)
    e21 = jnp.exp(v2 - v1)
    w1 = g_w / (1.0 + e21)
    w2 = g_w * e21 / (1.0 + e21)
    e1 = (gi * EXPERTS_PER_GROUP + i1).astype(F32)
    e2 = (gi * EXPERTS_PER_GROUP + i2).astype(F32)
    zero = jnp.zeros_like(w1)
    route_ref[...] = jnp.concatenate([e1, e2, w1, w2, zero, zero, zero, zero], axis=0)


def _outproj(x, a, b, c, wo_bf16, g2, router_g, router_gb, router_e, router_eb, tm=512):
    t, d = x.shape
    r = jnp.concatenate([router_g, router_e], axis=1)
    r = jnp.pad(r, ((0, 0), (0, LANES - r.shape[1])))
    rhi = r.astype(BF16)
    rlo = (r - rhi.astype(F32)).astype(BF16)
    rb = jnp.pad(jnp.concatenate([router_gb, router_eb]), (0, LANES - N_GROUPS - N_EXPERTS)).reshape(1, LANES)
    row_spec = lambda w: pl.BlockSpec((tm, w), lambda i: (i, 0))
    full = lambda arr: pl.BlockSpec(arr.shape, lambda i: (0, 0))
    g2r = g2.reshape(1, d)
    return pl.pallas_call(
        _outproj_body,
        grid=(t // tm,),
        in_specs=[row_spec(d), row_spec(a.shape[1]), row_spec(b.shape[1]), row_spec(c.shape[1]),
                  full(wo_bf16), full(g2r), full(rhi), full(rlo), full(rb)],
        out_specs=[row_spec(d), pl.BlockSpec((8, tm), lambda i: (0, i))],
        out_shape=[jax.ShapeDtypeStruct((t, d), F32), jax.ShapeDtypeStruct((8, t), F32)],
        compiler_params=_params(("parallel",)),
        name="outproj_route",
    )(x, a, b, c, wo_bf16, g2r, rhi, rlo, rb)


def _moe_plan(eid, n_tiles):
    flat = eid.reshape(-1)
    onehot = (flat[:, None] == jnp.arange(N_EXPERTS, dtype=I32)[None, :]).astype(I32)
    csum = jnp.cumsum(onehot, axis=0)
    counts = csum[-1]
    padded = ((counts + MOE_TILE - 1) // MOE_TILE) * MOE_TILE
    ends = jnp.cumsum(padded)
    starts = ends - padded
    pos = jnp.sum(onehot * (csum - 1 + starts[None, :]), axis=1).reshape(eid.shape)
    tile_start = jnp.arange(n_tiles, dtype=I32) * MOE_TILE
    tile_expert = jnp.sum((ends[None, :] <= tile_start[:, None]).astype(I32), axis=1)
    tile_expert = jnp.minimum(tile_expert, N_EXPERTS - 1).astype(I32)
    used = (ends[-1] // MOE_TILE).astype(I32).reshape(1)
    pad_start = (starts + counts).astype(I32)
    pad_count = (padded - counts).astype(I32)
    return pos.astype(I32), tile_expert, used, pad_start, pad_count


def _row_copy(src, src_row, dst, dst_row, sem):
    return pltpu.make_async_copy(src.at[pl.ds(src_row, 1)], dst.at[pl.ds(dst_row, 1)], sem)


def _dispatch_body(pad_start_ref, pad_count_ref, used_ref, pos_ref, x_ref, xs_hbm, zeros, sem, fill_sem,
                   *, tm, n_tiles):
    i = pl.program_id(0)

    def fill(act):
        def expert_pad(e, c):
            start, n = pad_start_ref[e], pad_count_ref[e]
            head = (SUBLANES - (start & (SUBLANES - 1))) & (SUBLANES - 1)
            for j in range(SUBLANES - 1):
                @pl.when(j < head)
                def _(j=j):
                    act(_row_copy(zeros, 0, xs_hbm, start + j, fill_sem))

            off, body = start + head, n - head
            bit = MOE_TILE // 2
            while bit >= SUBLANES:
                take = (body & bit) != 0

                @pl.when(take)
                def _(off=off, bit=bit):
                    dst = xs_hbm.at[pl.ds(pl.multiple_of(off, SUBLANES), bit)]
                    act(pltpu.make_async_copy(zeros.at[pl.ds(0, bit)], dst, fill_sem))

                off = off + jnp.where(take, bit, 0)
                bit //= 2
            return c

        lax.fori_loop(0, N_EXPERTS, expert_pad, 0)

        def unused_tile(j, c):
            dst = xs_hbm.at[pl.ds(pl.multiple_of(j * MOE_TILE, MOE_TILE), MOE_TILE)]
            act(pltpu.make_async_copy(zeros, dst, fill_sem))
            return c

        lax.fori_loop(used_ref[0], n_tiles, unused_tile, 0)

    @pl.when(i == 0)
    def _():
        zeros[...] = jnp.zeros_like(zeros)
        fill(lambda cp: cp.start())
        fill(lambda cp: cp.wait())

    def issue(g, c):
        for j in range(ROW_UNROLL):
            r = g * ROW_UNROLL + j
            for k in range(2):
                _row_copy(x_ref, r, xs_hbm, pos_ref[0, k, r], sem).start()
        return c

    lax.fori_loop(0, tm // ROW_UNROLL, issue, 0)
    for k in range(2):
        pltpu.make_async_copy(x_ref, xs_hbm.at[pl.ds(0, tm)], sem).wait()


def _dispatch(x1, pos, n_tiles, pad_start, pad_count, used, tm=512):
    t, d = x1.shape
    pos3 = pos.reshape(2, t // tm, tm).transpose(1, 0, 2)
    grid_spec = pltpu.PrefetchScalarGridSpec(
        num_scalar_prefetch=3,
        grid=(t // tm,),
        in_specs=[pl.BlockSpec((1, 2, tm), lambda i, *_: (i, 0, 0), memory_space=pltpu.SMEM),
                  pl.BlockSpec((tm, d), lambda i, *_: (i, 0))],
        out_specs=pl.BlockSpec(memory_space=pl.ANY),
        scratch_shapes=[pltpu.VMEM((MOE_TILE, d), F32), pltpu.SemaphoreType.DMA, pltpu.SemaphoreType.DMA],
    )
    return pl.pallas_call(
        functools.partial(_dispatch_body, tm=tm, n_tiles=n_tiles),
        grid_spec=grid_spec,
        out_shape=jax.ShapeDtypeStruct((n_tiles * MOE_TILE, d), F32),
        compiler_params=_params(("arbitrary",)),
        name="moe_dispatch",
    )(pad_start, pad_count, used, pos3, x1)


def _ffn_body(te_ref, used_ref, xs_ref, g2_ref, wgu_ref, wd_ref, ys_ref):
    live = pl.program_id(0) < used_ref[0]

    @pl.when(live)
    def _():
        hb = _rms(xs_ref[...], g2_ref[...]).astype(BF16)
        gu = jnp.dot(hb, wgu_ref[...], preferred_element_type=F32)
        hid = wd_ref.shape[0]
        gate, up = gu[:, :hid], gu[:, hid:]
        act = (gate * jax.nn.sigmoid(gate) * up).astype(BF16)
        ys_ref[...] = jnp.dot(act, wd_ref[...], preferred_element_type=F32)

    @pl.when(jnp.logical_not(live))
    def _():
        ys_ref[...] = jnp.zeros_like(ys_ref)


def _ffn(xs, g2, wgu_bf16, wd_bf16, tile_expert, used):
    n_rows, d = xs.shape
    n_tiles = n_rows // MOE_TILE
    hid = wd_bf16.shape[1]
    live = lambda i, te, used: jnp.minimum(i, used[0] - 1)
    grid_spec = pltpu.PrefetchScalarGridSpec(
        num_scalar_prefetch=2,
        grid=(n_tiles,),
        in_specs=[pl.BlockSpec((MOE_TILE, d), lambda i, te, used: (live(i, te, used), 0)),
                  pl.BlockSpec((1, d), lambda i, te, used: (0, 0)),
                  pl.BlockSpec((None, d, 2 * hid), lambda i, te, used: (te[i], 0, 0)),
                  pl.BlockSpec((None, hid, d), lambda i, te, used: (te[i], 0, 0))],
        out_specs=pl.BlockSpec((MOE_TILE, d), lambda i, te, used: (i, 0)),
    )
    return pl.pallas_call(
        _ffn_body,
        grid_spec=grid_spec,
        out_shape=jax.ShapeDtypeStruct((n_rows, d), F32),
        compiler_params=_params(("arbitrary",)),
        name="moe_experts",
    )(tile_expert, used, xs, g2.reshape(1, d), wgu_bf16, wd_bf16)


def _combine_body(pos_ref, pos_next_ref, ys_hbm, x1_ref, w_ref, gf_ref, o_ref, buf, sems, *, tm, final_norm):
    i = pl.program_id(0)
    slot = i % 2

    def gather(p_ref, s):
        def issue(g, c):
            for j in range(ROW_UNROLL):
                r = g * ROW_UNROLL + j
                for k in range(2):
                    _row_copy(ys_hbm, p_ref[0, k, r], buf.at[s, k], r, sems.at[s]).start()
            return c

        lax.fori_loop(0, tm // ROW_UNROLL, issue, 0)

    @pl.when(i == 0)
    def _():
        gather(pos_ref, slot)

    @pl.when(i + 1 < pl.num_programs(0))
    def _():
        gather(pos_next_ref, 1 - slot)

    for k in range(2):
        pltpu.make_async_copy(ys_hbm.at[pl.ds(0, tm)], buf.at[slot, k], sems.at[slot]).wait()
    w = w_ref[...]
    y = x1_ref[...] + w[:, 0:1] * buf[slot, 0] + w[:, 1:2] * buf[slot, 1]
    if final_norm:
        y = _rms(y, gf_ref[...])
    o_ref[...] = y


def _combine(ys, x1, pos, w2, gf, final_norm, tm=256):
    t, d = x1.shape
    n = t // tm
    pos3 = pos.reshape(2, n, tm).transpose(1, 0, 2)
    pos_spec = lambda f: pl.BlockSpec((1, 2, tm), lambda i: (f(i), 0, 0), memory_space=pltpu.SMEM)
    return pl.pallas_call(
        functools.partial(_combine_body, tm=tm, final_norm=final_norm),
        grid=(n,),
        in_specs=[pos_spec(lambda i: i),
                  pos_spec(lambda i: jnp.minimum(i + 1, n - 1)),
                  pl.BlockSpec(memory_space=pl.ANY),
                  pl.BlockSpec((tm, d), lambda i: (i, 0)),
                  pl.BlockSpec((tm, 2), lambda i: (i, 0)),
                  pl.BlockSpec((1, d), lambda i: (0, 0))],
        out_specs=pl.BlockSpec((tm, d), lambda i: (i, 0)),
        out_shape=jax.ShapeDtypeStruct((t, d), F32),
        scratch_shapes=[pltpu.VMEM((2, 2, tm, d), F32), pltpu.SemaphoreType.DMA((2,))],
        compiler_params=_params(("arbitrary",)),
        name="moe_combine",
    )(pos3, pos3, ys, x1, w2, gf.reshape(1, d))


def _layer(x, batch, seq, norm1, w_in, a_ln_g, a_ln_b, a_ws, a_bs, ret_gn, w_out, norm2,
           router_g, router_gb, router_e, router_eb, w_gate, w_up, w_down, final_gain, is_last):
    t, d = x.shape
    a_groups = a_ws.shape[0]
    aw = a_groups * HEAD_DIM
    c_heads = ret_gn.shape[0] // HEAD_DIM
    cw = c_heads * HEAD_DIM
    bw = w_in.shape[1] - 2 * aw - 4 * cw
    b_heads = (bw // 3) // HEAD_DIM
    z = _inproj(x, norm1, w_in.astype(BF16))
    z3 = z.reshape(batch, seq, z.shape[1])
    a_out = _sgu(z, a_ln_g, a_ln_b, a_ws, a_bs)
    b_out = _attn(z3, (2 * aw) // LANES, b_heads).reshape(t, bw // 3)
    c_out = _retention(z3, (2 * aw + bw) // cw, c_heads, ret_gn).reshape(t, cw)
    x1, route = _outproj(x, a_out, b_out, c_out, w_out.astype(BF16), norm2,
                         router_g, router_gb, router_e, router_eb)
    eid = route[0:2].astype(I32)
    w2 = route[2:4].T
    n_tiles = (2 * t) // MOE_TILE + N_EXPERTS
    pos, tile_expert, used, pad_start, pad_count = _moe_plan(eid, n_tiles)
    xs = _dispatch(x1, pos, n_tiles, pad_start, pad_count, used)
    wgu = jnp.concatenate([w_gate, w_up], axis=2).astype(BF16)
    ys = _ffn(xs, norm2, wgu, w_down.astype(BF16), tile_expert, used)
    return _combine(ys, x1, pos, w2, final_gain, is_last)


def kernel(x, norm1, w_in, a_ln_g, a_ln_b, a_ws, a_bs, ret_gn, w_out, norm2, router_g, router_gb, router_e,
           router_eb, w_gate, w_up, w_down, final_norm):
    batch, seq, d = x.shape
    depth = norm1.shape[0]
    h = x.reshape(batch * seq, d)
    for l in range(depth):
        h = _layer(h, batch, seq, norm1[l], w_in[l], a_ln_g[l], a_ln_b[l], a_ws[l], a_bs[l], ret_gn[l],
                   w_out[l], norm2[l], router_g[l], router_gb[l], router_e[l], router_eb[l],
                   w_gate[l], w_up[l], w_down[l], final_norm, l == depth - 1)
    return h.reshape(batch, seq, d)
```

```python
import functools

import numpy as np
import jax
import jax.numpy as jnp
from jax import lax
from jax.experimental import pallas as pl
from jax.experimental.pallas import tpu as pltpu

F32 = jnp.float32
BF16 = jnp.bfloat16
I32 = jnp.int32

EPS = 1e-6
HEAD_DIM = 64
SGU_CHUNK = 128
RET_CHUNK = 128
ATTN_BLOCK = 128
DILATIONS = (1, 4, 16)
ATTN_UNROLL = 8
ROPE_BASE = 10000.0
N_GROUPS = 4
EXPERTS_PER_GROUP = 4
N_EXPERTS = N_GROUPS * EXPERTS_PER_GROUP
LANES = 128
SUBLANES = 8
MOE_TILE = 512
ROW_UNROLL = 8
VMEM_LIMIT = 56 * 1024 * 1024


def _params(sem, vmem=VMEM_LIMIT):
    return pltpu.CompilerParams(dimension_semantics=sem, vmem_limit_bytes=vmem)


def _rms(x, g):
    return x * lax.rsqrt(jnp.mean(x * x, axis=-1, keepdims=True) + EPS) * g


def _split_bf16(a):
    hi = a.astype(BF16)
    lo = (a - hi.astype(F32)).astype(BF16)
    return hi, lo


def _inproj_body(x_ref, g_ref, w_ref, z_ref, *, n_chunk):
    hb = _rms(x_ref[...], g_ref[...]).astype(BF16)
    for n0 in range(0, w_ref.shape[1], n_chunk):
        z_ref[:, n0:n0 + n_chunk] = jnp.dot(hb, w_ref[:, n0:n0 + n_chunk], preferred_element_type=F32)


def _inproj(x, g, w_bf16, tm=512, n_chunk=512):
    t, d = x.shape
    n = w_bf16.shape[1]
    return pl.pallas_call(
        functools.partial(_inproj_body, n_chunk=n_chunk),
        grid=(t // tm,),
        in_specs=[pl.BlockSpec((tm, d), lambda i: (i, 0)),
                  pl.BlockSpec((1, d), lambda i: (0, 0)),
                  pl.BlockSpec((d, n), lambda i: (0, 0))],
        out_specs=pl.BlockSpec((tm, n), lambda i: (i, 0)),
        out_shape=jax.ShapeDtypeStruct((t, n), F32),
        compiler_params=_params(("parallel",)),
        name="inproj",
    )(x, g.reshape(1, d), w_bf16)


def _gelu_tanh(x):
    return 0.5 * x * (1.0 + jnp.tanh(np.sqrt(2.0 / np.pi).astype(np.float32) * (x + 0.044715 * (x * x * x))))


def _sgu_body(za_ref, lng_ref, lnb_ref, wcat_ref, bias_ref, o_ref):
    ga = _gelu_tanh(za_ref[...])
    aw = ga.shape[1] // 2
    u = ga[:, :aw]
    v = ga[:, aw:]
    mu = jnp.mean(v, axis=-1, keepdims=True)
    dv = v - mu
    var = jnp.mean(dv * dv, axis=-1, keepdims=True)
    vn = dv * lax.rsqrt(var + EPS) * lng_ref[...] + lnb_ref[...]
    group = lax.broadcasted_iota(I32, (1, aw), 1) // HEAD_DIM
    n_groups = aw // HEAD_DIM
    for c in range(ga.shape[0] // SGU_CHUNK):
        rows = slice(c * SGU_CHUNK, (c + 1) * SGU_CHUNK)
        vc = vn[rows]
        stack = jnp.concatenate([jnp.where(group == g, vc, 0.0) for g in range(n_groups)], axis=0)
        s = jnp.dot(wcat_ref[...], stack.astype(BF16), preferred_element_type=F32) + bias_ref[...]
        o_ref[rows, :] = u[rows] * s


def _sgu(z, ln_g, ln_b, w_s, b_s, tm=512):
    t = z.shape[0]
    n_groups = w_s.shape[0]
    aw = n_groups * HEAD_DIM
    causal = jnp.tril(jnp.ones((SGU_CHUNK, SGU_CHUNK), dtype=bool))
    wcat = jnp.where(causal[None], w_s, 0.0).transpose(1, 0, 2).reshape(SGU_CHUNK, n_groups * SGU_CHUNK)
    bias = jnp.repeat(b_s.T, HEAD_DIM, axis=1)
    return pl.pallas_call(
        _sgu_body,
        grid=(t // tm,),
        in_specs=[pl.BlockSpec((tm, 2 * aw), lambda i: (i, 0)),
                  pl.BlockSpec((1, aw), lambda i: (0, 0)),
                  pl.BlockSpec((1, aw), lambda i: (0, 0)),
                  pl.BlockSpec(wcat.shape, lambda i: (0, 0)),
                  pl.BlockSpec(bias.shape, lambda i: (0, 0))],
        out_specs=pl.BlockSpec((tm, aw), lambda i: (i, 0)),
        out_shape=jax.ShapeDtypeStruct((t, aw), F32),
        compiler_params=_params(("parallel",)),
        name="sgu",
    )(z, ln_g.reshape(1, aw), ln_b.reshape(1, aw), wcat.astype(BF16), bias)


def _attn_bias():
    blk = ATTN_BLOCK
    qi = np.arange(2 * blk)[:, None] % blk
    ci = np.arange(2 * blk)[None, :]
    first = (ci < blk) & (ci <= qi)
    later = np.where(ci < blk, ci >= qi, ci - blk <= qi)
    return np.where(np.stack([first, later]), 0.0, -np.inf).astype(np.float32)


def _attn_body(q_ref, k_ref, v_ref, bias_ref, o_ref, ob_scr, lse_scr, *, seq, unroll):
    q2, k2, v2 = q_ref.at[0], k_ref.at[0], v_ref.at[0]
    blk = ATTN_BLOCK
    head0 = lax.broadcasted_iota(I32, (1, LANES), 1) < HEAD_DIM
    scale = HEAD_DIM ** -0.5

    def rows(start, n, d):
        return pl.ds(start, n) if d == 1 else pl.ds(start, n, stride=d)

    def block(p, d, r, n):
        base = n * (blk * d) + r
        kstart = jnp.maximum(base - blk * d, r)
        if d == 1:
            base, kstart = pl.multiple_of(base, blk), pl.multiple_of(kstart, blk)
        qb = q2[rows(base, blk, d), :] * scale
        qs = jnp.concatenate([jnp.where(head0, qb, 0.0), jnp.where(head0, 0.0, qb)], axis=0).astype(BF16)
        kb = k2[rows(kstart, 2 * blk, d), :].astype(BF16)
        vb = v2[rows(kstart, 2 * blk, d), :].astype(BF16)
        va = jnp.concatenate([vb, jnp.ones_like(vb)], axis=1)
        s = lax.dot_general(qs, kb, (((1,), (1,)), ((), ())), preferred_element_type=F32)
        s = s + bias_ref[jnp.minimum(n, 1)]
        m = jnp.max(s, axis=-1, keepdims=True)
        e = jnp.exp(s - m)
        oa = jnp.dot(e.astype(BF16), va, preferred_element_type=F32)
        l = oa[:, LANES:]
        o = oa[:, :LANES] / l
        lse = m + jnp.log(l)
        ob_scr[p, rows(base, blk, d), :] = jnp.where(head0, o[:blk], o[blk:])
        lse_scr[p, rows(base, blk, d), :] = jnp.where(head0, lse[:blk], lse[blk:])

    for p, d in enumerate(DILATIONS):
        def group(i, carry, p=p, d=d):
            for u in range(unroll):
                b = i * unroll + u
                block(p, d, b & (d - 1), b >> (d.bit_length() - 1))
            return carry

        lax.fori_loop(0, seq // (blk * unroll), group, 0)

    step = 256

    def mix(i, carry):
        sl = pl.ds(pl.multiple_of(i * step, step), step)
        ls = [lse_scr[p, sl, :] for p in range(len(DILATIONS))]
        m = functools.reduce(jnp.maximum, ls)
        es = [jnp.exp(l - m) for l in ls]
        num = sum(e * ob_scr[p, sl, :] for p, e in enumerate(es))
        o_ref[0, sl, :] = num / sum(es)
        return carry

    lax.fori_loop(0, seq // step, mix, 0)


def _attn(z3, q_col, n_heads):
    b, s, _ = z3.shape
    n_pairs = n_heads * HEAD_DIM // LANES
    assert s % (ATTN_BLOCK * ATTN_UNROLL) == 0 and s % (2 * ATTN_BLOCK * max(DILATIONS)) == 0
    assert all(d & (d - 1) == 0 for d in DILATIONS)
    bias = jnp.asarray(_attn_bias())

    def spec(off):
        return pl.BlockSpec((1, s, LANES), lambda i, j: (i, 0, off + j))

    return pl.pallas_call(
        functools.partial(_attn_body, seq=s, unroll=ATTN_UNROLL),
        grid=(b, n_pairs),
        in_specs=[spec(q_col), spec(q_col + n_pairs), spec(q_col + 2 * n_pairs),
                  pl.BlockSpec(bias.shape, lambda i, j: (0, 0, 0))],
        out_specs=pl.BlockSpec((1, s, LANES), lambda i, j: (i, 0, j)),
        out_shape=jax.ShapeDtypeStruct((b, s, n_pairs * LANES), F32),
        scratch_shapes=[pltpu.VMEM((len(DILATIONS), s, LANES), F32),
                        pltpu.VMEM((len(DILATIONS), s, LANES), F32)],
        compiler_params=_params(("parallel", "parallel")),
        name="dilated_attn",
    )(z3, z3, z3, bias)


def _ret_body(q_ref, k_ref, v_ref, g_ref, cos_ref, sa_ref, sb_ref, dec_ref, qdec_ref, kdec_ref, cdec_ref,
              avg_ref, gn_ref, o_ref, state):
    cw = q_ref.shape[2]
    n_heads = cw // HEAD_DIM
    head = lax.broadcasted_iota(I32, (1, cw), 1) // HEAD_DIM
    hr = lax.broadcasted_iota(I32, (cw, cw), 0) // HEAD_DIM
    hc = lax.broadcasted_iota(I32, (cw, cw), 1) // HEAD_DIM
    same_head = hr == hc

    @pl.when(pl.program_id(1) == 0)
    def _():
        state[...] = jnp.zeros_like(state)

    def rotary(x, cos, sa, sb):
        half = HEAD_DIM // 2
        return x * cos + pltpu.roll(x, half, 1) * sa + pltpu.roll(x, cw - half, 1) * sb

    def head_mean(a):
        hi, lo = _split_bf16(a)
        return (jnp.dot(hi, avg_ref[...], preferred_element_type=F32)
                + jnp.dot(lo, avg_ref[...], preferred_element_type=F32))

    for c in range(q_ref.shape[1] // RET_CHUNK):
        rows = slice(c * RET_CHUNK, (c + 1) * RET_CHUNK)
        cos, sa, sb = cos_ref[rows, :], sa_ref[rows, :], sb_ref[rows, :]
        qr = rotary(q_ref[0, rows, :], cos, sa, sb)
        kr = rotary(k_ref[0, rows, :], cos, sa, sb) * (HEAD_DIM ** -0.5)
        v = v_ref[0, rows, :]
        krb = kr.astype(BF16)
        inner = [lax.dot_general(jnp.where(head == h, qr, 0.0).astype(BF16), krb, (((1,), (1,)), ((), ())),
                                 preferred_element_type=F32) for h in range(n_heads)]
        inner = jnp.concatenate(inner, axis=1) * dec_ref[...]
        vstack = jnp.concatenate([jnp.where(head == h, v, 0.0) for h in range(n_heads)], axis=0)
        st = state[...]
        o = (jnp.dot(inner.astype(BF16), vstack.astype(BF16), preferred_element_type=F32)
             + jnp.dot((qr * qdec_ref[...]).astype(BF16), st.astype(BF16), preferred_element_type=F32))
        ktv = lax.dot_general((kr * kdec_ref[...]).astype(BF16), v.astype(BF16), (((0,), (0,)), ((), ())),
                              preferred_element_type=F32)
        state[...] = st * cdec_ref[...] + jnp.where(same_head, ktv, 0.0)
        mu = head_mean(o)
        dd = o - mu
        var = head_mean(dd * dd)
        on = dd * lax.rsqrt(var + EPS) * gn_ref[...]
        gate = g_ref[0, rows, :]
        o_ref[0, rows, :] = gate * jax.nn.sigmoid(gate) * on


def _ret_tables(seq, n_heads):
    half = HEAD_DIM // 2
    inv = ROPE_BASE ** (-jnp.arange(half, dtype=F32) / half)
    ang = jnp.arange(seq, dtype=F32)[:, None] * inv[None]
    cos, sin = jnp.cos(ang), jnp.sin(ang)
    zero = jnp.zeros_like(sin)
    tile = lambda a: jnp.tile(a, (1, n_heads))
    cos_t = tile(jnp.concatenate([cos, cos], axis=1))
    sa_t = tile(jnp.concatenate([zero, sin], axis=1))
    sb_t = tile(jnp.concatenate([-sin, zero], axis=1))
    log_g = jnp.log(1.0 - 2.0 ** (-5.0 - jnp.arange(n_heads, dtype=F32)))
    c = RET_CHUNK
    idx = jnp.arange(c)
    diff = idx[:, None] - idx[None, :]
    decay_in = jnp.where(diff >= 0, jnp.exp(log_g[:, None, None] * jnp.maximum(diff, 0)[None]), 0.0)
    dec = decay_in.transpose(1, 0, 2).reshape(c, n_heads * c)
    lane_head = jnp.repeat(jnp.arange(n_heads), HEAD_DIM)
    qdec = jnp.exp(log_g[lane_head][None, :] * (idx + 1)[:, None].astype(F32))
    kdec = jnp.exp(log_g[lane_head][None, :] * (c - 1 - idx)[:, None].astype(F32))
    same = lane_head[:, None] == lane_head[None, :]
    cdec = jnp.where(same, jnp.exp(log_g * c)[lane_head][:, None], 0.0)
    avg = jnp.where(same, 1.0 / HEAD_DIM, 0.0).astype(BF16)
    return cos_t, sa_t, sb_t, dec, qdec, kdec, cdec, avg


def _retention(z3, col, n_heads, gn, tc=512):
    b, s, _ = z3.shape
    cw = n_heads * HEAD_DIM
    cos_t, sa_t, sb_t, dec, qdec, kdec, cdec, avg = _ret_tables(s, n_heads)

    def zspec(off):
        return pl.BlockSpec((1, tc, cw), lambda i, j: (i, j, off))

    tab = pl.BlockSpec((tc, cw), lambda i, j: (j, 0))
    full = lambda a: pl.BlockSpec(a.shape, lambda i, j: (0, 0))
    gn2 = gn.reshape(1, cw)
    return pl.pallas_call(
        _ret_body,
        grid=(b, s // tc),
        in_specs=[zspec(col), zspec(col + 1), zspec(col + 2), zspec(col + 3), tab, tab, tab,
                  full(dec), full(qdec), full(kdec), full(cdec), full(avg), full(gn2)],
        out_specs=pl.BlockSpec((1, tc, cw), lambda i, j: (i, j, 0)),
        out_shape=jax.ShapeDtypeStruct((b, s, cw), F32),
        scratch_shapes=[pltpu.VMEM((cw, cw), F32)],
        compiler_params=_params(("parallel", "arbitrary")),
        name="retention",
    )(z3, z3, z3, z3, cos_t, sa_t, sb_t, dec, qdec, kdec, cdec, avg, gn2)


def _outproj_body(x_ref, a_ref, b_ref, c_ref, wo_ref, g2_ref, rhi_ref, rlo_ref, rb_ref, x1_ref, route_ref):
    aw, bw = a_ref.shape[1], b_ref.shape[1]
    y = (jnp.dot(a_ref[...].astype(BF16), wo_ref[0:aw, :], preferred_element_type=F32)
         + jnp.dot(b_ref[...].astype(BF16), wo_ref[aw:aw + bw, :], preferred_element_type=F32)
         + jnp.dot(c_ref[...].astype(BF16), wo_ref[aw + bw:, :], preferred_element_type=F32))
    x1 = x_ref[...] + y
    x1_ref[...] = x1
    hi, lo = _split_bf16(_rms(x1, g2_ref[...]))
    logits = (jnp.dot(hi, rhi_ref[...], preferred_element_type=F32)
              + jnp.dot(lo, rhi_ref[...], preferred_element_type=F32)
              + jnp.dot(hi, rlo_ref[...], preferred_element_type=F32)) + rb_ref[...]
    lt = logits.T
    row = lambda i: lt[i:i + 1, :]
    best, gi = row(0), jnp.zeros_like(row(0), dtype=I32)
    for i in range(1, N_GROUPS):
        up = row(i) > best
        best = jnp.where(up, row(i), best)
        gi = jnp.where(up, i, gi)
    g_w = 1.0 / sum(jnp.exp(row(i) - best) for i in range(N_GROUPS))
    el = []
    for j in range(EXPERTS_PER_GROUP):
        e = row(N_GROUPS + (N_GROUPS - 1) * EXPERTS_PER_GROUP + j)
        for g in range(N_GROUPS - 2, -1, -1):
            e = jnp.where(gi == g, row(N_GROUPS + g * EXPERTS_PER_GROUP + j), e)
        el.append(e)
    v1, i1 = el[0], jnp.zeros_like(gi)
    for j in range(1, EXPERTS_PER_GROUP):
        up = el[j] > v1
        v1 = jnp.where(up, el[j], v1)
        i1 = jnp.where(up, j, i1)
    v2, i2 = jnp.full_like(v1, -jnp.inf), jnp.zeros_like(gi)
    for j in range(EXPERTS_PER_GROUP):
        up = jnp.logical_and(i1 != j, el[j] > v2)
        v2 = jnp.where(up, el[j], v2)
        i2 = jnp.where(up, j, i2)
    e21 = jnp.exp(v2 - v1)
    w1 = g_w / (1.0 + e21)
    w2 = g_w * e21 / (1.0 + e21)
    e1 = (gi * EXPERTS_PER_GROUP + i1).astype(F32)
    e2 = (gi * EXPERTS_PER_GROUP + i2).astype(F32)
    zero = jnp.zeros_like(w1)
    route_ref[...] = jnp.concatenate([e1, e2, w1, w2, zero, zero, zero, zero], axis=0)


def _outproj(x, a, b, c, wo_bf16, g2, router_g, router_gb, router_e, router_eb, tm=512):
    t, d = x.shape
    r = jnp.concatenate([router_g, router_e], axis=1)
    r = jnp.pad(r, ((0, 0), (0, LANES - r.shape[1])))
    rhi = r.astype(BF16)
    rlo = (r - rhi.astype(F32)).astype(BF16)
    rb = jnp.pad(jnp.concatenate([router_gb, router_eb]), (0, LANES - N_GROUPS - N_EXPERTS)).reshape(1, LANES)
    row_spec = lambda w: pl.BlockSpec((tm, w), lambda i: (i, 0))
    full = lambda arr: pl.BlockSpec(arr.shape, lambda i: (0, 0))
    g2r = g2.reshape(1, d)
    return pl.pallas_call(
        _outproj_body,
        grid=(t // tm,),
        in_specs=[row_spec(d), row_spec(a.shape[1]), row_spec(b.shape[1]), row_spec(c.shape[1]),
                  full(wo_bf16), full(g2r), full(rhi), full(rlo), full(rb)],
        out_specs=[row_spec(d), pl.BlockSpec((8, tm), lambda i: (0, i))],
        out_shape=[jax.ShapeDtypeStruct((t, d), F32), jax.ShapeDtypeStruct((8, t), F32)],
        compiler_params=_params(("parallel",)),
        name="outproj_route",
    )(x, a, b, c, wo_bf16, g2r, rhi, rlo, rb)


def _moe_plan(eid, n_tiles):
    flat = eid.reshape(-1)
    onehot = (flat[:, None] == jnp.arange(N_EXPERTS, dtype=I32)[None, :]).astype(I32)
    csum = jnp.cumsum(onehot, axis=0)
    counts = csum[-1]
    padded = ((counts + MOE_TILE - 1) // MOE_TILE) * MOE_TILE
    ends = jnp.cumsum(padded)
    starts = ends - padded
    pos = jnp.sum(onehot * (csum - 1 + starts[None, :]), axis=1)
    tile_start = jnp.arange(n_tiles, dtype=I32) * MOE_TILE
    tile_expert = jnp.sum((ends[None, :] <= tile_start[:, None]).astype(I32), axis=1)
    tile_expert = jnp.minimum(tile_expert, N_EXPERTS - 1).astype(I32)
    used = (ends[-1] // MOE_TILE).astype(I32).reshape(1)
    pad_start = (starts + counts).astype(I32)
    pad_count = (padded - counts).astype(I32)
    return pos.astype(I32), tile_expert, used, pad_start, pad_count


def _store_token_major(ref, x, tok0=0):
    n = x.shape[0]
    for c in range(SUBLANES):
        ref[pl.ds(tok0 * SUBLANES + c, n, stride=SUBLANES), :] = x[:, c * LANES:(c + 1) * LANES]


def _load_token_major(ref, n, tok0=0):
    return jnp.concatenate([ref[pl.ds(tok0 * SUBLANES + c, n, stride=SUBLANES), :] for c in range(SUBLANES)],
                           axis=1)


def _tokens(ref, tok, n=1):
    return ref.at[pl.ds(pl.multiple_of(tok * SUBLANES, SUBLANES), n * SUBLANES)]


def _dispatch_body(pad_start_ref, pad_count_ref, used_ref, pos_ref, x_ref, xs_hbm, xt, zeros, sem, fill_sem,
                   *, tm, n_tokens, n_tiles):
    i = pl.program_id(0)
    half_tile = MOE_TILE // 2

    def fill(act):
        def expert_pad(e, c):
            off, n = pad_start_ref[e], pad_count_ref[e]
            bit = half_tile
            while bit:
                take = (n & bit) != 0

                @pl.when(take)
                def _(off=off, bit=bit):
                    act(pltpu.make_async_copy(_tokens(zeros, 0, bit), _tokens(xs_hbm, off, bit), fill_sem))

                off = off + jnp.where(take, bit, 0)
                bit //= 2
            return c

        lax.fori_loop(0, N_EXPERTS, expert_pad, 0)

        def unused_half_tile(j, c):
            act(pltpu.make_async_copy(zeros, _tokens(xs_hbm, j * half_tile, half_tile), fill_sem))
            return c

        lax.fori_loop(2 * used_ref[0], 2 * n_tiles, unused_half_tile, 0)

    @pl.when(i == 0)
    def _():
        zeros[...] = jnp.zeros_like(zeros)
        fill(lambda cp: cp.start())
        fill(lambda cp: cp.wait())

    _store_token_major(xt, x_ref[...])

    def issue(g, c):
        for k in range(2):
            base = k * n_tokens + i * tm + g * ROW_UNROLL
            for j in range(ROW_UNROLL):
                dst = _tokens(xs_hbm, pos_ref[base + j])
                pltpu.make_async_copy(_tokens(xt, g * ROW_UNROLL + j), dst, sem).start()
        return c

    lax.fori_loop(0, tm // ROW_UNROLL, issue, 0)
    for k in range(2):
        pltpu.make_async_copy(xt, _tokens(xs_hbm, 0, tm), sem).wait()


def _dispatch(x1, pos, n_tiles, pad_start, pad_count, used, tm=512):
    t, d = x1.shape
    assert d == SUBLANES * LANES
    grid_spec = pltpu.PrefetchScalarGridSpec(
        num_scalar_prefetch=4,
        grid=(t // tm,),
        in_specs=[pl.BlockSpec((tm, d), lambda i, *_: (i, 0))],
        out_specs=pl.BlockSpec(memory_space=pl.ANY),
        scratch_shapes=[pltpu.VMEM((tm * SUBLANES, LANES), F32),
                        pltpu.VMEM((MOE_TILE // 2 * SUBLANES, LANES), F32),
                        pltpu.SemaphoreType.DMA, pltpu.SemaphoreType.DMA],
    )
    return pl.pallas_call(
        functools.partial(_dispatch_body, tm=tm, n_tokens=t, n_tiles=n_tiles),
        grid_spec=grid_spec,
        out_shape=jax.ShapeDtypeStruct((n_tiles * MOE_TILE * SUBLANES, LANES), F32),
        compiler_params=_params(("arbitrary",)),
        name="moe_dispatch",
    )(pad_start, pad_count, used, pos, x1)


def _ffn_body(te_ref, used_ref, xs_ref, g2_ref, wgu_ref, wd_ref, ys_ref):
    live = pl.program_id(0) < used_ref[0]

    @pl.when(live)
    def _():
        hid = wd_ref.shape[0]
        half = MOE_TILE // 2
        for tok0 in (0, half):
            hb = _rms(_load_token_major(xs_ref, half, tok0), g2_ref[...]).astype(BF16)
            gu = jnp.dot(hb, wgu_ref[...], preferred_element_type=F32)
            gate, up = gu[:, :hid], gu[:, hid:]
            act = (gate * jax.nn.sigmoid(gate) * up).astype(BF16)
            _store_token_major(ys_ref, jnp.dot(act, wd_ref[...], preferred_element_type=F32), tok0)

    @pl.when(jnp.logical_not(live))
    def _():
        ys_ref[...] = jnp.zeros_like(ys_ref)


def _ffn(xs, g2, wgu_bf16, wd_bf16, tile_expert, used):
    n_tiles = xs.shape[0] // (MOE_TILE * SUBLANES)
    d, hid = wd_bf16.shape[2], wd_bf16.shape[1]
    tile_rows = MOE_TILE * SUBLANES
    live = lambda i, te, used: jnp.minimum(i, used[0] - 1)
    grid_spec = pltpu.PrefetchScalarGridSpec(
        num_scalar_prefetch=2,
        grid=(n_tiles,),
        in_specs=[pl.BlockSpec((tile_rows, LANES), lambda i, te, used: (live(i, te, used), 0)),
                  pl.BlockSpec((1, d), lambda i, te, used: (0, 0)),
                  pl.BlockSpec((None, d, 2 * hid), lambda i, te, used: (te[i], 0, 0)),
                  pl.BlockSpec((None, hid, d), lambda i, te, used: (te[i], 0, 0))],
        out_specs=pl.BlockSpec((tile_rows, LANES), lambda i, te, used: (i, 0)),
    )
    return pl.pallas_call(
        _ffn_body,
        grid_spec=grid_spec,
        out_shape=jax.ShapeDtypeStruct(xs.shape, F32),
        compiler_params=_params(("arbitrary",)),
        name="moe_experts",
    )(tile_expert, used, xs, g2.reshape(1, d), wgu_bf16, wd_bf16)


def _combine_body(pos_ref, ys_hbm, x1_ref, w_ref, gf_ref, o_ref, buf, sems, *, tm, n_tokens, final_norm):
    i = pl.program_id(0)
    slot = i % 2

    def gather(step, s):
        def issue(g, c):
            for k in range(2):
                base = k * n_tokens + step * tm + g * ROW_UNROLL
                for j in range(ROW_UNROLL):
                    src = _tokens(ys_hbm, pos_ref[base + j])
                    pltpu.make_async_copy(src, _tokens(buf.at[s, k], g * ROW_UNROLL + j), sems.at[s]).start()
            return c

        lax.fori_loop(0, tm // ROW_UNROLL, issue, 0)

    @pl.when(i == 0)
    def _():
        gather(i, slot)

    @pl.when(i + 1 < pl.num_programs(0))
    def _():
        gather(i + 1, 1 - slot)

    for k in range(2):
        pltpu.make_async_copy(_tokens(ys_hbm, 0, tm), buf.at[slot, k], sems.at[slot]).wait()
    w = w_ref[...]
    y = (x1_ref[...] + w[:, 0:1] * _load_token_major(buf.at[slot, 0], tm)
         + w[:, 1:2] * _load_token_major(buf.at[slot, 1], tm))
    if final_norm:
        y = _rms(y, gf_ref[...])
    o_ref[...] = y


def _combine(ys, x1, pos, w2, gf, final_norm, tm=256):
    t, d = x1.shape
    grid_spec = pltpu.PrefetchScalarGridSpec(
        num_scalar_prefetch=1,
        grid=(t // tm,),
        in_specs=[pl.BlockSpec(memory_space=pl.ANY),
                  pl.BlockSpec((tm, d), lambda i, *_: (i, 0)),
                  pl.BlockSpec((tm, 2), lambda i, *_: (i, 0)),
                  pl.BlockSpec((1, d), lambda i, *_: (0, 0))],
        out_specs=pl.BlockSpec((tm, d), lambda i, *_: (i, 0)),
        scratch_shapes=[pltpu.VMEM((2, 2, tm * SUBLANES, LANES), F32), pltpu.SemaphoreType.DMA((2,))],
    )
    return pl.pallas_call(
        functools.partial(_combine_body, tm=tm, n_tokens=t, final_norm=final_norm),
        grid_spec=grid_spec,
        out_shape=jax.ShapeDtypeStruct((t, d), F32),
        compiler_params=_params(("arbitrary",)),
        name="moe_combine",
    )(pos, ys, x1, w2, gf.reshape(1, d))


def _layer(x, batch, seq, norm1, w_in, a_ln_g, a_ln_b, a_ws, a_bs, ret_gn, w_out, norm2,
           router_g, router_gb, router_e, router_eb, w_gate, w_up, w_down, final_gain, is_last):
    t, d = x.shape
    a_groups = a_ws.shape[0]
    aw = a_groups * HEAD_DIM
    c_heads = ret_gn.shape[0] // HEAD_DIM
    cw = c_heads * HEAD_DIM
    bw = w_in.shape[1] - 2 * aw - 4 * cw
    b_heads = (bw // 3) // HEAD_DIM
    z = _inproj(x, norm1, w_in.astype(BF16))
    z3 = z.reshape(batch, seq, z.shape[1])
    a_out = _sgu(z, a_ln_g, a_ln_b, a_ws, a_bs)
    b_out = _attn(z3, (2 * aw) // LANES, b_heads).reshape(t, bw // 3)
    c_out = _retention(z3, (2 * aw + bw) // cw, c_heads, ret_gn).reshape(t, cw)
    x1, route = _outproj(x, a_out, b_out, c_out, w_out.astype(BF16), norm2,
                         router_g, router_gb, router_e, router_eb)
    eid = route[0:2].astype(I32)
    w2 = route[2:4].T
    n_tiles = (2 * t) // MOE_TILE + N_EXPERTS
    pos, tile_expert, used, pad_start, pad_count = _moe_plan(eid, n_tiles)
    xs = _dispatch(x1, pos, n_tiles, pad_start, pad_count, used)
    wgu = jnp.concatenate([w_gate, w_up], axis=2).astype(BF16)
    ys = _ffn(xs, norm2, wgu, w_down.astype(BF16), tile_expert, used)
    return _combine(ys, x1, pos, w2, final_gain, is_last)


def kernel(x, norm1, w_in, a_ln_g, a_ln_b, a_ws, a_bs, ret_gn, w_out, norm2, router_g, router_gb, router_e,
           router_eb, w_gate, w_up, w_down, final_norm):
    batch, seq, d = x.shape
    depth = norm1.shape[0]
    h = x.reshape(batch * seq, d)
    for l in range(depth):
        h = _layer(h, batch, seq, norm1[l], w_in[l], a_ln_g[l], a_ln_b[l], a_ws[l], a_bs[l], ret_gn[l],
                   w_out[l], norm2[l], router_g[l], router_gb[l], router_e[l], router_eb[l],
                   w_gate[l], w_up[l], w_down[l], final_norm, l == depth - 1)
    return h.reshape(batch, seq, d)
```

```python
import functools

import numpy as np
import jax
import jax.numpy as jnp
from jax import lax
from jax.experimental import pallas as pl
from jax.experimental.pallas import tpu as pltpu

F32 = jnp.float32
BF16 = jnp.bfloat16
I32 = jnp.int32

EPS = 1e-6
HEAD_DIM = 64
SGU_CHUNK = 128
RET_CHUNK = 128
ATTN_BLOCK = 128
DILATIONS = (1, 4, 16)
ATTN_UNROLL = 8
ROPE_BASE = 10000.0
N_GROUPS = 4
EXPERTS_PER_GROUP = 4
N_EXPERTS = N_GROUPS * EXPERTS_PER_GROUP
LANES = 128
SUBLANES = 8
MOE_TILE = 512
ROW_UNROLL = 8
VMEM_LIMIT = 56 * 1024 * 1024


def _params(sem, vmem=VMEM_LIMIT):
    return pltpu.CompilerParams(dimension_semantics=sem, vmem_limit_bytes=vmem)


def _rms(x, g):
    return x * lax.rsqrt(jnp.mean(x * x, axis=-1, keepdims=True) + EPS) * g


def _split_bf16(a):
    hi = a.astype(BF16)
    lo = (a - hi.astype(F32)).astype(BF16)
    return hi, lo


def _inproj_body(x_ref, g_ref, w_ref, z_ref, *, n_chunk):
    hb = _rms(x_ref[...], g_ref[...]).astype(BF16)
    for n0 in range(0, w_ref.shape[1], n_chunk):
        z_ref[:, n0:n0 + n_chunk] = jnp.dot(hb, w_ref[:, n0:n0 + n_chunk], preferred_element_type=F32)


def _inproj(x, g, w_bf16, tm=512, n_chunk=512):
    t, d = x.shape
    n = w_bf16.shape[1]
    return pl.pallas_call(
        functools.partial(_inproj_body, n_chunk=n_chunk),
        grid=(t // tm,),
        in_specs=[pl.BlockSpec((tm, d), lambda i: (i, 0)),
                  pl.BlockSpec((1, d), lambda i: (0, 0)),
                  pl.BlockSpec((d, n), lambda i: (0, 0))],
        out_specs=pl.BlockSpec((tm, n), lambda i: (i, 0)),
        out_shape=jax.ShapeDtypeStruct((t, n), F32),
        compiler_params=_params(("parallel",)),
        name="inproj",
    )(x, g.reshape(1, d), w_bf16)


def _gelu_tanh(x):
    return 0.5 * x * (1.0 + jnp.tanh(np.sqrt(2.0 / np.pi).astype(np.float32) * (x + 0.044715 * (x * x * x))))


def _sgu_body(za_ref, lng_ref, lnb_ref, wcat_ref, bias_ref, o_ref):
    ga = _gelu_tanh(za_ref[...])
    aw = ga.shape[1] // 2
    u = ga[:, :aw]
    v = ga[:, aw:]
    mu = jnp.mean(v, axis=-1, keepdims=True)
    dv = v - mu
    var = jnp.mean(dv * dv, axis=-1, keepdims=True)
    vn = dv * lax.rsqrt(var + EPS) * lng_ref[...] + lnb_ref[...]
    group = lax.broadcasted_iota(I32, (1, aw), 1) // HEAD_DIM
    n_groups = aw // HEAD_DIM
    for c in range(ga.shape[0] // SGU_CHUNK):
        rows = slice(c * SGU_CHUNK, (c + 1) * SGU_CHUNK)
        vc = vn[rows]
        stack = jnp.concatenate([jnp.where(group == g, vc, 0.0) for g in range(n_groups)], axis=0)
        s = jnp.dot(wcat_ref[...], stack.astype(BF16), preferred_element_type=F32) + bias_ref[...]
        o_ref[rows, :] = u[rows] * s


def _sgu(z, ln_g, ln_b, w_s, b_s, tm=512):
    t = z.shape[0]
    n_groups = w_s.shape[0]
    aw = n_groups * HEAD_DIM
    causal = jnp.tril(jnp.ones((SGU_CHUNK, SGU_CHUNK), dtype=bool))
    wcat = jnp.where(causal[None], w_s, 0.0).transpose(1, 0, 2).reshape(SGU_CHUNK, n_groups * SGU_CHUNK)
    bias = jnp.repeat(b_s.T, HEAD_DIM, axis=1)
    return pl.pallas_call(
        _sgu_body,
        grid=(t // tm,),
        in_specs=[pl.BlockSpec((tm, 2 * aw), lambda i: (i, 0)),
                  pl.BlockSpec((1, aw), lambda i: (0, 0)),
                  pl.BlockSpec((1, aw), lambda i: (0, 0)),
                  pl.BlockSpec(wcat.shape, lambda i: (0, 0)),
                  pl.BlockSpec(bias.shape, lambda i: (0, 0))],
        out_specs=pl.BlockSpec((tm, aw), lambda i: (i, 0)),
        out_shape=jax.ShapeDtypeStruct((t, aw), F32),
        compiler_params=_params(("parallel",)),
        name="sgu",
    )(z, ln_g.reshape(1, aw), ln_b.reshape(1, aw), wcat.astype(BF16), bias)


def _attn_bias():
    blk = ATTN_BLOCK
    qi = np.arange(2 * blk)[:, None] % blk
    ci = np.arange(2 * blk)[None, :]
    first = (ci < blk) & (ci <= qi)
    later = np.where(ci < blk, ci >= qi, ci - blk <= qi)
    return np.where(np.stack([first, later]), 0.0, -np.inf).astype(np.float32)


def _attn_body(q_ref, k_ref, v_ref, bias_ref, o_ref, ob_scr, lse_scr, *, seq, unroll):
    q2, k2, v2 = q_ref.at[0], k_ref.at[0], v_ref.at[0]
    blk = ATTN_BLOCK
    head0 = lax.broadcasted_iota(I32, (1, LANES), 1) < HEAD_DIM
    scale = HEAD_DIM ** -0.5

    def rows(start, n, d):
        return pl.ds(start, n) if d == 1 else pl.ds(start, n, stride=d)

    def block(p, d, r, n):
        base = n * (blk * d) + r
        kstart = jnp.maximum(base - blk * d, r)
        if d == 1:
            base, kstart = pl.multiple_of(base, blk), pl.multiple_of(kstart, blk)
        qb = q2[rows(base, blk, d), :] * scale
        qs = jnp.concatenate([jnp.where(head0, qb, 0.0), jnp.where(head0, 0.0, qb)], axis=0).astype(BF16)
        kb = k2[rows(kstart, 2 * blk, d), :].astype(BF16)
        vb = v2[rows(kstart, 2 * blk, d), :].astype(BF16)
        va = jnp.concatenate([vb, jnp.ones_like(vb)], axis=1)
        s = lax.dot_general(qs, kb, (((1,), (1,)), ((), ())), preferred_element_type=F32)
        s = s + bias_ref[jnp.minimum(n, 1)]
        m = jnp.max(s, axis=-1, keepdims=True)
        e = jnp.exp(s - m)
        oa = jnp.dot(e.astype(BF16), va, preferred_element_type=F32)
        l = oa[:, LANES:]
        o = oa[:, :LANES] / l
        lse = m + jnp.log(l)
        ob_scr[p, rows(base, blk, d), :] = jnp.where(head0, o[:blk], o[blk:])
        lse_scr[p, rows(base, blk, d), :] = jnp.where(head0, lse[:blk], lse[blk:])

    for p, d in enumerate(DILATIONS):
        def group(i, carry, p=p, d=d):
            for u in range(unroll):
                b = i * unroll + u
                block(p, d, b & (d - 1), b >> (d.bit_length() - 1))
            return carry

        lax.fori_loop(0, seq // (blk * unroll), group, 0)

    step = 256

    def mix(i, carry):
        sl = pl.ds(pl.multiple_of(i * step, step), step)
        ls = [lse_scr[p, sl, :] for p in range(len(DILATIONS))]
        m = functools.reduce(jnp.maximum, ls)
        es = [jnp.exp(l - m) for l in ls]
        num = sum(e * ob_scr[p, sl, :] for p, e in enumerate(es))
        o_ref[0, sl, :] = num / sum(es)
        return carry

    lax.fori_loop(0, seq // step, mix, 0)


def _attn(z3, q_col, n_heads):
    b, s, _ = z3.shape
    n_pairs = n_heads * HEAD_DIM // LANES
    assert s % (ATTN_BLOCK * ATTN_UNROLL) == 0 and s % (2 * ATTN_BLOCK * max(DILATIONS)) == 0
    assert all(d & (d - 1) == 0 for d in DILATIONS)
    bias = jnp.asarray(_attn_bias())

    def spec(off):
        return pl.BlockSpec((1, s, LANES), lambda i, j: (i, 0, off + j))

    return pl.pallas_call(
        functools.partial(_attn_body, seq=s, unroll=ATTN_UNROLL),
        grid=(b, n_pairs),
        in_specs=[spec(q_col), spec(q_col + n_pairs), spec(q_col + 2 * n_pairs),
                  pl.BlockSpec(bias.shape, lambda i, j: (0, 0, 0))],
        out_specs=pl.BlockSpec((1, s, LANES), lambda i, j: (i, 0, j)),
        out_shape=jax.ShapeDtypeStruct((b, s, n_pairs * LANES), F32),
        scratch_shapes=[pltpu.VMEM((len(DILATIONS), s, LANES), F32),
                        pltpu.VMEM((len(DILATIONS), s, LANES), F32)],
        compiler_params=_params(("parallel", "parallel")),
        name="dilated_attn",
    )(z3, z3, z3, bias)


def _ret_body(q_ref, k_ref, v_ref, g_ref, cos_ref, sa_ref, sb_ref, dec_ref, qdec_ref, kdec_ref, cdec_ref,
              avg_ref, gn_ref, o_ref, state):
    cw = q_ref.shape[2]
    n_heads = cw // HEAD_DIM
    head = lax.broadcasted_iota(I32, (1, cw), 1) // HEAD_DIM
    hr = lax.broadcasted_iota(I32, (cw, cw), 0) // HEAD_DIM
    hc = lax.broadcasted_iota(I32, (cw, cw), 1) // HEAD_DIM
    same_head = hr == hc

    @pl.when(pl.program_id(1) == 0)
    def _():
        state[...] = jnp.zeros_like(state)

    def rotary(x, cos, sa, sb):
        half = HEAD_DIM // 2
        return x * cos + pltpu.roll(x, half, 1) * sa + pltpu.roll(x, cw - half, 1) * sb

    def head_mean(a):
        hi, lo = _split_bf16(a)
        return (jnp.dot(hi, avg_ref[...], preferred_element_type=F32)
                + jnp.dot(lo, avg_ref[...], preferred_element_type=F32))

    for c in range(q_ref.shape[1] // RET_CHUNK):
        rows = slice(c * RET_CHUNK, (c + 1) * RET_CHUNK)
        cos, sa, sb = cos_ref[rows, :], sa_ref[rows, :], sb_ref[rows, :]
        qr = rotary(q_ref[0, rows, :], cos, sa, sb)
        kr = rotary(k_ref[0, rows, :], cos, sa, sb) * (HEAD_DIM ** -0.5)
        v = v_ref[0, rows, :]
        krb = kr.astype(BF16)
        inner = [lax.dot_general(jnp.where(head == h, qr, 0.0).astype(BF16), krb, (((1,), (1,)), ((), ())),
                                 preferred_element_type=F32) for h in range(n_heads)]
        inner = jnp.concatenate(inner, axis=1) * dec_ref[...]
        vstack = jnp.concatenate([jnp.where(head == h, v, 0.0) for h in range(n_heads)], axis=0)
        st = state[...]
        o = (jnp.dot(inner.astype(BF16), vstack.astype(BF16), preferred_element_type=F32)
             + jnp.dot((qr * qdec_ref[...]).astype(BF16), st.astype(BF16), preferred_element_type=F32))
        ktv = lax.dot_general((kr * kdec_ref[...]).astype(BF16), v.astype(BF16), (((0,), (0,)), ((), ())),
                              preferred_element_type=F32)
        state[...] = st * cdec_ref[...] + jnp.where(same_head, ktv, 0.0)
        mu = head_mean(o)
        dd = o - mu
        var = head_mean(dd * dd)
        on = dd * lax.rsqrt(var + EPS) * gn_ref[...]
        gate = g_ref[0, rows, :]
        o_ref[0, rows, :] = gate * jax.nn.sigmoid(gate) * on


def _ret_tables(seq, n_heads):
    half = HEAD_DIM // 2
    inv = ROPE_BASE ** (-jnp.arange(half, dtype=F32) / half)
    ang = jnp.arange(seq, dtype=F32)[:, None] * inv[None]
    cos, sin = jnp.cos(ang), jnp.sin(ang)
    zero = jnp.zeros_like(sin)
    tile = lambda a: jnp.tile(a, (1, n_heads))
    cos_t = tile(jnp.concatenate([cos, cos], axis=1))
    sa_t = tile(jnp.concatenate([zero, sin], axis=1))
    sb_t = tile(jnp.concatenate([-sin, zero], axis=1))
    log_g = jnp.log(1.0 - 2.0 ** (-5.0 - jnp.arange(n_heads, dtype=F32)))
    c = RET_CHUNK
    idx = jnp.arange(c)
    diff = idx[:, None] - idx[None, :]
    decay_in = jnp.where(diff >= 0, jnp.exp(log_g[:, None, None] * jnp.maximum(diff, 0)[None]), 0.0)
    dec = decay_in.transpose(1, 0, 2).reshape(c, n_heads * c)
    lane_head = jnp.repeat(jnp.arange(n_heads), HEAD_DIM)
    qdec = jnp.exp(log_g[lane_head][None, :] * (idx + 1)[:, None].astype(F32))
    kdec = jnp.exp(log_g[lane_head][None, :] * (c - 1 - idx)[:, None].astype(F32))
    same = lane_head[:, None] == lane_head[None, :]
    cdec = jnp.where(same, jnp.exp(log_g * c)[lane_head][:, None], 0.0)
    avg = jnp.where(same, 1.0 / HEAD_DIM, 0.0).astype(BF16)
    return cos_t, sa_t, sb_t, dec, qdec, kdec, cdec, avg


def _retention(z3, col, n_heads, gn, tc=512):
    b, s, _ = z3.shape
    cw = n_heads * HEAD_DIM
    cos_t, sa_t, sb_t, dec, qdec, kdec, cdec, avg = _ret_tables(s, n_heads)

    def zspec(off):
        return pl.BlockSpec((1, tc, cw), lambda i, j: (i, j, off))

    tab = pl.BlockSpec((tc, cw), lambda i, j: (j, 0))
    full = lambda a: pl.BlockSpec(a.shape, lambda i, j: (0, 0))
    gn2 = gn.reshape(1, cw)
    return pl.pallas_call(
        _ret_body,
        grid=(b, s // tc),
        in_specs=[zspec(col), zspec(col + 1), zspec(col + 2), zspec(col + 3), tab, tab, tab,
                  full(dec), full(qdec), full(kdec), full(cdec), full(avg), full(gn2)],
        out_specs=pl.BlockSpec((1, tc, cw), lambda i, j: (i, j, 0)),
        out_shape=jax.ShapeDtypeStruct((b, s, cw), F32),
        scratch_shapes=[pltpu.VMEM((cw, cw), F32)],
        compiler_params=_params(("parallel", "arbitrary")),
        name="retention",
    )(z3, z3, z3, z3, cos_t, sa_t, sb_t, dec, qdec, kdec, cdec, avg, gn2)


def _outproj_body(x_ref, a_ref, b_ref, c_ref, wo_ref, g2_ref, rhi_ref, rlo_ref, rb_ref, x1_ref, route_ref):
    _outproj_rows(slice(None), x_ref, a_ref, b_ref, c_ref, wo_ref, g2_ref, rhi_ref, rlo_ref, rb_ref, x1_ref,
                  route_ref)


def _outproj_rows(rows, x_ref, a_ref, b_ref, c_ref, wo_ref, g2_ref, rhi_ref, rlo_ref, rb_ref, x1_ref, route_ref):
    aw, bw = a_ref.shape[1], b_ref.shape[1]
    y = (jnp.dot(a_ref[rows, :].astype(BF16), wo_ref[0:aw, :], preferred_element_type=F32)
         + jnp.dot(b_ref[rows, :].astype(BF16), wo_ref[aw:aw + bw, :], preferred_element_type=F32)
         + jnp.dot(c_ref[rows, :].astype(BF16), wo_ref[aw + bw:, :], preferred_element_type=F32))
    x1 = x_ref[rows, :] + y
    x1_ref[rows, :] = x1
    hi, lo = _split_bf16(_rms(x1, g2_ref[...]))
    nt = lambda r, h: lax.dot_general(r, h, (((1,), (1,)), ((), ())), preferred_element_type=F32)
    lt = nt(rhi_ref[...], hi) + nt(rhi_ref[...], lo) + nt(rlo_ref[...], hi) + rb_ref[...]
    row = lambda i: lt[i:i + 1, :]
    best, gi = row(0), jnp.zeros_like(row(0), dtype=I32)
    for i in range(1, N_GROUPS):
        up = row(i) > best
        best = jnp.where(up, row(i), best)
        gi = jnp.where(up, i, gi)
    g_w = 1.0 / sum(jnp.exp(row(i) - best) for i in range(N_GROUPS))
    el = []
    for j in range(EXPERTS_PER_GROUP):
        e = row(N_GROUPS + (N_GROUPS - 1) * EXPERTS_PER_GROUP + j)
        for g in range(N_GROUPS - 2, -1, -1):
            e = jnp.where(gi == g, row(N_GROUPS + g * EXPERTS_PER_GROUP + j), e)
        el.append(e)
    v1, i1 = el[0], jnp.zeros_like(gi)
    for j in range(1, EXPERTS_PER_GROUP):
        up = el[j] > v1
        v1 = jnp.where(up, el[j], v1)
        i1 = jnp.where(up, j, i1)
    v2, i2 = jnp.full_like(v1, -jnp.inf), jnp.zeros_like(gi)
    for j in range(EXPERTS_PER_GROUP):
        up = jnp.logical_and(i1 != j, el[j] > v2)
        v2 = jnp.where(up, el[j], v2)
        i2 = jnp.where(up, j, i2)
    e21 = jnp.exp(v2 - v1)
    w1 = g_w / (1.0 + e21)
    w2 = g_w * e21 / (1.0 + e21)
    e1 = (gi * EXPERTS_PER_GROUP + i1).astype(F32)
    e2 = (gi * EXPERTS_PER_GROUP + i2).astype(F32)
    zero = jnp.zeros_like(w1)
    route_ref[:, rows] = jnp.concatenate([e1, e2, w1, w2, zero, zero, zero, zero], axis=0)


def _outproj(x, a, b, c, wo_bf16, g2, router_g, router_gb, router_e, router_eb, tm=512):
    t, d = x.shape
    n_logits = N_GROUPS + N_EXPERTS
    n_rows = -(-n_logits // 16) * 16
    r = jnp.pad(jnp.concatenate([router_g, router_e], axis=1).T, ((0, n_rows - n_logits), (0, 0)))
    rhi = r.astype(BF16)
    rlo = (r - rhi.astype(F32)).astype(BF16)
    rb = jnp.pad(jnp.concatenate([router_gb, router_eb]), (0, n_rows - n_logits)).reshape(n_rows, 1)
    row_spec = lambda w: pl.BlockSpec((tm, w), lambda i: (i, 0))
    full = lambda arr: pl.BlockSpec(arr.shape, lambda i: (0, 0))
    g2r = g2.reshape(1, d)
    return pl.pallas_call(
        _outproj_body,
        grid=(t // tm,),
        in_specs=[row_spec(d), row_spec(a.shape[1]), row_spec(b.shape[1]), row_spec(c.shape[1]),
                  full(wo_bf16), full(g2r), full(rhi), full(rlo), full(rb)],
        out_specs=[row_spec(d), pl.BlockSpec((8, tm), lambda i: (0, i))],
        out_shape=[jax.ShapeDtypeStruct((t, d), F32), jax.ShapeDtypeStruct((8, t), F32)],
        compiler_params=_params(("parallel",)),
        name="outproj_route",
    )(x, a, b, c, wo_bf16, g2r, rhi, rlo, rb)


def _moe_plan(eid, n_tiles):
    flat = eid.reshape(-1)
    onehot = (flat[:, None] == jnp.arange(N_EXPERTS, dtype=I32)[None, :]).astype(I32)
    csum = jnp.cumsum(onehot, axis=0)
    counts = csum[-1]
    padded = ((counts + MOE_TILE - 1) // MOE_TILE) * MOE_TILE
    ends = jnp.cumsum(padded)
    starts = ends - padded
    pos = jnp.sum(onehot * (csum - 1 + starts[None, :]), axis=1)
    tile_start = jnp.arange(n_tiles, dtype=I32) * MOE_TILE
    tile_expert = jnp.sum((ends[None, :] <= tile_start[:, None]).astype(I32), axis=1)
    tile_expert = jnp.minimum(tile_expert, N_EXPERTS - 1).astype(I32)
    used = (ends[-1] // MOE_TILE).astype(I32).reshape(1)
    pad_start = (starts + counts).astype(I32)
    pad_count = (padded - counts).astype(I32)
    return pos.astype(I32), tile_expert, used, pad_start, pad_count


def _store_token_major(ref, x, tok0=0):
    n = x.shape[0]
    for c in range(SUBLANES):
        ref[pl.ds(tok0 * SUBLANES + c, n, stride=SUBLANES), :] = x[:, c * LANES:(c + 1) * LANES]


def _load_token_major(ref, n, tok0=0):
    return jnp.concatenate([ref[pl.ds(tok0 * SUBLANES + c, n, stride=SUBLANES), :] for c in range(SUBLANES)],
                           axis=1)


def _tokens(ref, tok, n=1):
    return ref.at[pl.ds(pl.multiple_of(tok * SUBLANES, SUBLANES), n * SUBLANES)]


def _dispatch_body(pad_start_ref, pad_count_ref, used_ref, pos_ref, x_ref, xs_hbm, xt, zeros, sem, fill_sem,
                   *, tm, n_tokens, n_tiles):
    i = pl.program_id(0)
    half_tile = MOE_TILE // 2

    def fill(act):
        def expert_pad(e, c):
            off, n = pad_start_ref[e], pad_count_ref[e]
            bit = half_tile
            while bit:
                take = (n & bit) != 0

                @pl.when(take)
                def _(off=off, bit=bit):
                    act(pltpu.make_async_copy(_tokens(zeros, 0, bit), _tokens(xs_hbm, off, bit), fill_sem))

                off = off + jnp.where(take, bit, 0)
                bit //= 2
            return c

        lax.fori_loop(0, N_EXPERTS, expert_pad, 0)

        def unused_half_tile(j, c):
            act(pltpu.make_async_copy(zeros, _tokens(xs_hbm, j * half_tile, half_tile), fill_sem))
            return c

        lax.fori_loop(2 * used_ref[0], 2 * n_tiles, unused_half_tile, 0)

    @pl.when(i == 0)
    def _():
        zeros[...] = jnp.zeros_like(zeros)
        fill(lambda cp: cp.start())
        fill(lambda cp: cp.wait())

    _store_token_major(xt, x_ref[...])

    def issue(g, c):
        for k in range(2):
            base = k * n_tokens + i * tm + g * ROW_UNROLL
            for j in range(ROW_UNROLL):
                dst = _tokens(xs_hbm, pos_ref[base + j])
                pltpu.make_async_copy(_tokens(xt, g * ROW_UNROLL + j), dst, sem).start(priority=j % 2)
        return c

    lax.fori_loop(0, tm // ROW_UNROLL, issue, 0)
    for k in range(2):
        pltpu.make_async_copy(xt, _tokens(xs_hbm, 0, tm), sem).wait()


def _dispatch(x1, pos, n_tiles, pad_start, pad_count, used, tm=512):
    t, d = x1.shape
    assert d == SUBLANES * LANES
    grid_spec = pltpu.PrefetchScalarGridSpec(
        num_scalar_prefetch=4,
        grid=(t // tm,),
        in_specs=[pl.BlockSpec((tm, d), lambda i, *_: (i, 0))],
        out_specs=pl.BlockSpec(memory_space=pl.ANY),
        scratch_shapes=[pltpu.VMEM((tm * SUBLANES, LANES), F32),
                        pltpu.VMEM((MOE_TILE // 2 * SUBLANES, LANES), F32),
                        pltpu.SemaphoreType.DMA, pltpu.SemaphoreType.DMA],
    )
    return pl.pallas_call(
        functools.partial(_dispatch_body, tm=tm, n_tokens=t, n_tiles=n_tiles),
        grid_spec=grid_spec,
        out_shape=jax.ShapeDtypeStruct((n_tiles * MOE_TILE * SUBLANES, LANES), F32),
        compiler_params=_params(("arbitrary",)),
        name="moe_dispatch",
    )(pad_start, pad_count, used, pos, x1)


def _ffn_body(te_ref, used_ref, xs_ref, g2_ref, wgu_ref, wd_ref, ys_ref):
    live = pl.program_id(0) < used_ref[0]

    @pl.when(live)
    def _():
        hid = wd_ref.shape[0]
        half = MOE_TILE // 2
        for tok0 in (0, half):
            hb = _rms(_load_token_major(xs_ref, half, tok0), g2_ref[...]).astype(BF16)
            gu = jnp.dot(hb, wgu_ref[...], preferred_element_type=F32)
            gate, up = gu[:, :hid], gu[:, hid:]
            act = (gate * jax.nn.sigmoid(gate) * up).astype(BF16)
            _store_token_major(ys_ref, jnp.dot(act, wd_ref[...], preferred_element_type=F32), tok0)

    @pl.when(jnp.logical_not(live))
    def _():
        ys_ref[...] = jnp.zeros_like(ys_ref)


def _ffn(xs, g2, wgu_bf16, wd_bf16, tile_expert, used):
    n_tiles = xs.shape[0] // (MOE_TILE * SUBLANES)
    d, hid = wd_bf16.shape[2], wd_bf16.shape[1]
    tile_rows = MOE_TILE * SUBLANES
    live = lambda i, te, used: jnp.minimum(i, used[0] - 1)
    grid_spec = pltpu.PrefetchScalarGridSpec(
        num_scalar_prefetch=2,
        grid=(n_tiles,),
        in_specs=[pl.BlockSpec((tile_rows, LANES), lambda i, te, used: (live(i, te, used), 0)),
                  pl.BlockSpec((1, d), lambda i, te, used: (0, 0)),
                  pl.BlockSpec((None, d, 2 * hid), lambda i, te, used: (te[i], 0, 0)),
                  pl.BlockSpec((None, hid, d), lambda i, te, used: (te[i], 0, 0))],
        out_specs=pl.BlockSpec((tile_rows, LANES), lambda i, te, used: (i, 0)),
    )
    return pl.pallas_call(
        _ffn_body,
        grid_spec=grid_spec,
        out_shape=jax.ShapeDtypeStruct(xs.shape, F32),
        compiler_params=_params(("arbitrary",)),
        name="moe_experts",
    )(tile_expert, used, xs, g2.reshape(1, d), wgu_bf16, wd_bf16)


def _combine_body(pos_ref, ys_hbm, x1_ref, w_ref, gf_ref, o_ref, buf, sems, *, tm, n_tokens, final_norm):
    i = pl.program_id(0)
    slot = i % 2

    def gather(step, s):
        def issue(g, c):
            for k in range(2):
                base = k * n_tokens + step * tm + g * ROW_UNROLL
                for j in range(ROW_UNROLL):
                    src = _tokens(ys_hbm, pos_ref[base + j])
                    pltpu.make_async_copy(src, _tokens(buf.at[s, k], g * ROW_UNROLL + j),
                                          sems.at[s]).start(priority=j % 2)
            return c

        lax.fori_loop(0, tm // ROW_UNROLL, issue, 0)

    @pl.when(i == 0)
    def _():
        gather(i, slot)

    @pl.when(i + 1 < pl.num_programs(0))
    def _():
        gather(i + 1, 1 - slot)

    for k in range(2):
        pltpu.make_async_copy(_tokens(ys_hbm, 0, tm), buf.at[slot, k], sems.at[slot]).wait()
    w = w_ref[...]
    y = (x1_ref[...] + w[:, 0:1] * _load_token_major(buf.at[slot, 0], tm)
         + w[:, 1:2] * _load_token_major(buf.at[slot, 1], tm))
    if final_norm:
        y = _rms(y, gf_ref[...])
    o_ref[...] = y


def _combine(ys, x1, pos, w2, gf, final_norm, tm=256):
    t, d = x1.shape
    grid_spec = pltpu.PrefetchScalarGridSpec(
        num_scalar_prefetch=1,
        grid=(t // tm,),
        in_specs=[pl.BlockSpec(memory_space=pl.ANY),
                  pl.BlockSpec((tm, d), lambda i, *_: (i, 0)),
                  pl.BlockSpec((tm, 2), lambda i, *_: (i, 0)),
                  pl.BlockSpec((1, d), lambda i, *_: (0, 0))],
        out_specs=pl.BlockSpec((tm, d), lambda i, *_: (i, 0)),
        scratch_shapes=[pltpu.VMEM((2, 2, tm * SUBLANES, LANES), F32), pltpu.SemaphoreType.DMA((2,))],
    )
    return pl.pallas_call(
        functools.partial(_combine_body, tm=tm, n_tokens=t, final_norm=final_norm),
        grid_spec=grid_spec,
        out_shape=jax.ShapeDtypeStruct((t, d), F32),
        compiler_params=_params(("arbitrary",)),
        name="moe_combine",
    )(pos, ys, x1, w2, gf.reshape(1, d))


def _layer(x, batch, seq, norm1, w_in, a_ln_g, a_ln_b, a_ws, a_bs, ret_gn, w_out, norm2,
           router_g, router_gb, router_e, router_eb, w_gate, w_up, w_down, final_gain, is_last):
    t, d = x.shape
    a_groups = a_ws.shape[0]
    aw = a_groups * HEAD_DIM
    c_heads = ret_gn.shape[0] // HEAD_DIM
    cw = c_heads * HEAD_DIM
    bw = w_in.shape[1] - 2 * aw - 4 * cw
    b_heads = (bw // 3) // HEAD_DIM
    z = _inproj(x, norm1, w_in.astype(BF16))
    z3 = z.reshape(batch, seq, z.shape[1])
    a_out = _sgu(z, a_ln_g, a_ln_b, a_ws, a_bs)
    b_out = _attn(z3, (2 * aw) // LANES, b_heads).reshape(t, bw // 3)
    c_out = _retention(z3, (2 * aw + bw) // cw, c_heads, ret_gn).reshape(t, cw)
    x1, route = _outproj(x, a_out, b_out, c_out, w_out.astype(BF16), norm2,
                         router_g, router_gb, router_e, router_eb)
    eid = route[0:2].astype(I32)
    w2 = route[2:4].T
    n_tiles = (2 * t) // MOE_TILE + N_EXPERTS
    pos, tile_expert, used, pad_start, pad_count = _moe_plan(eid, n_tiles)
    xs = _dispatch(x1, pos, n_tiles, pad_start, pad_count, used)
    wgu = jnp.concatenate([w_gate, w_up], axis=2).astype(BF16)
    ys = _ffn(xs, norm2, wgu, w_down.astype(BF16), tile_expert, used)
    return _combine(ys, x1, pos, w2, final_gain, is_last)


def kernel(x, norm1, w_in, a_ln_g, a_ln_b, a_ws, a_bs, ret_gn, w_out, norm2, router_g, router_gb, router_e,
           router_eb, w_gate, w_up, w_down, final_norm):
    batch, seq, d = x.shape
    depth = norm1.shape[0]
    h = x.reshape(batch * seq, d)
    for l in range(depth):
        h = _layer(h, batch, seq, norm1[l], w_in[l], a_ln_g[l], a_ln_b[l], a_ws[l], a_bs[l], ret_gn[l],
                   w_out[l], norm2[l], router_g[l], router_gb[l], router_e[l], router_eb[l],
                   w_gate[l], w_up[l], w_down[l], final_norm, l == depth - 1)
    return h.reshape(batch, seq, d)
```

```python
import functools

import numpy as np
import jax
import jax.numpy as jnp
from jax import lax
from jax.experimental import pallas as pl
from jax.experimental.pallas import tpu as pltpu

F32 = jnp.float32
BF16 = jnp.bfloat16
I32 = jnp.int32

EPS = 1e-6
HEAD_DIM = 64
SGU_CHUNK = 128
RET_CHUNK = 128
ATTN_BLOCK = 128
DILATIONS = (1, 4, 16)
ATTN_UNROLL = 8
ROPE_BASE = 10000.0
N_GROUPS = 4
EXPERTS_PER_GROUP = 4
N_EXPERTS = N_GROUPS * EXPERTS_PER_GROUP
LANES = 128
SUBLANES = 8
MOE_TILE = 512
ROW_UNROLL = 8
VMEM_LIMIT = 56 * 1024 * 1024


def _params(sem, vmem=VMEM_LIMIT):
    return pltpu.CompilerParams(dimension_semantics=sem, vmem_limit_bytes=vmem)


def _rms(x, g):
    return x * lax.rsqrt(jnp.mean(x * x, axis=-1, keepdims=True) + EPS) * g


def _split_bf16(a):
    hi = a.astype(BF16)
    lo = (a - hi.astype(F32)).astype(BF16)
    return hi, lo


def _cast_rows(dst_ref, src_ref, rows=256):
    for r0 in range(0, src_ref.shape[0], rows):
        dst_ref[r0:r0 + rows, :] = src_ref[r0:r0 + rows, :].astype(dst_ref.dtype)


def _inproj_body(x_ref, g_ref, w_ref, *refs, n_chunk):
    out_refs, wb = refs[:-1], refs[-1]

    @pl.when(pl.program_id(0) == 0)
    def _():
        _cast_rows(wb, w_ref)

    hb = _rms(x_ref[...], g_ref[...]).astype(BF16)
    col = 0
    for o_ref in out_refs:
        for n0 in range(0, o_ref.shape[1], n_chunk):
            n1 = min(n0 + n_chunk, o_ref.shape[1])
            acc = jnp.dot(hb, wb[:, col + n0:col + n1], preferred_element_type=F32)
            o_ref[:, n0:n1] = acc.astype(o_ref.dtype)
        col += o_ref.shape[1]


def _resident(shape):
    return pl.BlockSpec(shape, lambda *_: (0,) * len(shape), pipeline_mode=pl.Buffered(1))


def _inproj(x, g, w, widths, dtypes, tm=512, n_chunk=512):
    t, d = x.shape
    assert sum(widths) == w.shape[1]
    return pl.pallas_call(
        functools.partial(_inproj_body, n_chunk=n_chunk),
        grid=(t // tm,),
        in_specs=[pl.BlockSpec((tm, d), lambda i: (i, 0)),
                  pl.BlockSpec((1, d), lambda i: (0, 0)),
                  _resident(w.shape)],
        out_specs=[pl.BlockSpec((tm, n), lambda i: (i, 0)) for n in widths],
        out_shape=[jax.ShapeDtypeStruct((t, n), dt) for n, dt in zip(widths, dtypes)],
        scratch_shapes=[pltpu.VMEM(w.shape, BF16)],
        compiler_params=_params(("arbitrary",)),
        name="inproj",
    )(x, g.reshape(1, d), w)


def _gelu_tanh(x):
    return 0.5 * x * (1.0 + jnp.tanh(np.sqrt(2.0 / np.pi).astype(np.float32) * (x + 0.044715 * (x * x * x))))


def _sgu_body(za_ref, lng_ref, lnb_ref, wcat_ref, bias_ref, o_ref):
    ga = _gelu_tanh(za_ref[...])
    aw = ga.shape[1] // 2
    u = ga[:, :aw]
    v = ga[:, aw:]
    mu = jnp.mean(v, axis=-1, keepdims=True)
    dv = v - mu
    var = jnp.mean(dv * dv, axis=-1, keepdims=True)
    vn = dv * lax.rsqrt(var + EPS) * lng_ref[...] + lnb_ref[...]
    group = lax.broadcasted_iota(I32, (1, aw), 1) // HEAD_DIM
    n_groups = aw // HEAD_DIM
    for c in range(ga.shape[0] // SGU_CHUNK):
        rows = slice(c * SGU_CHUNK, (c + 1) * SGU_CHUNK)
        vc = vn[rows]
        stack = jnp.concatenate([jnp.where(group == g, vc, 0.0) for g in range(n_groups)], axis=0)
        s = jnp.dot(wcat_ref[...], stack.astype(BF16), preferred_element_type=F32) + bias_ref[...]
        o_ref[rows, :] = (u[rows] * s).astype(o_ref.dtype)


def _sgu(z, ln_g, ln_b, w_s, b_s, tm=512):
    t = z.shape[0]
    n_groups = w_s.shape[0]
    aw = n_groups * HEAD_DIM
    causal = jnp.tril(jnp.ones((SGU_CHUNK, SGU_CHUNK), dtype=bool))
    wcat = jnp.where(causal[None], w_s, 0.0).transpose(1, 0, 2).reshape(SGU_CHUNK, n_groups * SGU_CHUNK)
    bias = jnp.repeat(b_s.T, HEAD_DIM, axis=1)
    return pl.pallas_call(
        _sgu_body,
        grid=(t // tm,),
        in_specs=[pl.BlockSpec((tm, 2 * aw), lambda i: (i, 0)),
                  pl.BlockSpec((1, aw), lambda i: (0, 0)),
                  pl.BlockSpec((1, aw), lambda i: (0, 0)),
                  pl.BlockSpec(wcat.shape, lambda i: (0, 0)),
                  pl.BlockSpec(bias.shape, lambda i: (0, 0))],
        out_specs=pl.BlockSpec((tm, aw), lambda i: (i, 0)),
        out_shape=jax.ShapeDtypeStruct((t, aw), BF16),
        compiler_params=_params(("parallel",)),
        name="sgu",
    )(z, ln_g.reshape(1, aw), ln_b.reshape(1, aw), wcat.astype(BF16), bias)


def _attn_bias():
    blk = ATTN_BLOCK
    qi = np.arange(2 * blk)[:, None] % blk
    ci = np.arange(2 * blk)[None, :]
    first = (ci < blk) & (ci <= qi)
    later = np.where(ci < blk, ci >= qi, ci - blk <= qi)
    return np.where(np.stack([first, later]), 0.0, -np.inf).astype(np.float32)


def _attn_body(q_ref, k_ref, v_ref, bias_ref, o_ref, qkv_scr, ob_scr, lse_scr, *, seq, unroll):
    blk = ATTN_BLOCK
    head0 = lax.broadcasted_iota(I32, (1, LANES), 1) < HEAD_DIM
    scale = HEAD_DIM ** -0.5

    widen_rows = 512

    def widen(i, carry):
        sl = pl.ds(pl.multiple_of(i * widen_rows, widen_rows), widen_rows)
        qkv_scr[0, sl, :] = q_ref[0, sl, :].astype(F32) * scale
        qkv_scr[1, sl, :] = k_ref[0, sl, :].astype(F32)
        qkv_scr[2, sl, :] = v_ref[0, sl, :].astype(F32)
        return carry

    lax.fori_loop(0, seq // widen_rows, widen, 0)
    q2, k2, v2 = qkv_scr.at[0], qkv_scr.at[1], qkv_scr.at[2]

    def rows(start, n, d):
        return pl.ds(start, n) if d == 1 else pl.ds(start, n, stride=d)

    def block(p, d, r, n):
        base = n * (blk * d) + r
        kstart = jnp.maximum(base - blk * d, r)
        if d == 1:
            base, kstart = pl.multiple_of(base, blk), pl.multiple_of(kstart, blk)
        qb = q2[rows(base, blk, d), :]
        qs = jnp.concatenate([jnp.where(head0, qb, 0.0), jnp.where(head0, 0.0, qb)], axis=0).astype(BF16)
        kb = k2[rows(kstart, 2 * blk, d), :].astype(BF16)
        vb = v2[rows(kstart, 2 * blk, d), :].astype(BF16)
        va = jnp.concatenate([vb, jnp.ones_like(vb)], axis=1)
        s = lax.dot_general(qs, kb, (((1,), (1,)), ((), ())), preferred_element_type=F32)
        s = s + bias_ref[jnp.minimum(n, 1)]
        m = jnp.max(s, axis=-1, keepdims=True)
        e = jnp.exp(s - m)
        oa = jnp.dot(e.astype(BF16), va, preferred_element_type=F32)
        l = oa[:, LANES:]
        o = oa[:, :LANES] / l
        lse = m + jnp.log(l)
        ob_scr[p, rows(base, blk, d), :] = jnp.where(head0, o[:blk], o[blk:])
        lse_scr[p, rows(base, blk, d), :] = jnp.where(head0, lse[:blk], lse[blk:])

    for p, d in enumerate(DILATIONS):
        def group(i, carry, p=p, d=d):
            for u in range(unroll):
                b = i * unroll + u
                block(p, d, b & (d - 1), b >> (d.bit_length() - 1))
            return carry

        lax.fori_loop(0, seq // (blk * unroll), group, 0)

    step = 256

    def mix(i, carry):
        sl = pl.ds(pl.multiple_of(i * step, step), step)
        ls = [lse_scr[p, sl, :] for p in range(len(DILATIONS))]
        m = functools.reduce(jnp.maximum, ls)
        es = [jnp.exp(l - m) for l in ls]
        num = sum(e * ob_scr[p, sl, :] for p, e in enumerate(es))
        o_ref[0, sl, :] = (num / sum(es)).astype(o_ref.dtype)
        return carry

    lax.fori_loop(0, seq // step, mix, 0)


def _attn(z3, q_col, n_heads):
    b, s, _ = z3.shape
    n_pairs = n_heads * HEAD_DIM // LANES
    assert s % (ATTN_BLOCK * ATTN_UNROLL) == 0 and s % (2 * ATTN_BLOCK * max(DILATIONS)) == 0
    assert all(d & (d - 1) == 0 for d in DILATIONS)
    bias = jnp.asarray(_attn_bias())

    def spec(off):
        return pl.BlockSpec((1, s, LANES), lambda i, j: (i, 0, off + j))

    return pl.pallas_call(
        functools.partial(_attn_body, seq=s, unroll=ATTN_UNROLL),
        grid=(b, n_pairs),
        in_specs=[spec(q_col), spec(q_col + n_pairs), spec(q_col + 2 * n_pairs),
                  pl.BlockSpec(bias.shape, lambda i, j: (0, 0, 0))],
        out_specs=pl.BlockSpec((1, s, LANES), lambda i, j: (i, 0, j)),
        out_shape=jax.ShapeDtypeStruct((b, s, n_pairs * LANES), BF16),
        scratch_shapes=[pltpu.VMEM((3, s, LANES), F32),
                        pltpu.VMEM((len(DILATIONS), s, LANES), F32),
                        pltpu.VMEM((len(DILATIONS), s, LANES), F32)],
        compiler_params=_params(("parallel", "parallel")),
        name="dilated_attn",
    )(z3, z3, z3, bias)


def _ret_body(q_ref, k_ref, v_ref, g_ref, cos_ref, sa_ref, sb_ref, dec_ref, qdec_ref, kdec_ref, cdec_ref,
              avg_ref, gn_ref, o_ref, state):
    cw = q_ref.shape[2]
    n_heads = cw // HEAD_DIM
    head = lax.broadcasted_iota(I32, (1, cw), 1) // HEAD_DIM
    hr = lax.broadcasted_iota(I32, (cw, cw), 0) // HEAD_DIM
    hc = lax.broadcasted_iota(I32, (cw, cw), 1) // HEAD_DIM
    same_head = hr == hc

    @pl.when(pl.program_id(1) == 0)
    def _():
        state[...] = jnp.zeros_like(state)

    def rotary(x, cos, sa, sb):
        half = HEAD_DIM // 2
        return x * cos + pltpu.roll(x, half, 1) * sa + pltpu.roll(x, cw - half, 1) * sb

    def head_mean(a):
        hi, lo = _split_bf16(a)
        return (jnp.dot(hi, avg_ref[...], preferred_element_type=F32)
                + jnp.dot(lo, avg_ref[...], preferred_element_type=F32))

    for c in range(q_ref.shape[1] // RET_CHUNK):
        rows = slice(c * RET_CHUNK, (c + 1) * RET_CHUNK)
        cos, sa, sb = cos_ref[rows, :], sa_ref[rows, :], sb_ref[rows, :]
        qr = rotary(q_ref[0, rows, :].astype(F32), cos, sa, sb)
        kr = rotary(k_ref[0, rows, :].astype(F32), cos, sa, sb) * (HEAD_DIM ** -0.5)
        v = v_ref[0, rows, :]
        krb = kr.astype(BF16)
        inner = [lax.dot_general(jnp.where(head == h, qr, 0.0).astype(BF16), krb, (((1,), (1,)), ((), ())),
                                 preferred_element_type=F32) for h in range(n_heads)]
        inner = jnp.concatenate(inner, axis=1) * dec_ref[...]
        vstack = jnp.concatenate([jnp.where(head == h, v, 0.0) for h in range(n_heads)], axis=0)
        st = state[...]
        o = (jnp.dot(inner.astype(BF16), vstack.astype(BF16), preferred_element_type=F32)
             + jnp.dot((qr * qdec_ref[...]).astype(BF16), st.astype(BF16), preferred_element_type=F32))
        ktv = lax.dot_general((kr * kdec_ref[...]).astype(BF16), v.astype(BF16), (((0,), (0,)), ((), ())),
                              preferred_element_type=F32)
        state[...] = st * cdec_ref[...] + jnp.where(same_head, ktv, 0.0)
        mu = head_mean(o)
        dd = o - mu
        var = head_mean(dd * dd)
        on = dd * lax.rsqrt(var + EPS) * gn_ref[...]
        gate = g_ref[0, rows, :]
        o_ref[0, rows, :] = (gate * jax.nn.sigmoid(gate) * on).astype(o_ref.dtype)


def _ret_tables(seq, n_heads):
    half = HEAD_DIM // 2
    inv = ROPE_BASE ** (-jnp.arange(half, dtype=F32) / half)
    ang = jnp.arange(seq, dtype=F32)[:, None] * inv[None]
    cos, sin = jnp.cos(ang), jnp.sin(ang)
    zero = jnp.zeros_like(sin)
    tile = lambda a: jnp.tile(a, (1, n_heads))
    cos_t = tile(jnp.concatenate([cos, cos], axis=1))
    sa_t = tile(jnp.concatenate([zero, sin], axis=1))
    sb_t = tile(jnp.concatenate([-sin, zero], axis=1))
    log_g = jnp.log(1.0 - 2.0 ** (-5.0 - jnp.arange(n_heads, dtype=F32)))
    c = RET_CHUNK
    idx = jnp.arange(c)
    diff = idx[:, None] - idx[None, :]
    decay_in = jnp.where(diff >= 0, jnp.exp(log_g[:, None, None] * jnp.maximum(diff, 0)[None]), 0.0)
    dec = decay_in.transpose(1, 0, 2).reshape(c, n_heads * c)
    lane_head = jnp.repeat(jnp.arange(n_heads), HEAD_DIM)
    qdec = jnp.exp(log_g[lane_head][None, :] * (idx + 1)[:, None].astype(F32))
    kdec = jnp.exp(log_g[lane_head][None, :] * (c - 1 - idx)[:, None].astype(F32))
    same = lane_head[:, None] == lane_head[None, :]
    cdec = jnp.where(same, jnp.exp(log_g * c)[lane_head][:, None], 0.0)
    avg = jnp.where(same, 1.0 / HEAD_DIM, 0.0).astype(BF16)
    return cos_t, sa_t, sb_t, dec, qdec, kdec, cdec, avg


def _retention(z3, col, gate3, n_heads, gn, tc=512):
    b, s, _ = z3.shape
    cw = n_heads * HEAD_DIM
    cos_t, sa_t, sb_t, dec, qdec, kdec, cdec, avg = _ret_tables(s, n_heads)

    def zspec(off):
        return pl.BlockSpec((1, tc, cw), lambda i, j: (i, j, off))

    tab = pl.BlockSpec((tc, cw), lambda i, j: (j, 0))
    full = lambda a: pl.BlockSpec(a.shape, lambda i, j: (0, 0))
    gn2 = gn.reshape(1, cw)
    return pl.pallas_call(
        _ret_body,
        grid=(b, s // tc),
        in_specs=[zspec(col), zspec(col + 1), zspec(col + 2), zspec(0), tab, tab, tab,
                  full(dec), full(qdec), full(kdec), full(cdec), full(avg), full(gn2)],
        out_specs=pl.BlockSpec((1, tc, cw), lambda i, j: (i, j, 0)),
        out_shape=jax.ShapeDtypeStruct((b, s, cw), BF16),
        scratch_shapes=[pltpu.VMEM((cw, cw), F32)],
        compiler_params=_params(("parallel", "arbitrary")),
        name="retention",
    )(z3, z3, z3, gate3, cos_t, sa_t, sb_t, dec, qdec, kdec, cdec, avg, gn2)


def _outproj_body(x_ref, a_ref, b_ref, c_ref, wo_ref, g2_ref, rhi_ref, rlo_ref, rb_ref, x1_ref, route_ref, wb):
    @pl.when(pl.program_id(0) == 0)
    def _():
        _cast_rows(wb, wo_ref)

    aw, bw = a_ref.shape[1], b_ref.shape[1]
    y = (jnp.dot(a_ref[...], wb[0:aw, :], preferred_element_type=F32)
         + jnp.dot(b_ref[...], wb[aw:aw + bw, :], preferred_element_type=F32)
         + jnp.dot(c_ref[...], wb[aw + bw:, :], preferred_element_type=F32))
    x1 = x_ref[...] + y
    x1_ref[...] = x1
    hi, lo = _split_bf16(_rms(x1, g2_ref[...]))
    nt = lambda r, h: lax.dot_general(r, h, (((1,), (1,)), ((), ())), preferred_element_type=F32)
    lt = nt(rhi_ref[...], hi) + nt(rhi_ref[...], lo) + nt(rlo_ref[...], hi) + rb_ref[...]
    row = lambda i: lt[i:i + 1, :]
    best, gi = row(0), jnp.zeros_like(row(0), dtype=I32)
    for i in range(1, N_GROUPS):
        up = row(i) > best
        best = jnp.where(up, row(i), best)
        gi = jnp.where(up, i, gi)
    g_w = 1.0 / sum(jnp.exp(row(i) - best) for i in range(N_GROUPS))
    el = []
    for j in range(EXPERTS_PER_GROUP):
        e = row(N_GROUPS + (N_GROUPS - 1) * EXPERTS_PER_GROUP + j)
        for g in range(N_GROUPS - 2, -1, -1):
            e = jnp.where(gi == g, row(N_GROUPS + g * EXPERTS_PER_GROUP + j), e)
        el.append(e)
    v1, i1 = el[0], jnp.zeros_like(gi)
    for j in range(1, EXPERTS_PER_GROUP):
        up = el[j] > v1
        v1 = jnp.where(up, el[j], v1)
        i1 = jnp.where(up, j, i1)
    v2, i2 = jnp.full_like(v1, -jnp.inf), jnp.zeros_like(gi)
    for j in range(EXPERTS_PER_GROUP):
        up = jnp.logical_and(i1 != j, el[j] > v2)
        v2 = jnp.where(up, el[j], v2)
        i2 = jnp.where(up, j, i2)
    e21 = jnp.exp(v2 - v1)
    w1 = g_w / (1.0 + e21)
    w2 = g_w * e21 / (1.0 + e21)
    e1 = (gi * EXPERTS_PER_GROUP + i1).astype(F32)
    e2 = (gi * EXPERTS_PER_GROUP + i2).astype(F32)
    zero = jnp.zeros_like(w1)
    route_ref[...] = jnp.concatenate([e1, e2, w1, w2, zero, zero, zero, zero], axis=0)


def _outproj(x, a, b, c, wo, g2, router_g, router_gb, router_e, router_eb, tm=512):
    t, d = x.shape
    n_logits = N_GROUPS + N_EXPERTS
    n_rows = -(-n_logits // 16) * 16
    r = jnp.pad(jnp.concatenate([router_g, router_e], axis=1).T, ((0, n_rows - n_logits), (0, 0)))
    rhi = r.astype(BF16)
    rlo = (r - rhi.astype(F32)).astype(BF16)
    rb = jnp.pad(jnp.concatenate([router_gb, router_eb]), (0, n_rows - n_logits)).reshape(n_rows, 1)
    row_spec = lambda w: pl.BlockSpec((tm, w), lambda i: (i, 0))
    full = lambda arr: pl.BlockSpec(arr.shape, lambda i: (0, 0))
    g2r = g2.reshape(1, d)
    return pl.pallas_call(
        _outproj_body,
        grid=(t // tm,),
        in_specs=[row_spec(d), row_spec(a.shape[1]), row_spec(b.shape[1]), row_spec(c.shape[1]),
                  _resident(wo.shape), full(g2r), full(rhi), full(rlo), full(rb)],
        out_specs=[row_spec(d), pl.BlockSpec((8, tm), lambda i: (0, i))],
        out_shape=[jax.ShapeDtypeStruct((t, d), F32), jax.ShapeDtypeStruct((8, t), F32)],
        scratch_shapes=[pltpu.VMEM(wo.shape, BF16)],
        compiler_params=_params(("arbitrary",)),
        name="outproj_route",
    )(x, a, b, c, wo, g2r, rhi, rlo, rb)


def _moe_plan(eid, n_tiles):
    flat = eid.reshape(-1)
    onehot = (flat[:, None] == jnp.arange(N_EXPERTS, dtype=I32)[None, :]).astype(I32)
    csum = jnp.cumsum(onehot, axis=0)
    counts = csum[-1]
    padded = ((counts + MOE_TILE - 1) // MOE_TILE) * MOE_TILE
    ends = jnp.cumsum(padded)
    starts = ends - padded
    pos = jnp.sum(onehot * (csum - 1 + starts[None, :]), axis=1)
    tile_start = jnp.arange(n_tiles, dtype=I32) * MOE_TILE
    tile_expert = jnp.sum((ends[None, :] <= tile_start[:, None]).astype(I32), axis=1)
    tile_expert = jnp.minimum(tile_expert, N_EXPERTS - 1).astype(I32)
    used = (ends[-1] // MOE_TILE).astype(I32).reshape(1)
    pad_start = (starts + counts).astype(I32)
    pad_count = (padded - counts).astype(I32)
    return pos.astype(I32), tile_expert, used, pad_start, pad_count


def _store_token_major(ref, x, tok0=0):
    n = x.shape[0]
    for c in range(SUBLANES):
        ref[pl.ds(tok0 * SUBLANES + c, n, stride=SUBLANES), :] = x[:, c * LANES:(c + 1) * LANES]


def _load_token_major(ref, n, tok0=0):
    return jnp.concatenate([ref[pl.ds(tok0 * SUBLANES + c, n, stride=SUBLANES), :] for c in range(SUBLANES)],
                           axis=1)


def _tokens(ref, tok, n=1):
    return ref.at[pl.ds(pl.multiple_of(tok * SUBLANES, SUBLANES), n * SUBLANES)]


def _dispatch_body(pad_start_ref, pad_count_ref, used_ref, pos_ref, x_ref, xs_hbm, xt, zeros, sem, fill_sem,
                   *, tm, n_tokens, n_tiles):
    i = pl.program_id(0)
    half_tile = MOE_TILE // 2

    def fill(act):
        def expert_pad(e, c):
            off, n = pad_start_ref[e], pad_count_ref[e]
            bit = half_tile
            while bit:
                take = (n & bit) != 0

                @pl.when(take)
                def _(off=off, bit=bit):
                    act(pltpu.make_async_copy(_tokens(zeros, 0, bit), _tokens(xs_hbm, off, bit), fill_sem))

                off = off + jnp.where(take, bit, 0)
                bit //= 2
            return c

        lax.fori_loop(0, N_EXPERTS, expert_pad, 0)

        def unused_half_tile(j, c):
            act(pltpu.make_async_copy(zeros, _tokens(xs_hbm, j * half_tile, half_tile), fill_sem))
            return c

        lax.fori_loop(2 * used_ref[0], 2 * n_tiles, unused_half_tile, 0)

    @pl.when(i == 0)
    def _():
        zeros[...] = jnp.zeros_like(zeros)
        fill(lambda cp: cp.start())
        fill(lambda cp: cp.wait())

    _store_token_major(xt, x_ref[...])

    def issue(g, c):
        for k in range(2):
            base = k * n_tokens + i * tm + g * ROW_UNROLL
            for j in range(ROW_UNROLL):
                dst = _tokens(xs_hbm, pos_ref[base + j])
                pltpu.make_async_copy(_tokens(xt, g * ROW_UNROLL + j), dst, sem).start(priority=j % 2)
        return c

    lax.fori_loop(0, tm // ROW_UNROLL, issue, 0)
    for k in range(2):
        pltpu.make_async_copy(xt, _tokens(xs_hbm, 0, tm), sem).wait()


def _dispatch(x1, pos, n_tiles, pad_start, pad_count, used, tm=512):
    t, d = x1.shape
    assert d == SUBLANES * LANES
    grid_spec = pltpu.PrefetchScalarGridSpec(
        num_scalar_prefetch=4,
        grid=(t // tm,),
        in_specs=[pl.BlockSpec((tm, d), lambda i, *_: (i, 0))],
        out_specs=pl.BlockSpec(memory_space=pl.ANY),
        scratch_shapes=[pltpu.VMEM((tm * SUBLANES, LANES), F32),
                        pltpu.VMEM((MOE_TILE // 2 * SUBLANES, LANES), F32),
                        pltpu.SemaphoreType.DMA, pltpu.SemaphoreType.DMA],
    )
    return pl.pallas_call(
        functools.partial(_dispatch_body, tm=tm, n_tokens=t, n_tiles=n_tiles),
        grid_spec=grid_spec,
        out_shape=jax.ShapeDtypeStruct((n_tiles * MOE_TILE * SUBLANES, LANES), F32),
        compiler_params=_params(("arbitrary",)),
        name="moe_dispatch",
    )(pad_start, pad_count, used, pos, x1)


def _ffn_body(te_ref, used_ref, xs_ref, g2_ref, wg_ref, wu_ref, wd_ref, ys_ref, wgu_b, wd_b):
    i = pl.program_id(0)
    live = i < used_ref[0]
    hid = wd_ref.shape[0]

    @pl.when(jnp.logical_and(live, jnp.logical_or(i == 0, te_ref[i] != te_ref[jnp.maximum(i - 1, 0)])))
    def _():
        _cast_rows(wgu_b.at[:, 0:hid], wg_ref)
        _cast_rows(wgu_b.at[:, hid:2 * hid], wu_ref)
        _cast_rows(wd_b, wd_ref)

    @pl.when(live)
    def _():
        half = MOE_TILE // 2
        for tok0 in (0, half):
            hb = _rms(_load_token_major(xs_ref, half, tok0), g2_ref[...]).astype(BF16)
            gu = jnp.dot(hb, wgu_b[...], preferred_element_type=F32)
            gate, up = gu[:, :hid], gu[:, hid:]
            act = (gate * jax.nn.sigmoid(gate) * up).astype(BF16)
            _store_token_major(ys_ref, jnp.dot(act, wd_b[...], preferred_element_type=F32), tok0)

    @pl.when(jnp.logical_not(live))
    def _():
        ys_ref[...] = jnp.zeros_like(ys_ref)


def _ffn(xs, g2, w_gate, w_up, w_down, tile_expert, used):
    n_tiles = xs.shape[0] // (MOE_TILE * SUBLANES)
    d, hid = w_down.shape[2], w_down.shape[1]
    tile_rows = MOE_TILE * SUBLANES
    live = lambda i, te, used: jnp.minimum(i, used[0] - 1)
    grid_spec = pltpu.PrefetchScalarGridSpec(
        num_scalar_prefetch=2,
        grid=(n_tiles,),
        in_specs=[pl.BlockSpec((tile_rows, LANES), lambda i, te, used: (live(i, te, used), 0)),
                  pl.BlockSpec((1, d), lambda i, te, used: (0, 0)),
                  pl.BlockSpec((None, d, hid), lambda i, te, used: (te[i], 0, 0)),
                  pl.BlockSpec((None, d, hid), lambda i, te, used: (te[i], 0, 0)),
                  pl.BlockSpec((None, hid, d), lambda i, te, used: (te[i], 0, 0))],
        out_specs=pl.BlockSpec((tile_rows, LANES), lambda i, te, used: (i, 0)),
        scratch_shapes=[pltpu.VMEM((d, 2 * hid), BF16), pltpu.VMEM((hid, d), BF16)],
    )
    return pl.pallas_call(
        _ffn_body,
        grid_spec=grid_spec,
        out_shape=jax.ShapeDtypeStruct(xs.shape, F32),
        compiler_params=_params(("arbitrary",)),
        name="moe_experts",
    )(tile_expert, used, xs, g2.reshape(1, d), w_gate, w_up, w_down)


def _combine_body(pos_ref, ys_hbm, x1_ref, w_ref, gf_ref, o_ref, buf, sems, *, tm, n_tokens, final_norm):
    i = pl.program_id(0)
    slot = i % 2

    def gather(step, s):
        def issue(g, c):
            for k in range(2):
                base = k * n_tokens + step * tm + g * ROW_UNROLL
                for j in range(ROW_UNROLL):
                    src = _tokens(ys_hbm, pos_ref[base + j])
                    pltpu.make_async_copy(src, _tokens(buf.at[s, k], g * ROW_UNROLL + j),
                                          sems.at[s]).start(priority=j % 2)
            return c

        lax.fori_loop(0, tm // ROW_UNROLL, issue, 0)

    @pl.when(i == 0)
    def _():
        gather(i, slot)

    @pl.when(i + 1 < pl.num_programs(0))
    def _():
        gather(i + 1, 1 - slot)

    for k in range(2):
        pltpu.make_async_copy(_tokens(ys_hbm, 0, tm), buf.at[slot, k], sems.at[slot]).wait()
    w = w_ref[...]
    y = (x1_ref[...] + w[:, 0:1] * _load_token_major(buf.at[slot, 0], tm)
         + w[:, 1:2] * _load_token_major(buf.at[slot, 1], tm))
    if final_norm:
        y = _rms(y, gf_ref[...])
    o_ref[...] = y


def _combine(ys, x1, pos, w2, gf, final_norm, tm=256):
    t, d = x1.shape
    grid_spec = pltpu.PrefetchScalarGridSpec(
        num_scalar_prefetch=1,
        grid=(t // tm,),
        in_specs=[pl.BlockSpec(memory_space=pl.ANY),
                  pl.BlockSpec((tm, d), lambda i, *_: (i, 0)),
                  pl.BlockSpec((tm, 2), lambda i, *_: (i, 0)),
                  pl.BlockSpec((1, d), lambda i, *_: (0, 0))],
        out_specs=pl.BlockSpec((tm, d), lambda i, *_: (i, 0)),
        scratch_shapes=[pltpu.VMEM((2, 2, tm * SUBLANES, LANES), F32), pltpu.SemaphoreType.DMA((2,))],
    )
    return pl.pallas_call(
        functools.partial(_combine_body, tm=tm, n_tokens=t, final_norm=final_norm),
        grid_spec=grid_spec,
        out_shape=jax.ShapeDtypeStruct((t, d), F32),
        compiler_params=_params(("arbitrary",)),
        name="moe_combine",
    )(pos, ys, x1, w2, gf.reshape(1, d))


def _layer(x, batch, seq, norm1, w_in, a_ln_g, a_ln_b, a_ws, a_bs, ret_gn, w_out, norm2,
           router_g, router_gb, router_e, router_eb, w_gate, w_up, w_down, final_gain, is_last):
    t, d = x.shape
    a_groups = a_ws.shape[0]
    aw = a_groups * HEAD_DIM
    c_heads = ret_gn.shape[0] // HEAD_DIM
    cw = c_heads * HEAD_DIM
    bw = w_in.shape[1] - 2 * aw - 4 * cw
    b_heads = (bw // 3) // HEAD_DIM
    za, zqkv, zg = _inproj(x, norm1, w_in, (2 * aw, bw + 3 * cw, cw), (F32, BF16, F32))
    zqkv3 = zqkv.reshape(batch, seq, zqkv.shape[1])
    a_out = _sgu(za, a_ln_g, a_ln_b, a_ws, a_bs)
    b_out = _attn(zqkv3, 0, b_heads).reshape(t, bw // 3)
    c_out = _retention(zqkv3, bw // cw, zg.reshape(batch, seq, cw), c_heads, ret_gn).reshape(t, cw)
    x1, route = _outproj(x, a_out, b_out, c_out, w_out, norm2,
                         router_g, router_gb, router_e, router_eb)
    eid = route[0:2].astype(I32)
    w2 = route[2:4].T
    n_tiles = (2 * t) // MOE_TILE + N_EXPERTS
    pos, tile_expert, used, pad_start, pad_count = _moe_plan(eid, n_tiles)
    xs = _dispatch(x1, pos, n_tiles, pad_start, pad_count, used)
    ys = _ffn(xs, norm2, w_gate, w_up, w_down, tile_expert, used)
    return _combine(ys, x1, pos, w2, final_gain, is_last)


def kernel(x, norm1, w_in, a_ln_g, a_ln_b, a_ws, a_bs, ret_gn, w_out, norm2, router_g, router_gb, router_e,
           router_eb, w_gate, w_up, w_down, final_norm):
    batch, seq, d = x.shape
    depth = norm1.shape[0]
    h = x.reshape(batch * seq, d)
    for l in range(depth):
        h = _layer(h, batch, seq, norm1[l], w_in[l], a_ln_g[l], a_ln_b[l], a_ws[l], a_bs[l], ret_gn[l],
                   w_out[l], norm2[l], router_g[l], router_gb[l], router_e[l], router_eb[l],
                   w_gate[l], w_up[l], w_down[l], final_norm, l == depth - 1)
    return h.reshape(batch, seq, d)
```

```python
import functools

import numpy as np
import jax
import jax.numpy as jnp
from jax import lax
from jax.experimental import pallas as pl
from jax.experimental.pallas import tpu as pltpu

F32 = jnp.float32
BF16 = jnp.bfloat16
I32 = jnp.int32

EPS = 1e-6
HEAD_DIM = 64
SGU_CHUNK = 128
RET_CHUNK = 128
ATTN_BLOCK = 128
DILATIONS = (1, 4, 16)
ATTN_UNROLL = 8
ROPE_BASE = 10000.0
N_GROUPS = 4
EXPERTS_PER_GROUP = 4
N_EXPERTS = N_GROUPS * EXPERTS_PER_GROUP
LANES = 128
SUBLANES = 8
MOE_TILE = 512
ROW_UNROLL = 8
VMEM_LIMIT = 56 * 1024 * 1024


def _params(sem, vmem=VMEM_LIMIT):
    return pltpu.CompilerParams(dimension_semantics=sem, vmem_limit_bytes=vmem)


def _rms(x, g):
    return x * lax.rsqrt(jnp.mean(x * x, axis=-1, keepdims=True) + EPS) * g


def _split_bf16(a):
    hi = a.astype(BF16)
    lo = (a - hi.astype(F32)).astype(BF16)
    return hi, lo


def _cast_rows(dst_ref, src_ref, rows=256):
    for r0 in range(0, src_ref.shape[0], rows):
        dst_ref[r0:r0 + rows, :] = src_ref[r0:r0 + rows, :].astype(dst_ref.dtype)


def _inproj_body(x_ref, g_ref, w_ref, *refs, n_chunk):
    out_refs, wb = refs[:-1], refs[-1]

    @pl.when(pl.program_id(0) == 0)
    def _():
        _cast_rows(wb, w_ref)

    hb = _rms(x_ref[...], g_ref[...]).astype(BF16)
    col = 0
    for o_ref in out_refs:
        for n0 in range(0, o_ref.shape[1], n_chunk):
            n1 = min(n0 + n_chunk, o_ref.shape[1])
            acc = jnp.dot(hb, wb[:, col + n0:col + n1], preferred_element_type=F32)
            o_ref[:, n0:n1] = acc.astype(o_ref.dtype)
        col += o_ref.shape[1]


def _layer_resident(stacked, layer):
    rest = stacked.shape[1:]
    return pl.BlockSpec((None,) + rest, lambda *_: (layer,) + (0,) * len(rest), pipeline_mode=pl.Buffered(1))


def _inproj(x, g, w_stack, layer, widths, dtypes, tm=512, n_chunk=512):
    t, d = x.shape
    assert sum(widths) == w_stack.shape[2]
    return pl.pallas_call(
        functools.partial(_inproj_body, n_chunk=n_chunk),
        grid=(t // tm,),
        in_specs=[pl.BlockSpec((tm, d), lambda i: (i, 0)),
                  pl.BlockSpec((1, d), lambda i: (0, 0)),
                  _layer_resident(w_stack, layer)],
        out_specs=[pl.BlockSpec((tm, n), lambda i: (i, 0)) for n in widths],
        out_shape=[jax.ShapeDtypeStruct((t, n), dt) for n, dt in zip(widths, dtypes)],
        scratch_shapes=[pltpu.VMEM(w_stack.shape[1:], BF16)],
        compiler_params=_params(("arbitrary",)),
        name="inproj",
    )(x, g.reshape(1, d), w_stack)


def _gelu_tanh(x):
    return 0.5 * x * (1.0 + jnp.tanh(np.sqrt(2.0 / np.pi).astype(np.float32) * (x + 0.044715 * (x * x * x))))


def _sgu_body(za_ref, lng_ref, lnb_ref, wcat_ref, bias_ref, o_ref):
    ga = _gelu_tanh(za_ref[...])
    aw = ga.shape[1] // 2
    u = ga[:, :aw]
    v = ga[:, aw:]
    mu = jnp.mean(v, axis=-1, keepdims=True)
    dv = v - mu
    var = jnp.mean(dv * dv, axis=-1, keepdims=True)
    vn = dv * lax.rsqrt(var + EPS) * lng_ref[...] + lnb_ref[...]
    group = lax.broadcasted_iota(I32, (1, aw), 1) // HEAD_DIM
    n_groups = aw // HEAD_DIM
    for c in range(ga.shape[0] // SGU_CHUNK):
        rows = slice(c * SGU_CHUNK, (c + 1) * SGU_CHUNK)
        vc = vn[rows]
        stack = jnp.concatenate([jnp.where(group == g, vc, 0.0) for g in range(n_groups)], axis=0)
        s = jnp.dot(wcat_ref[...], stack.astype(BF16), preferred_element_type=F32) + bias_ref[...]
        o_ref[rows, :] = (u[rows] * s).astype(o_ref.dtype)


def _sgu(z, ln_g, ln_b, w_s, b_s, tm=1024):
    t = z.shape[0]
    n_groups = w_s.shape[0]
    aw = n_groups * HEAD_DIM
    causal = jnp.tril(jnp.ones((SGU_CHUNK, SGU_CHUNK), dtype=bool))
    wcat = jnp.where(causal[None], w_s, 0.0).transpose(1, 0, 2).reshape(SGU_CHUNK, n_groups * SGU_CHUNK)
    bias = jnp.repeat(b_s.T, HEAD_DIM, axis=1)
    return pl.pallas_call(
        _sgu_body,
        grid=(t // tm,),
        in_specs=[pl.BlockSpec((tm, 2 * aw), lambda i: (i, 0)),
                  pl.BlockSpec((1, aw), lambda i: (0, 0)),
                  pl.BlockSpec((1, aw), lambda i: (0, 0)),
                  pl.BlockSpec(wcat.shape, lambda i: (0, 0)),
                  pl.BlockSpec(bias.shape, lambda i: (0, 0))],
        out_specs=pl.BlockSpec((tm, aw), lambda i: (i, 0)),
        out_shape=jax.ShapeDtypeStruct((t, aw), BF16),
        compiler_params=_params(("parallel",)),
        name="sgu",
    )(z, ln_g.reshape(1, aw), ln_b.reshape(1, aw), wcat.astype(BF16), bias)


def _attn_bias():
    blk = ATTN_BLOCK
    qi = np.arange(2 * blk)[:, None] % blk
    ci = np.arange(2 * blk)[None, :]
    first = (ci < blk) & (ci <= qi)
    later = np.where(ci < blk, ci >= qi, ci - blk <= qi)
    return np.where(np.stack([first, later]), 0.0, -np.inf).astype(np.float32)


def _attn_body(q_ref, k_ref, v_ref, bias_ref, o_ref, qkv_scr, ob_scr, lse_scr, *, seq, unroll):
    blk = ATTN_BLOCK
    head0 = lax.broadcasted_iota(I32, (1, LANES), 1) < HEAD_DIM
    scale = HEAD_DIM ** -0.5 * np.log2(np.e)

    widen_rows = 512

    def widen(i, carry):
        sl = pl.ds(pl.multiple_of(i * widen_rows, widen_rows), widen_rows)
        qkv_scr[0, sl, :] = q_ref[0, sl, :].astype(F32) * scale
        qkv_scr[1, sl, :] = k_ref[0, sl, :].astype(F32)
        qkv_scr[2, sl, :] = v_ref[0, sl, :].astype(F32)
        return carry

    lax.fori_loop(0, seq // widen_rows, widen, 0)
    q2, k2, v2 = qkv_scr.at[0], qkv_scr.at[1], qkv_scr.at[2]

    def rows(start, n, d):
        return pl.ds(start, n) if d == 1 else pl.ds(start, n, stride=d)

    def block(p, d, r, n):
        base = n * (blk * d) + r
        kstart = jnp.maximum(base - blk * d, r)
        if d == 1:
            base, kstart = pl.multiple_of(base, blk), pl.multiple_of(kstart, blk)
        qb = q2[rows(base, blk, d), :]
        qs = jnp.concatenate([jnp.where(head0, qb, 0.0), jnp.where(head0, 0.0, qb)], axis=0).astype(BF16)
        kb = k2[rows(kstart, 2 * blk, d), :].astype(BF16)
        vb = v2[rows(kstart, 2 * blk, d), :].astype(BF16)
        va = jnp.concatenate([vb, jnp.ones_like(vb)], axis=1)
        s = lax.dot_general(qs, kb, (((1,), (1,)), ((), ())), preferred_element_type=F32)
        s = s + bias_ref[jnp.minimum(n, 1)]
        m = jnp.max(s, axis=-1, keepdims=True)
        e = jnp.exp2(s - m)
        oa = jnp.dot(e.astype(BF16), va, preferred_element_type=F32)
        l = oa[:, LANES:]
        o = oa[:, :LANES] / l
        lse = m + jnp.log2(l)
        ob_scr[p, rows(base, blk, d), :] = jnp.where(head0, o[:blk], o[blk:])
        lse_scr[p, rows(base, blk, d), :] = jnp.where(head0, lse[:blk], lse[blk:])

    for p, d in enumerate(DILATIONS):
        def group(i, carry, p=p, d=d):
            for u in range(unroll):
                b = i * unroll + u
                block(p, d, b & (d - 1), b >> (d.bit_length() - 1))
            return carry

        lax.fori_loop(0, seq // (blk * unroll), group, 0)

    step = 256

    def mix(i, carry):
        sl = pl.ds(pl.multiple_of(i * step, step), step)
        ls = [lse_scr[p, sl, :] for p in range(len(DILATIONS))]
        m = functools.reduce(jnp.maximum, ls)
        es = [jnp.exp2(l - m) for l in ls]
        num = sum(e * ob_scr[p, sl, :] for p, e in enumerate(es))
        o_ref[0, sl, :] = (num / sum(es)).astype(o_ref.dtype)
        return carry

    lax.fori_loop(0, seq // step, mix, 0)


def _attn(z3, q_col, n_heads):
    b, s, _ = z3.shape
    n_pairs = n_heads * HEAD_DIM // LANES
    assert s % (ATTN_BLOCK * ATTN_UNROLL) == 0 and s % (2 * ATTN_BLOCK * max(DILATIONS)) == 0
    assert all(d & (d - 1) == 0 for d in DILATIONS)
    bias = jnp.asarray(_attn_bias())

    def spec(off):
        return pl.BlockSpec((1, s, LANES), lambda i, j: (i, 0, off + j))

    return pl.pallas_call(
        functools.partial(_attn_body, seq=s, unroll=ATTN_UNROLL),
        grid=(b, n_pairs),
        in_specs=[spec(q_col), spec(q_col + n_pairs), spec(q_col + 2 * n_pairs),
                  pl.BlockSpec(bias.shape, lambda i, j: (0, 0, 0))],
        out_specs=pl.BlockSpec((1, s, LANES), lambda i, j: (i, 0, j)),
        out_shape=jax.ShapeDtypeStruct((b, s, n_pairs * LANES), BF16),
        scratch_shapes=[pltpu.VMEM((3, s, LANES), F32),
                        pltpu.VMEM((len(DILATIONS), s, LANES), F32),
                        pltpu.VMEM((len(DILATIONS), s, LANES), F32)],
        compiler_params=_params(("parallel", "parallel")),
        name="dilated_attn",
    )(z3, z3, z3, bias)


def _ret_body(q_ref, k_ref, v_ref, g_ref, cos_ref, sa_ref, sb_ref, dec_ref, qdec_ref, kdec_ref, cdec_ref,
              avg_ref, gn_ref, o_ref, state):
    cw = q_ref.shape[2]
    n_heads = cw // HEAD_DIM
    head = lax.broadcasted_iota(I32, (1, cw), 1) // HEAD_DIM
    hr = lax.broadcasted_iota(I32, (cw, cw), 0) // HEAD_DIM
    hc = lax.broadcasted_iota(I32, (cw, cw), 1) // HEAD_DIM
    same_head = hr == hc

    @pl.when(pl.program_id(1) == 0)
    def _():
        state[...] = jnp.zeros_like(state)

    def rotary(x, cos, sa, sb):
        half = HEAD_DIM // 2
        return x * cos + pltpu.roll(x, half, 1) * sa + pltpu.roll(x, cw - half, 1) * sb

    def head_mean(a):
        hi, lo = _split_bf16(a)
        return (jnp.dot(hi, avg_ref[...], preferred_element_type=F32)
                + jnp.dot(lo, avg_ref[...], preferred_element_type=F32))

    for c in range(q_ref.shape[1] // RET_CHUNK):
        rows = slice(c * RET_CHUNK, (c + 1) * RET_CHUNK)
        cos, sa, sb = cos_ref[rows, :], sa_ref[rows, :], sb_ref[rows, :]
        qr = rotary(q_ref[0, rows, :].astype(F32), cos, sa, sb)
        kr = rotary(k_ref[0, rows, :].astype(F32), cos, sa, sb) * (HEAD_DIM ** -0.5)
        v = v_ref[0, rows, :]
        krb = kr.astype(BF16)
        inner = [lax.dot_general(jnp.where(head == h, qr, 0.0).astype(BF16), krb, (((1,), (1,)), ((), ())),
                                 preferred_element_type=F32) for h in range(n_heads)]
        inner = jnp.concatenate(inner, axis=1) * dec_ref[...]
        vstack = jnp.concatenate([jnp.where(head == h, v, 0.0) for h in range(n_heads)], axis=0)
        st = state[...]
        o = (jnp.dot(inner.astype(BF16), vstack.astype(BF16), preferred_element_type=F32)
             + jnp.dot((qr * qdec_ref[...]).astype(BF16), st.astype(BF16), preferred_element_type=F32))
        ktv = lax.dot_general((kr * kdec_ref[...]).astype(BF16), v.astype(BF16), (((0,), (0,)), ((), ())),
                              preferred_element_type=F32)
        state[...] = st * cdec_ref[...] + jnp.where(same_head, ktv, 0.0)
        mu = head_mean(o)
        dd = o - mu
        var = head_mean(dd * dd)
        on = dd * lax.rsqrt(var + EPS) * gn_ref[...]
        gate = g_ref[0, rows, :]
        o_ref[0, rows, :] = (gate * jax.nn.sigmoid(gate) * on).astype(o_ref.dtype)


def _ret_tables(seq, n_heads):
    half = HEAD_DIM // 2
    inv = ROPE_BASE ** (-jnp.arange(half, dtype=F32) / half)
    ang = jnp.arange(seq, dtype=F32)[:, None] * inv[None]
    cos, sin = jnp.cos(ang), jnp.sin(ang)
    zero = jnp.zeros_like(sin)
    tile = lambda a: jnp.tile(a, (1, n_heads))
    cos_t = tile(jnp.concatenate([cos, cos], axis=1))
    sa_t = tile(jnp.concatenate([zero, sin], axis=1))
    sb_t = tile(jnp.concatenate([-sin, zero], axis=1))
    log_g = jnp.log(1.0 - 2.0 ** (-5.0 - jnp.arange(n_heads, dtype=F32)))
    c = RET_CHUNK
    idx = jnp.arange(c)
    diff = idx[:, None] - idx[None, :]
    decay_in = jnp.where(diff >= 0, jnp.exp(log_g[:, None, None] * jnp.maximum(diff, 0)[None]), 0.0)
    dec = decay_in.transpose(1, 0, 2).reshape(c, n_heads * c)
    lane_head = jnp.repeat(jnp.arange(n_heads), HEAD_DIM)
    qdec = jnp.exp(log_g[lane_head][None, :] * (idx + 1)[:, None].astype(F32))
    kdec = jnp.exp(log_g[lane_head][None, :] * (c - 1 - idx)[:, None].astype(F32))
    same = lane_head[:, None] == lane_head[None, :]
    cdec = jnp.where(same, jnp.exp(log_g * c)[lane_head][:, None], 0.0)
    avg = jnp.where(same, 1.0 / HEAD_DIM, 0.0).astype(BF16)
    return cos_t, sa_t, sb_t, dec, qdec, kdec, cdec, avg


def _retention(z3, col, gate3, n_heads, gn, tc=1024):
    b, s, _ = z3.shape
    cw = n_heads * HEAD_DIM
    cos_t, sa_t, sb_t, dec, qdec, kdec, cdec, avg = _ret_tables(s, n_heads)

    def zspec(off):
        return pl.BlockSpec((1, tc, cw), lambda i, j: (i, j, off))

    tab = pl.BlockSpec((tc, cw), lambda i, j: (j, 0))
    full = lambda a: pl.BlockSpec(a.shape, lambda i, j: (0, 0))
    gn2 = gn.reshape(1, cw)
    return pl.pallas_call(
        _ret_body,
        grid=(b, s // tc),
        in_specs=[zspec(col), zspec(col + 1), zspec(col + 2), zspec(0), tab, tab, tab,
                  full(dec), full(qdec), full(kdec), full(cdec), full(avg), full(gn2)],
        out_specs=pl.BlockSpec((1, tc, cw), lambda i, j: (i, j, 0)),
        out_shape=jax.ShapeDtypeStruct((b, s, cw), BF16),
        scratch_shapes=[pltpu.VMEM((cw, cw), F32)],
        compiler_params=_params(("parallel", "arbitrary")),
        name="retention",
    )(z3, z3, z3, gate3, cos_t, sa_t, sb_t, dec, qdec, kdec, cdec, avg, gn2)


def _outproj_body(x_ref, a_ref, b_ref, c_ref, wo_ref, g2_ref, rhi_ref, rlo_ref, rb_ref, x1_ref, route_ref, wb):
    @pl.when(pl.program_id(0) == 0)
    def _():
        _cast_rows(wb, wo_ref)

    aw, bw = a_ref.shape[1], b_ref.shape[1]
    y = (jnp.dot(a_ref[...], wb[0:aw, :], preferred_element_type=F32)
         + jnp.dot(b_ref[...], wb[aw:aw + bw, :], preferred_element_type=F32)
         + jnp.dot(c_ref[...], wb[aw + bw:, :], preferred_element_type=F32))
    x1 = x_ref[...] + y
    x1_ref[...] = x1
    hi, lo = _split_bf16(_rms(x1, g2_ref[...]))
    nt = lambda r, h: lax.dot_general(r, h, (((1,), (1,)), ((), ())), preferred_element_type=F32)
    lt = nt(rhi_ref[...], hi) + nt(rhi_ref[...], lo) + nt(rlo_ref[...], hi) + rb_ref[...]
    row = lambda i: lt[i:i + 1, :]
    best, gi = row(0), jnp.zeros_like(row(0), dtype=I32)
    for i in range(1, N_GROUPS):
        up = row(i) > best
        best = jnp.where(up, row(i), best)
        gi = jnp.where(up, i, gi)
    g_w = 1.0 / sum(jnp.exp(row(i) - best) for i in range(N_GROUPS))
    el = []
    for j in range(EXPERTS_PER_GROUP):
        e = row(N_GROUPS + (N_GROUPS - 1) * EXPERTS_PER_GROUP + j)
        for g in range(N_GROUPS - 2, -1, -1):
            e = jnp.where(gi == g, row(N_GROUPS + g * EXPERTS_PER_GROUP + j), e)
        el.append(e)
    v1, i1 = el[0], jnp.zeros_like(gi)
    for j in range(1, EXPERTS_PER_GROUP):
        up = el[j] > v1
        v1 = jnp.where(up, el[j], v1)
        i1 = jnp.where(up, j, i1)
    v2, i2 = jnp.full_like(v1, -jnp.inf), jnp.zeros_like(gi)
    for j in range(EXPERTS_PER_GROUP):
        up = jnp.logical_and(i1 != j, el[j] > v2)
        v2 = jnp.where(up, el[j], v2)
        i2 = jnp.where(up, j, i2)
    e21 = jnp.exp(v2 - v1)
    w1 = g_w / (1.0 + e21)
    w2 = g_w * e21 / (1.0 + e21)
    e1 = (gi * EXPERTS_PER_GROUP + i1).astype(F32)
    e2 = (gi * EXPERTS_PER_GROUP + i2).astype(F32)
    zero = jnp.zeros_like(w1)
    route_ref[...] = jnp.concatenate([e1, e2, w1, w2, zero, zero, zero, zero], axis=0)


def _outproj(x, a, b, c, wo_stack, layer, g2, router_g, router_gb, router_e, router_eb, tm=512):
    t, d = x.shape
    n_logits = N_GROUPS + N_EXPERTS
    n_rows = -(-n_logits // 16) * 16
    r = jnp.pad(jnp.concatenate([router_g, router_e], axis=1).T, ((0, n_rows - n_logits), (0, 0)))
    rhi = r.astype(BF16)
    rlo = (r - rhi.astype(F32)).astype(BF16)
    rb = jnp.pad(jnp.concatenate([router_gb, router_eb]), (0, n_rows - n_logits)).reshape(n_rows, 1)
    row_spec = lambda w: pl.BlockSpec((tm, w), lambda i: (i, 0))
    full = lambda arr: pl.BlockSpec(arr.shape, lambda i: (0, 0))
    g2r = g2.reshape(1, d)
    return pl.pallas_call(
        _outproj_body,
        grid=(t // tm,),
        in_specs=[row_spec(d), row_spec(a.shape[1]), row_spec(b.shape[1]), row_spec(c.shape[1]),
                  _layer_resident(wo_stack, layer), full(g2r), full(rhi), full(rlo), full(rb)],
        out_specs=[row_spec(d), pl.BlockSpec((8, tm), lambda i: (0, i))],
        out_shape=[jax.ShapeDtypeStruct((t, d), F32), jax.ShapeDtypeStruct((8, t), F32)],
        scratch_shapes=[pltpu.VMEM(wo_stack.shape[1:], BF16)],
        compiler_params=_params(("arbitrary",)),
        name="outproj_route",
    )(x, a, b, c, wo_stack, g2r, rhi, rlo, rb)


def _moe_plan(eid, n_tiles):
    flat = eid.reshape(-1)
    onehot = (flat[:, None] == jnp.arange(N_EXPERTS, dtype=I32)[None, :]).astype(I32)
    csum = jnp.cumsum(onehot, axis=0)
    counts = csum[-1]
    padded = ((counts + MOE_TILE - 1) // MOE_TILE) * MOE_TILE
    ends = jnp.cumsum(padded)
    starts = ends - padded
    pos = jnp.sum(onehot * (csum - 1 + starts[None, :]), axis=1)
    tile_start = jnp.arange(n_tiles, dtype=I32) * MOE_TILE
    tile_expert = jnp.sum((ends[None, :] <= tile_start[:, None]).astype(I32), axis=1)
    tile_expert = jnp.minimum(tile_expert, N_EXPERTS - 1).astype(I32)
    used = (ends[-1] // MOE_TILE).astype(I32).reshape(1)
    pad_start = (starts + counts).astype(I32)
    pad_count = (padded - counts).astype(I32)
    return pos.astype(I32), tile_expert, used, pad_start, pad_count


def _store_token_major(ref, x, tok0=0):
    n = x.shape[0]
    for c in range(SUBLANES):
        ref[pl.ds(tok0 * SUBLANES + c, n, stride=SUBLANES), :] = x[:, c * LANES:(c + 1) * LANES]


def _load_token_major(ref, n, tok0=0):
    return jnp.concatenate([ref[pl.ds(tok0 * SUBLANES + c, n, stride=SUBLANES), :] for c in range(SUBLANES)],
                           axis=1)


def _tokens(ref, tok, n=1):
    return ref.at[pl.ds(pl.multiple_of(tok * SUBLANES, SUBLANES), n * SUBLANES)]


def _dispatch_body(pad_start_ref, pad_count_ref, used_ref, pos_ref, x_ref, xs_hbm, xt, zeros, sem, fill_sem,
                   *, tm, n_tokens, n_tiles):
    i = pl.program_id(0)
    half_tile = MOE_TILE // 2

    def fill(act):
        def expert_pad(e, c):
            off, n = pad_start_ref[e], pad_count_ref[e]
            bit = half_tile
            while bit:
                take = (n & bit) != 0

                @pl.when(take)
                def _(off=off, bit=bit):
                    act(pltpu.make_async_copy(_tokens(zeros, 0, bit), _tokens(xs_hbm, off, bit), fill_sem))

                off = off + jnp.where(take, bit, 0)
                bit //= 2
            return c

        lax.fori_loop(0, N_EXPERTS, expert_pad, 0)

        def unused_half_tile(j, c):
            act(pltpu.make_async_copy(zeros, _tokens(xs_hbm, j * half_tile, half_tile), fill_sem))
            return c

        lax.fori_loop(2 * used_ref[0], 2 * n_tiles, unused_half_tile, 0)

    @pl.when(i == 0)
    def _():
        zeros[...] = jnp.zeros_like(zeros)
        fill(lambda cp: cp.start())
        fill(lambda cp: cp.wait())

    _store_token_major(xt, x_ref[...])

    def issue(g, c):
        for k in range(2):
            base = k * n_tokens + i * tm + g * ROW_UNROLL
            for j in range(ROW_UNROLL):
                dst = _tokens(xs_hbm, pos_ref[base + j])
                pltpu.make_async_copy(_tokens(xt, g * ROW_UNROLL + j), dst, sem).start(priority=j % 2)
        return c

    lax.fori_loop(0, tm // ROW_UNROLL, issue, 0)
    for k in range(2):
        pltpu.make_async_copy(xt, _tokens(xs_hbm, 0, tm), sem).wait()


def _dispatch(x1, pos, n_tiles, pad_start, pad_count, used, tm=512):
    t, d = x1.shape
    assert d == SUBLANES * LANES
    grid_spec = pltpu.PrefetchScalarGridSpec(
        num_scalar_prefetch=4,
        grid=(t // tm,),
        in_specs=[pl.BlockSpec((tm, d), lambda i, *_: (i, 0))],
        out_specs=pl.BlockSpec(memory_space=pl.ANY),
        scratch_shapes=[pltpu.VMEM((tm * SUBLANES, LANES), F32),
                        pltpu.VMEM((MOE_TILE // 2 * SUBLANES, LANES), F32),
                        pltpu.SemaphoreType.DMA, pltpu.SemaphoreType.DMA],
    )
    return pl.pallas_call(
        functools.partial(_dispatch_body, tm=tm, n_tokens=t, n_tiles=n_tiles),
        grid_spec=grid_spec,
        out_shape=jax.ShapeDtypeStruct((n_tiles * MOE_TILE * SUBLANES, LANES), F32),
        compiler_params=_params(("arbitrary",)),
        name="moe_dispatch",
    )(pad_start, pad_count, used, pos, x1)


def _ffn_body(te_ref, used_ref, xs_ref, g2_ref, wg_ref, wu_ref, wd_ref, ys_ref, wgu_b, wd_b):
    i = pl.program_id(0)
    live = i < used_ref[0]
    hid = wd_ref.shape[0]

    @pl.when(jnp.logical_and(live, jnp.logical_or(i == 0, te_ref[i] != te_ref[jnp.maximum(i - 1, 0)])))
    def _():
        _cast_rows(wgu_b.at[:, 0:hid], wg_ref)
        _cast_rows(wgu_b.at[:, hid:2 * hid], wu_ref)
        _cast_rows(wd_b, wd_ref)

    @pl.when(live)
    def _():
        half = MOE_TILE // 2
        for tok0 in (0, half):
            hb = _rms(_load_token_major(xs_ref, half, tok0), g2_ref[...]).astype(BF16)
            gu = jnp.dot(hb, wgu_b[...], preferred_element_type=F32)
            gate, up = gu[:, :hid], gu[:, hid:]
            act = (gate * jax.nn.sigmoid(gate) * up).astype(BF16)
            _store_token_major(ys_ref, jnp.dot(act, wd_b[...], preferred_element_type=F32), tok0)

    @pl.when(jnp.logical_not(live))
    def _():
        ys_ref[...] = jnp.zeros_like(ys_ref)


def _ffn(xs, g2, w_gate, w_up, w_down, layer, tile_expert, used):
    n_tiles = xs.shape[0] // (MOE_TILE * SUBLANES)
    d, hid = w_down.shape[3], w_down.shape[2]
    tile_rows = MOE_TILE * SUBLANES
    live = lambda i, te, used: jnp.minimum(i, used[0] - 1)
    grid_spec = pltpu.PrefetchScalarGridSpec(
        num_scalar_prefetch=2,
        grid=(n_tiles,),
        in_specs=[pl.BlockSpec((tile_rows, LANES), lambda i, te, used: (live(i, te, used), 0)),
                  pl.BlockSpec((1, d), lambda i, te, used: (0, 0)),
                  pl.BlockSpec((None, None, d, hid), lambda i, te, used: (layer, te[i], 0, 0)),
                  pl.BlockSpec((None, None, d, hid), lambda i, te, used: (layer, te[i], 0, 0)),
                  pl.BlockSpec((None, None, hid, d), lambda i, te, used: (layer, te[i], 0, 0))],
        out_specs=pl.BlockSpec((tile_rows, LANES), lambda i, te, used: (i, 0)),
        scratch_shapes=[pltpu.VMEM((d, 2 * hid), BF16), pltpu.VMEM((hid, d), BF16)],
    )
    return pl.pallas_call(
        _ffn_body,
        grid_spec=grid_spec,
        out_shape=jax.ShapeDtypeStruct(xs.shape, F32),
        compiler_params=_params(("arbitrary",)),
        name="moe_experts",
    )(tile_expert, used, xs, g2.reshape(1, d), w_gate, w_up, w_down)


def _combine_body(pos_ref, ys_hbm, x1_ref, w_ref, gf_ref, o_ref, buf, sems, *, tm, n_tokens, final_norm):
    i = pl.program_id(0)
    slot = i % 2

    def gather(step, s):
        def issue(g, c):
            for k in range(2):
                base = k * n_tokens + step * tm + g * ROW_UNROLL
                for j in range(ROW_UNROLL):
                    src = _tokens(ys_hbm, pos_ref[base + j])
                    pltpu.make_async_copy(src, _tokens(buf.at[s, k], g * ROW_UNROLL + j),
                                          sems.at[s]).start(priority=j % 2)
            return c

        lax.fori_loop(0, tm // ROW_UNROLL, issue, 0)

    @pl.when(i == 0)
    def _():
        gather(i, slot)

    @pl.when(i + 1 < pl.num_programs(0))
    def _():
        gather(i + 1, 1 - slot)

    for k in range(2):
        pltpu.make_async_copy(_tokens(ys_hbm, 0, tm), buf.at[slot, k], sems.at[slot]).wait()
    w = w_ref[...]
    y = (x1_ref[...] + w[:, 0:1] * _load_token_major(buf.at[slot, 0], tm)
         + w[:, 1:2] * _load_token_major(buf.at[slot, 1], tm))
    if final_norm:
        y = _rms(y, gf_ref[...])
    o_ref[...] = y


def _combine(ys, x1, pos, w2, gf, final_norm, tm=256):
    t, d = x1.shape
    grid_spec = pltpu.PrefetchScalarGridSpec(
        num_scalar_prefetch=1,
        grid=(t // tm,),
        in_specs=[pl.BlockSpec(memory_space=pl.ANY),
                  pl.BlockSpec((tm, d), lambda i, *_: (i, 0)),
                  pl.BlockSpec((tm, 2), lambda i, *_: (i, 0)),
                  pl.BlockSpec((1, d), lambda i, *_: (0, 0))],
        out_specs=pl.BlockSpec((tm, d), lambda i, *_: (i, 0)),
        scratch_shapes=[pltpu.VMEM((2, 2, tm * SUBLANES, LANES), F32), pltpu.SemaphoreType.DMA((2,))],
    )
    return pl.pallas_call(
        functools.partial(_combine_body, tm=tm, n_tokens=t, final_norm=final_norm),
        grid_spec=grid_spec,
        out_shape=jax.ShapeDtypeStruct((t, d), F32),
        compiler_params=_params(("arbitrary",)),
        name="moe_combine",
    )(pos, ys, x1, w2, gf.reshape(1, d))


def _layer(x, batch, seq, layer, norm1, w_in, a_ln_g, a_ln_b, a_ws, a_bs, ret_gn, w_out, norm2,
           router_g, router_gb, router_e, router_eb, w_gate, w_up, w_down, final_gain, is_last):
    t, d = x.shape
    a_groups = a_ws.shape[0]
    aw = a_groups * HEAD_DIM
    c_heads = ret_gn.shape[0] // HEAD_DIM
    cw = c_heads * HEAD_DIM
    bw = w_in.shape[2] - 2 * aw - 4 * cw
    b_heads = (bw // 3) // HEAD_DIM
    za, zqkv, zg = _inproj(x, norm1, w_in, layer, (2 * aw, bw + 3 * cw, cw), (F32, BF16, F32))
    zqkv3 = zqkv.reshape(batch, seq, zqkv.shape[1])
    a_out = _sgu(za, a_ln_g, a_ln_b, a_ws, a_bs)
    b_out = _attn(zqkv3, 0, b_heads).reshape(t, bw // 3)
    c_out = _retention(zqkv3, bw // cw, zg.reshape(batch, seq, cw), c_heads, ret_gn).reshape(t, cw)
    x1, route = _outproj(x, a_out, b_out, c_out, w_out, layer, norm2,
                         router_g, router_gb, router_e, router_eb)
    eid = route[0:2].astype(I32)
    w2 = route[2:4].T
    n_tiles = (2 * t) // MOE_TILE + N_EXPERTS
    pos, tile_expert, used, pad_start, pad_count = _moe_plan(eid, n_tiles)
    xs = _dispatch(x1, pos, n_tiles, pad_start, pad_count, used)
    ys = _ffn(xs, norm2, w_gate, w_up, w_down, layer, tile_expert, used)
    return _combine(ys, x1, pos, w2, final_gain, is_last)


def kernel(x, norm1, w_in, a_ln_g, a_ln_b, a_ws, a_bs, ret_gn, w_out, norm2, router_g, router_gb, router_e,
           router_eb, w_gate, w_up, w_down, final_norm):
    batch, seq, d = x.shape
    depth = norm1.shape[0]
    h = x.reshape(batch * seq, d)
    for l in range(depth):
        h = _layer(h, batch, seq, l, norm1[l], w_in, a_ln_g[l], a_ln_b[l], a_ws[l], a_bs[l], ret_gn[l],
                   w_out, norm2[l], router_g[l], router_gb[l], router_e[l], router_eb[l],
                   w_gate, w_up, w_down, final_norm, l == depth - 1)
    return h.reshape(batch, seq, d)
```

```python
import functools

import numpy as np
import jax
import jax.numpy as jnp
from jax import lax
from jax.experimental import pallas as pl
from jax.experimental.pallas import tpu as pltpu

F32 = jnp.float32
BF16 = jnp.bfloat16
I32 = jnp.int32

EPS = 1e-6
HEAD_DIM = 64
SGU_CHUNK = 128
RET_CHUNK = 128
ATTN_BLOCK = 128
DILATIONS = (1, 4, 16)
ATTN_UNROLL = 8
ROPE_BASE = 10000.0
N_GROUPS = 4
EXPERTS_PER_GROUP = 4
N_EXPERTS = N_GROUPS * EXPERTS_PER_GROUP
LANES = 128
SUBLANES = 8
MOE_TILE = 512
ROW_UNROLL = 8
VMEM_LIMIT = 56 * 1024 * 1024


def _params(sem, vmem=VMEM_LIMIT):
    return pltpu.CompilerParams(dimension_semantics=sem, vmem_limit_bytes=vmem)


def _rms(x, g):
    return x * lax.rsqrt(jnp.mean(x * x, axis=-1, keepdims=True) + EPS) * g


def _split_bf16(a):
    hi = a.astype(BF16)
    lo = (a - hi.astype(F32)).astype(BF16)
    return hi, lo


def _cast_rows(dst_ref, src_ref, rows=256):
    for r0 in range(0, src_ref.shape[0], rows):
        dst_ref[r0:r0 + rows, :] = src_ref[r0:r0 + rows, :].astype(dst_ref.dtype)


def _inproj_body(x_ref, g_ref, w_ref, *refs, n_chunk):
    out_refs, wb = refs[:-1], refs[-1]

    @pl.when(pl.program_id(0) == 0)
    def _():
        _cast_rows(wb, w_ref)

    hb = _rms(x_ref[...], g_ref[...]).astype(BF16)
    col = 0
    for o_ref in out_refs:
        for n0 in range(0, o_ref.shape[1], n_chunk):
            n1 = min(n0 + n_chunk, o_ref.shape[1])
            acc = jnp.dot(hb, wb[:, col + n0:col + n1], preferred_element_type=F32)
            o_ref[:, n0:n1] = acc.astype(o_ref.dtype)
        col += o_ref.shape[1]


def _layer_resident(stacked, layer):
    rest = stacked.shape[1:]
    return pl.BlockSpec((None,) + rest, lambda *_: (layer,) + (0,) * len(rest), pipeline_mode=pl.Buffered(1))


def _inproj(x, g, w_stack, layer, widths, dtypes, tm=1024, n_chunk=512):
    t, d = x.shape
    assert sum(widths) == w_stack.shape[2]
    return pl.pallas_call(
        functools.partial(_inproj_body, n_chunk=n_chunk),
        grid=(t // tm,),
        in_specs=[pl.BlockSpec((tm, d), lambda i: (i, 0)),
                  pl.BlockSpec((1, d), lambda i: (0, 0)),
                  _layer_resident(w_stack, layer)],
        out_specs=[pl.BlockSpec((tm, n), lambda i: (i, 0)) for n in widths],
        out_shape=[jax.ShapeDtypeStruct((t, n), dt) for n, dt in zip(widths, dtypes)],
        scratch_shapes=[pltpu.VMEM(w_stack.shape[1:], BF16)],
        compiler_params=_params(("arbitrary",)),
        name="inproj",
    )(x, g.reshape(1, d), w_stack)


def _gelu_tanh(x):
    return 0.5 * x * (1.0 + jnp.tanh(np.sqrt(2.0 / np.pi).astype(np.float32) * (x + 0.044715 * (x * x * x))))


def _sgu_body(za_ref, lng_ref, lnb_ref, wcat_ref, bias_ref, o_ref):
    ga = _gelu_tanh(za_ref[...])
    aw = ga.shape[1] // 2
    u = ga[:, :aw]
    v = ga[:, aw:]
    mu = jnp.mean(v, axis=-1, keepdims=True)
    dv = v - mu
    var = jnp.mean(dv * dv, axis=-1, keepdims=True)
    vn = dv * lax.rsqrt(var + EPS) * lng_ref[...] + lnb_ref[...]
    group = lax.broadcasted_iota(I32, (1, aw), 1) // HEAD_DIM
    n_groups = aw // HEAD_DIM
    for c in range(ga.shape[0] // SGU_CHUNK):
        rows = slice(c * SGU_CHUNK, (c + 1) * SGU_CHUNK)
        vc = vn[rows]
        stack = jnp.concatenate([jnp.where(group == g, vc, 0.0) for g in range(n_groups)], axis=0)
        s = jnp.dot(wcat_ref[...], stack.astype(BF16), preferred_element_type=F32) + bias_ref[...]
        o_ref[rows, :] = (u[rows] * s).astype(o_ref.dtype)


def _sgu(z, ln_g, ln_b, w_s, b_s, tm=1024):
    t = z.shape[0]
    n_groups = w_s.shape[0]
    aw = n_groups * HEAD_DIM
    causal = jnp.tril(jnp.ones((SGU_CHUNK, SGU_CHUNK), dtype=bool))
    wcat = jnp.where(causal[None], w_s, 0.0).transpose(1, 0, 2).reshape(SGU_CHUNK, n_groups * SGU_CHUNK)
    bias = jnp.repeat(b_s.T, HEAD_DIM, axis=1)
    return pl.pallas_call(
        _sgu_body,
        grid=(t // tm,),
        in_specs=[pl.BlockSpec((tm, 2 * aw), lambda i: (i, 0)),
                  pl.BlockSpec((1, aw), lambda i: (0, 0)),
                  pl.BlockSpec((1, aw), lambda i: (0, 0)),
                  pl.BlockSpec(wcat.shape, lambda i: (0, 0)),
                  pl.BlockSpec(bias.shape, lambda i: (0, 0))],
        out_specs=pl.BlockSpec((tm, aw), lambda i: (i, 0)),
        out_shape=jax.ShapeDtypeStruct((t, aw), BF16),
        compiler_params=_params(("parallel",)),
        name="sgu",
    )(z, ln_g.reshape(1, aw), ln_b.reshape(1, aw), wcat.astype(BF16), bias)


def _attn_bias():
    blk = ATTN_BLOCK
    qi = np.arange(2 * blk)[:, None] % blk
    ci = np.arange(2 * blk)[None, :]
    first = (ci < blk) & (ci <= qi)
    later = np.where(ci < blk, ci >= qi, ci - blk <= qi)
    return np.where(np.stack([first, later]), 0.0, -np.inf).astype(np.float32)


def _attn_body(q_ref, k_ref, v_ref, bias_ref, o_ref, qkv_scr, ob_scr, lse_scr, *, seq, unroll):
    blk = ATTN_BLOCK
    head0 = lax.broadcasted_iota(I32, (1, LANES), 1) < HEAD_DIM
    scale = HEAD_DIM ** -0.5 * np.log2(np.e)

    widen_rows = 512

    def widen(i, carry):
        sl = pl.ds(pl.multiple_of(i * widen_rows, widen_rows), widen_rows)
        qkv_scr[0, 0, sl, :] = q_ref[0, sl, :].astype(F32) * scale
        qkv_scr[0, 1, sl, :] = k_ref[0, sl, :].astype(F32)
        qkv_scr[0, 2, sl, :] = v_ref[0, sl, :].astype(F32)
        return carry

    lax.fori_loop(0, seq // widen_rows, widen, 0)

    src = {DILATIONS[0]: qkv_scr.at[0]}
    for stage, d in enumerate(DILATIONS[1:], start=1):
        prev_d, ratio, slab = DILATIONS[stage - 1], d // DILATIONS[stage - 1], seq // d
        prev, cur = qkv_scr.at[stage - 1], qkv_scr.at[stage]
        for r_prev in range(prev_d):
            for c in range(ratio):
                r = r_prev + prev_d * c
                for a in range(3):
                    cur[a, r * slab:(r + 1) * slab, :] = prev[a, pl.ds(r_prev * (seq // prev_d) + c, slab,
                                                                     stride=ratio), :]
        src[d] = cur

    def rows(start, n, d):
        return pl.ds(start, n) if d == 1 else pl.ds(start, n, stride=d)

    def block(p, d, r, n):
        base = n * (blk * d) + r
        first = r * (seq // d) + n * blk
        kfirst = r * (seq // d) + jnp.maximum(n - 1, 0) * blk
        first, kfirst = pl.multiple_of(first, blk), pl.multiple_of(kfirst, blk)
        qb = src[d][0, pl.ds(first, blk), :]
        qs = jnp.concatenate([jnp.where(head0, qb, 0.0), jnp.where(head0, 0.0, qb)], axis=0).astype(BF16)
        kb = src[d][1, pl.ds(kfirst, 2 * blk), :].astype(BF16)
        vb = src[d][2, pl.ds(kfirst, 2 * blk), :].astype(BF16)
        va = jnp.concatenate([vb, jnp.ones_like(vb)], axis=1)
        s = lax.dot_general(qs, kb, (((1,), (1,)), ((), ())), preferred_element_type=F32)
        s = s + bias_ref[jnp.minimum(n, 1)]
        m = jnp.max(s, axis=-1, keepdims=True)
        e = jnp.exp2(s - m)
        oa = jnp.dot(e.astype(BF16), va, preferred_element_type=F32)
        l = oa[:, LANES:]
        o = oa[:, :LANES] / l
        lse = m + jnp.log2(l)
        ob_scr[p, rows(base, blk, d), :] = jnp.where(head0, o[:blk], o[blk:])
        lse_scr[p, rows(base, blk, d), :] = jnp.where(head0, lse[:blk], lse[blk:])

    for p, d in enumerate(DILATIONS):
        def group(i, carry, p=p, d=d):
            for u in range(unroll):
                b = i * unroll + u
                block(p, d, b & (d - 1), b >> (d.bit_length() - 1))
            return carry

        lax.fori_loop(0, seq // (blk * unroll), group, 0)

    step = 256

    def mix(i, carry):
        sl = pl.ds(pl.multiple_of(i * step, step), step)
        ls = [lse_scr[p, sl, :] for p in range(len(DILATIONS))]
        m = functools.reduce(jnp.maximum, ls)
        es = [jnp.exp2(l - m) for l in ls]
        num = sum(e * ob_scr[p, sl, :] for p, e in enumerate(es))
        o_ref[0, sl, :] = (num / sum(es)).astype(o_ref.dtype)
        return carry

    lax.fori_loop(0, seq // step, mix, 0)


def _attn(z3, q_col, n_heads):
    b, s, _ = z3.shape
    n_pairs = n_heads * HEAD_DIM // LANES
    assert s % (ATTN_BLOCK * ATTN_UNROLL) == 0 and s % (2 * ATTN_BLOCK * max(DILATIONS)) == 0
    assert all(d & (d - 1) == 0 for d in DILATIONS)
    bias = jnp.asarray(_attn_bias())

    def spec(off):
        return pl.BlockSpec((1, s, LANES), lambda i, j: (i, 0, off + j))

    return pl.pallas_call(
        functools.partial(_attn_body, seq=s, unroll=ATTN_UNROLL),
        grid=(b, n_pairs),
        in_specs=[spec(q_col), spec(q_col + n_pairs), spec(q_col + 2 * n_pairs),
                  pl.BlockSpec(bias.shape, lambda i, j: (0, 0, 0))],
        out_specs=pl.BlockSpec((1, s, LANES), lambda i, j: (i, 0, j)),
        out_shape=jax.ShapeDtypeStruct((b, s, n_pairs * LANES), BF16),
        scratch_shapes=[pltpu.VMEM((len(DILATIONS), 3, s, LANES), F32),
                        pltpu.VMEM((len(DILATIONS), s, LANES), F32),
                        pltpu.VMEM((len(DILATIONS), s, LANES), F32)],
        compiler_params=_params(("parallel", "parallel")),
        name="dilated_attn",
    )(z3, z3, z3, bias)


def _ret_body(q_ref, k_ref, v_ref, g_ref, cos_ref, sa_ref, sb_ref, dec_ref, qdec_ref, kdec_ref, cdec_ref,
              avg_ref, gn_ref, o_ref, state):
    cw = q_ref.shape[2]
    n_heads = cw // HEAD_DIM
    head = lax.broadcasted_iota(I32, (1, cw), 1) // HEAD_DIM
    hr = lax.broadcasted_iota(I32, (cw, cw), 0) // HEAD_DIM
    hc = lax.broadcasted_iota(I32, (cw, cw), 1) // HEAD_DIM
    same_head = hr == hc

    @pl.when(pl.program_id(1) == 0)
    def _():
        state[...] = jnp.zeros_like(state)

    def rotary(x, cos, sa, sb):
        half = HEAD_DIM // 2
        return x * cos + pltpu.roll(x, half, 1) * sa + pltpu.roll(x, cw - half, 1) * sb

    def head_mean(a):
        hi, lo = _split_bf16(a)
        return (jnp.dot(hi, avg_ref[...], preferred_element_type=F32)
                + jnp.dot(lo, avg_ref[...], preferred_element_type=F32))

    for c in range(q_ref.shape[1] // RET_CHUNK):
        rows = slice(c * RET_CHUNK, (c + 1) * RET_CHUNK)
        cos, sa, sb = cos_ref[rows, :], sa_ref[rows, :], sb_ref[rows, :]
        qr = rotary(q_ref[0, rows, :].astype(F32), cos, sa, sb)
        kr = rotary(k_ref[0, rows, :].astype(F32), cos, sa, sb) * (HEAD_DIM ** -0.5)
        v = v_ref[0, rows, :]
        krb = kr.astype(BF16)
        inner = [lax.dot_general(jnp.where(head == h, qr, 0.0).astype(BF16), krb, (((1,), (1,)), ((), ())),
                                 preferred_element_type=F32) for h in range(n_heads)]
        inner = jnp.concatenate(inner, axis=1) * dec_ref[...]
        vstack = jnp.concatenate([jnp.where(head == h, v, 0.0) for h in range(n_heads)], axis=0)
        st = state[...]
        o = (jnp.dot(inner.astype(BF16), vstack.astype(BF16), preferred_element_type=F32)
             + jnp.dot((qr * qdec_ref[...]).astype(BF16), st.astype(BF16), preferred_element_type=F32))
        ktv = lax.dot_general((kr * kdec_ref[...]).astype(BF16), v.astype(BF16), (((0,), (0,)), ((), ())),
                              preferred_element_type=F32)
        state[...] = st * cdec_ref[...] + jnp.where(same_head, ktv, 0.0)
        mu = head_mean(o)
        dd = o - mu
        var = head_mean(dd * dd)
        on = dd * lax.rsqrt(var + EPS) * gn_ref[...]
        gate = g_ref[0, rows, :]
        o_ref[0, rows, :] = (gate * jax.nn.sigmoid(gate) * on).astype(o_ref.dtype)


def _ret_tables(seq, n_heads):
    half = HEAD_DIM // 2
    inv = ROPE_BASE ** (-jnp.arange(half, dtype=F32) / half)
    ang = jnp.arange(seq, dtype=F32)[:, None] * inv[None]
    cos, sin = jnp.cos(ang), jnp.sin(ang)
    zero = jnp.zeros_like(sin)
    tile = lambda a: jnp.tile(a, (1, n_heads))
    cos_t = tile(jnp.concatenate([cos, cos], axis=1))
    sa_t = tile(jnp.concatenate([zero, sin], axis=1))
    sb_t = tile(jnp.concatenate([-sin, zero], axis=1))
    log_g = jnp.log(1.0 - 2.0 ** (-5.0 - jnp.arange(n_heads, dtype=F32)))
    c = RET_CHUNK
    idx = jnp.arange(c)
    diff = idx[:, None] - idx[None, :]
    decay_in = jnp.where(diff >= 0, jnp.exp(log_g[:, None, None] * jnp.maximum(diff, 0)[None]), 0.0)
    dec = decay_in.transpose(1, 0, 2).reshape(c, n_heads * c)
    lane_head = jnp.repeat(jnp.arange(n_heads), HEAD_DIM)
    qdec = jnp.exp(log_g[lane_head][None, :] * (idx + 1)[:, None].astype(F32))
    kdec = jnp.exp(log_g[lane_head][None, :] * (c - 1 - idx)[:, None].astype(F32))
    same = lane_head[:, None] == lane_head[None, :]
    cdec = jnp.where(same, jnp.exp(log_g * c)[lane_head][:, None], 0.0)
    avg = jnp.where(same, 1.0 / HEAD_DIM, 0.0).astype(BF16)
    return cos_t, sa_t, sb_t, dec, qdec, kdec, cdec, avg


def _retention(z3, col, gate3, n_heads, gn, tc=1024):
    b, s, _ = z3.shape
    cw = n_heads * HEAD_DIM
    cos_t, sa_t, sb_t, dec, qdec, kdec, cdec, avg = _ret_tables(s, n_heads)

    def zspec(off):
        return pl.BlockSpec((1, tc, cw), lambda i, j: (i, j, off))

    tab = pl.BlockSpec((tc, cw), lambda i, j: (j, 0))
    full = lambda a: pl.BlockSpec(a.shape, lambda i, j: (0, 0))
    gn2 = gn.reshape(1, cw)
    return pl.pallas_call(
        _ret_body,
        grid=(b, s // tc),
        in_specs=[zspec(col), zspec(col + 1), zspec(col + 2), zspec(0), tab, tab, tab,
                  full(dec), full(qdec), full(kdec), full(cdec), full(avg), full(gn2)],
        out_specs=pl.BlockSpec((1, tc, cw), lambda i, j: (i, j, 0)),
        out_shape=jax.ShapeDtypeStruct((b, s, cw), BF16),
        scratch_shapes=[pltpu.VMEM((cw, cw), F32)],
        compiler_params=_params(("parallel", "arbitrary")),
        name="retention",
    )(z3, z3, z3, gate3, cos_t, sa_t, sb_t, dec, qdec, kdec, cdec, avg, gn2)


def _outproj_body(x_ref, a_ref, b_ref, c_ref, wo_ref, g2_ref, rhi_ref, rlo_ref, rb_ref, x1_ref, route_ref, wb):
    @pl.when(pl.program_id(0) == 0)
    def _():
        _cast_rows(wb, wo_ref)

    aw, bw = a_ref.shape[1], b_ref.shape[1]
    y = (jnp.dot(a_ref[...], wb[0:aw, :], preferred_element_type=F32)
         + jnp.dot(b_ref[...], wb[aw:aw + bw, :], preferred_element_type=F32)
         + jnp.dot(c_ref[...], wb[aw + bw:, :], preferred_element_type=F32))
    x1 = x_ref[...] + y
    x1_ref[...] = x1
    hi, lo = _split_bf16(_rms(x1, g2_ref[...]))
    nt = lambda r, h: lax.dot_general(r, h, (((1,), (1,)), ((), ())), preferred_element_type=F32)
    lt = nt(rhi_ref[...], hi) + nt(rhi_ref[...], lo) + nt(rlo_ref[...], hi) + rb_ref[...]
    row = lambda i: lt[i:i + 1, :]
    best, gi = row(0), jnp.zeros_like(row(0), dtype=I32)
    for i in range(1, N_GROUPS):
        up = row(i) > best
        best = jnp.where(up, row(i), best)
        gi = jnp.where(up, i, gi)
    g_w = 1.0 / sum(jnp.exp(row(i) - best) for i in range(N_GROUPS))
    el = []
    for j in range(EXPERTS_PER_GROUP):
        e = row(N_GROUPS + (N_GROUPS - 1) * EXPERTS_PER_GROUP + j)
        for g in range(N_GROUPS - 2, -1, -1):
            e = jnp.where(gi == g, row(N_GROUPS + g * EXPERTS_PER_GROUP + j), e)
        el.append(e)
    v1, i1 = el[0], jnp.zeros_like(gi)
    for j in range(1, EXPERTS_PER_GROUP):
        up = el[j] > v1
        v1 = jnp.where(up, el[j], v1)
        i1 = jnp.where(up, j, i1)
    v2, i2 = jnp.full_like(v1, -jnp.inf), jnp.zeros_like(gi)
    for j in range(EXPERTS_PER_GROUP):
        up = jnp.logical_and(i1 != j, el[j] > v2)
        v2 = jnp.where(up, el[j], v2)
        i2 = jnp.where(up, j, i2)
    e21 = jnp.exp(v2 - v1)
    w1 = g_w / (1.0 + e21)
    w2 = g_w * e21 / (1.0 + e21)
    e1 = (gi * EXPERTS_PER_GROUP + i1).astype(F32)
    e2 = (gi * EXPERTS_PER_GROUP + i2).astype(F32)
    zero = jnp.zeros_like(w1)
    route_ref[...] = jnp.concatenate([e1, e2, w1, w2, zero, zero, zero, zero], axis=0)


def _outproj(x, a, b, c, wo_stack, layer, g2, router_g, router_gb, router_e, router_eb, tm=512):
    t, d = x.shape
    n_logits = N_GROUPS + N_EXPERTS
    n_rows = -(-n_logits // 16) * 16
    r = jnp.pad(jnp.concatenate([router_g, router_e], axis=1).T, ((0, n_rows - n_logits), (0, 0)))
    rhi = r.astype(BF16)
    rlo = (r - rhi.astype(F32)).astype(BF16)
    rb = jnp.pad(jnp.concatenate([router_gb, router_eb]), (0, n_rows - n_logits)).reshape(n_rows, 1)
    row_spec = lambda w: pl.BlockSpec((tm, w), lambda i: (i, 0))
    full = lambda arr: pl.BlockSpec(arr.shape, lambda i: (0, 0))
    g2r = g2.reshape(1, d)
    return pl.pallas_call(
        _outproj_body,
        grid=(t // tm,),
        in_specs=[row_spec(d), row_spec(a.shape[1]), row_spec(b.shape[1]), row_spec(c.shape[1]),
                  _layer_resident(wo_stack, layer), full(g2r), full(rhi), full(rlo), full(rb)],
        out_specs=[row_spec(d), pl.BlockSpec((8, tm), lambda i: (0, i))],
        out_shape=[jax.ShapeDtypeStruct((t, d), F32), jax.ShapeDtypeStruct((8, t), F32)],
        scratch_shapes=[pltpu.VMEM(wo_stack.shape[1:], BF16)],
        compiler_params=_params(("arbitrary",)),
        name="outproj_route",
    )(x, a, b, c, wo_stack, g2r, rhi, rlo, rb)


def _moe_plan(eid, n_tiles):
    flat = eid.reshape(-1)
    onehot = (flat[:, None] == jnp.arange(N_EXPERTS, dtype=I32)[None, :]).astype(I32)
    csum = jnp.cumsum(onehot, axis=0)
    counts = csum[-1]
    padded = ((counts + MOE_TILE - 1) // MOE_TILE) * MOE_TILE
    ends = jnp.cumsum(padded)
    starts = ends - padded
    pos = jnp.sum(onehot * (csum - 1 + starts[None, :]), axis=1)
    tile_start = jnp.arange(n_tiles, dtype=I32) * MOE_TILE
    tile_expert = jnp.sum((ends[None, :] <= tile_start[:, None]).astype(I32), axis=1)
    tile_expert = jnp.minimum(tile_expert, N_EXPERTS - 1).astype(I32)
    used = (ends[-1] // MOE_TILE).astype(I32).reshape(1)
    pad_start = (starts + counts).astype(I32)
    pad_count = (padded - counts).astype(I32)
    return pos.astype(I32), tile_expert, used, pad_start, pad_count


def _store_token_major(ref, x, tok0=0):
    n = x.shape[0]
    for c in range(SUBLANES):
        ref[pl.ds(tok0 * SUBLANES + c, n, stride=SUBLANES), :] = x[:, c * LANES:(c + 1) * LANES]


def _load_token_major(ref, n, tok0=0):
    return jnp.concatenate([ref[pl.ds(tok0 * SUBLANES + c, n, stride=SUBLANES), :] for c in range(SUBLANES)],
                           axis=1)


def _tokens(ref, tok, n=1):
    return ref.at[pl.ds(pl.multiple_of(tok * SUBLANES, SUBLANES), n * SUBLANES)]


def _dispatch_body(pad_start_ref, pad_count_ref, used_ref, pos_ref, x_ref, g2_ref, xs_hbm, xt, zeros, sem, fill_sem,
                   *, tm, n_tokens, n_tiles):
    i = pl.program_id(0)
    half_tile = MOE_TILE // 2

    def fill(act):
        def expert_pad(e, c):
            off, n = pad_start_ref[e], pad_count_ref[e]
            bit = half_tile
            while bit:
                take = (n & bit) != 0

                @pl.when(take)
                def _(off=off, bit=bit):
                    act(pltpu.make_async_copy(_tokens(zeros, 0, bit), _tokens(xs_hbm, off, bit), fill_sem))

                off = off + jnp.where(take, bit, 0)
                bit //= 2
            return c

        lax.fori_loop(0, N_EXPERTS, expert_pad, 0)

        def unused_half_tile(j, c):
            act(pltpu.make_async_copy(zeros, _tokens(xs_hbm, j * half_tile, half_tile), fill_sem))
            return c

        lax.fori_loop(2 * used_ref[0], 2 * n_tiles, unused_half_tile, 0)

    @pl.when(i == 0)
    def _():
        zeros[...] = jnp.zeros_like(zeros)
        fill(lambda cp: cp.start())
        fill(lambda cp: cp.wait())

    _store_token_major(xt, _rms(x_ref[...], g2_ref[...]))

    def issue(g, c):
        for k in range(2):
            base = k * n_tokens + i * tm + g * ROW_UNROLL
            for j in range(ROW_UNROLL):
                dst = _tokens(xs_hbm, pos_ref[base + j])
                pltpu.make_async_copy(_tokens(xt, g * ROW_UNROLL + j), dst, sem).start(priority=j % 2)
        return c

    lax.fori_loop(0, tm // ROW_UNROLL, issue, 0)
    for k in range(2):
        pltpu.make_async_copy(xt, _tokens(xs_hbm, 0, tm), sem).wait()


def _dispatch(x1, g2, pos, n_tiles, pad_start, pad_count, used, tm=1024):
    t, d = x1.shape
    assert d == SUBLANES * LANES
    grid_spec = pltpu.PrefetchScalarGridSpec(
        num_scalar_prefetch=4,
        grid=(t // tm,),
        in_specs=[pl.BlockSpec((tm, d), lambda i, *_: (i, 0)),
                  pl.BlockSpec((1, d), lambda i, *_: (0, 0))],
        out_specs=pl.BlockSpec(memory_space=pl.ANY),
        scratch_shapes=[pltpu.VMEM((tm * SUBLANES, LANES), F32),
                        pltpu.VMEM((MOE_TILE // 2 * SUBLANES, LANES), F32),
                        pltpu.SemaphoreType.DMA, pltpu.SemaphoreType.DMA],
    )
    return pl.pallas_call(
        functools.partial(_dispatch_body, tm=tm, n_tokens=t, n_tiles=n_tiles),
        grid_spec=grid_spec,
        out_shape=jax.ShapeDtypeStruct((n_tiles * MOE_TILE * SUBLANES, LANES), F32),
        compiler_params=_params(("arbitrary",)),
        name="moe_dispatch",
    )(pad_start, pad_count, used, pos, x1, g2.reshape(1, d))


def _ffn_body(te_ref, used_ref, xs_ref, wg_ref, wu_ref, wd_ref, ys_ref, wgu_b, wd_b):
    i = pl.program_id(0)
    live = i < used_ref[0]
    hid = wd_ref.shape[0]

    @pl.when(jnp.logical_and(live, jnp.logical_or(i == 0, te_ref[i] != te_ref[jnp.maximum(i - 1, 0)])))
    def _():
        _cast_rows(wgu_b.at[:, 0:hid], wg_ref)
        _cast_rows(wgu_b.at[:, hid:2 * hid], wu_ref)
        _cast_rows(wd_b, wd_ref)

    @pl.when(live)
    def _():
        half = MOE_TILE // 2
        for tok0 in (0, half):
            hb = _load_token_major(xs_ref, half, tok0).astype(BF16)
            gu = jnp.dot(hb, wgu_b[...], preferred_element_type=F32)
            gate, up = gu[:, :hid], gu[:, hid:]
            act = (gate * jax.nn.sigmoid(gate) * up).astype(BF16)
            _store_token_major(ys_ref, jnp.dot(act, wd_b[...], preferred_element_type=F32), tok0)

    @pl.when(jnp.logical_not(live))
    def _():
        ys_ref[...] = jnp.zeros_like(ys_ref)


def _ffn(xs, w_gate, w_up, w_down, layer, tile_expert, used):
    n_tiles = xs.shape[0] // (MOE_TILE * SUBLANES)
    d, hid = w_down.shape[3], w_down.shape[2]
    tile_rows = MOE_TILE * SUBLANES
    live = lambda i, te, used: jnp.minimum(i, used[0] - 1)
    grid_spec = pltpu.PrefetchScalarGridSpec(
        num_scalar_prefetch=2,
        grid=(n_tiles,),
        in_specs=[pl.BlockSpec((tile_rows, LANES), lambda i, te, used: (live(i, te, used), 0)),
                  pl.BlockSpec((None, None, d, hid), lambda i, te, used: (layer, te[i], 0, 0)),
                  pl.BlockSpec((None, None, d, hid), lambda i, te, used: (layer, te[i], 0, 0)),
                  pl.BlockSpec((None, None, hid, d), lambda i, te, used: (layer, te[i], 0, 0))],
        out_specs=pl.BlockSpec((tile_rows, LANES), lambda i, te, used: (i, 0)),
        scratch_shapes=[pltpu.VMEM((d, 2 * hid), BF16), pltpu.VMEM((hid, d), BF16)],
    )
    return pl.pallas_call(
        _ffn_body,
        grid_spec=grid_spec,
        out_shape=jax.ShapeDtypeStruct(xs.shape, F32),
        compiler_params=_params(("arbitrary",)),
        name="moe_experts",
    )(tile_expert, used, xs, w_gate, w_up, w_down)


def _combine_body(pos_ref, ys_hbm, x1_ref, w_ref, gf_ref, o_ref, buf, sems, *, tm, n_tokens, final_norm):
    i = pl.program_id(0)
    slot = i % 2

    def gather(step, s):
        def issue(g, c):
            for k in range(2):
                base = k * n_tokens + step * tm + g * ROW_UNROLL
                for j in range(ROW_UNROLL):
                    src = _tokens(ys_hbm, pos_ref[base + j])
                    pltpu.make_async_copy(src, _tokens(buf.at[s, k], g * ROW_UNROLL + j),
                                          sems.at[s]).start(priority=j % 2)
            return c

        lax.fori_loop(0, tm // ROW_UNROLL, issue, 0)

    @pl.when(i == 0)
    def _():
        gather(i, slot)

    @pl.when(i + 1 < pl.num_programs(0))
    def _():
        gather(i + 1, 1 - slot)

    for k in range(2):
        pltpu.make_async_copy(_tokens(ys_hbm, 0, tm), buf.at[slot, k], sems.at[slot]).wait()
    w = w_ref[...]
    y = (x1_ref[...] + w[:, 0:1] * _load_token_major(buf.at[slot, 0], tm)
         + w[:, 1:2] * _load_token_major(buf.at[slot, 1], tm))
    if final_norm:
        y = _rms(y, gf_ref[...])
    o_ref[...] = y


def _combine(ys, x1, pos, w2, gf, final_norm, tm=512):
    t, d = x1.shape
    grid_spec = pltpu.PrefetchScalarGridSpec(
        num_scalar_prefetch=1,
        grid=(t // tm,),
        in_specs=[pl.BlockSpec(memory_space=pl.ANY),
                  pl.BlockSpec((tm, d), lambda i, *_: (i, 0)),
                  pl.BlockSpec((tm, 2), lambda i, *_: (i, 0)),
                  pl.BlockSpec((1, d), lambda i, *_: (0, 0))],
        out_specs=pl.BlockSpec((tm, d), lambda i, *_: (i, 0)),
        scratch_shapes=[pltpu.VMEM((2, 2, tm * SUBLANES, LANES), F32), pltpu.SemaphoreType.DMA((2,))],
    )
    return pl.pallas_call(
        functools.partial(_combine_body, tm=tm, n_tokens=t, final_norm=final_norm),
        grid_spec=grid_spec,
        out_shape=jax.ShapeDtypeStruct((t, d), F32),
        compiler_params=_params(("arbitrary",)),
        name="moe_combine",
    )(pos, ys, x1, w2, gf.reshape(1, d))


def _layer(x, batch, seq, layer, norm1, w_in, a_ln_g, a_ln_b, a_ws, a_bs, ret_gn, w_out, norm2,
           router_g, router_gb, router_e, router_eb, w_gate, w_up, w_down, final_gain, is_last):
    t, d = x.shape
    a_groups = a_ws.shape[0]
    aw = a_groups * HEAD_DIM
    c_heads = ret_gn.shape[0] // HEAD_DIM
    cw = c_heads * HEAD_DIM
    bw = w_in.shape[2] - 2 * aw - 4 * cw
    b_heads = (bw // 3) // HEAD_DIM
    za, zqkv, zg = _inproj(x, norm1, w_in, layer, (2 * aw, bw + 3 * cw, cw), (F32, BF16, F32))
    zqkv3 = zqkv.reshape(batch, seq, zqkv.shape[1])
    a_out = _sgu(za, a_ln_g, a_ln_b, a_ws, a_bs)
    b_out = _attn(zqkv3, 0, b_heads).reshape(t, bw // 3)
    c_out = _retention(zqkv3, bw // cw, zg.reshape(batch, seq, cw), c_heads, ret_gn).reshape(t, cw)
    x1, route = _outproj(x, a_out, b_out, c_out, w_out, layer, norm2,
                         router_g, router_gb, router_e, router_eb)
    eid = route[0:2].astype(I32)
    w2 = route[2:4].T
    n_tiles = (2 * t) // MOE_TILE + N_EXPERTS
    pos, tile_expert, used, pad_start, pad_count = _moe_plan(eid, n_tiles)
    xs = _dispatch(x1, norm2, pos, n_tiles, pad_start, pad_count, used)
    ys = _ffn(xs, w_gate, w_up, w_down, layer, tile_expert, used)
    return _combine(ys, x1, pos, w2, final_gain, is_last)


def kernel(x, norm1, w_in, a_ln_g, a_ln_b, a_ws, a_bs, ret_gn, w_out, norm2, router_g, router_gb, router_e,
           router_eb, w_gate, w_up, w_down, final_norm):
    batch, seq, d = x.shape
    depth = norm1.shape[0]
    h = x.reshape(batch * seq, d)
    for l in range(depth):
        h = _layer(h, batch, seq, l, norm1[l], w_in, a_ln_g[l], a_ln_b[l], a_ws[l], a_bs[l], ret_gn[l],
                   w_out, norm2[l], router_g[l], router_gb[l], router_e[l], router_eb[l],
                   w_gate, w_up, w_down, final_norm, l == depth - 1)
    return h.reshape(batch, seq, d)
```

```python
import functools
import itertools

import numpy as np
import jax
import jax.numpy as jnp
from jax import lax
from jax.experimental import pallas as pl
from jax.experimental.pallas import tpu as pltpu

F32 = jnp.float32
BF16 = jnp.bfloat16
I32 = jnp.int32

EPS = 1e-6
HEAD_DIM = 64
SGU_CHUNK = 128
RET_CHUNK = 128
ATTN_BLOCK = 128
DILATIONS = (1, 4, 16)
ATTN_UNROLL = 8
ROPE_BASE = 10000.0
N_GROUPS = 4
EXPERTS_PER_GROUP = 4
N_EXPERTS = N_GROUPS * EXPERTS_PER_GROUP
_PAIRS = tuple(itertools.combinations(range(EXPERTS_PER_GROUP), 2))
N_CLASSES = N_GROUPS * len(_PAIRS)
LANES = 128
SUBLANES = 8
MOE_TILE = 256
ROW_UNROLL = 8
VMEM_LIMIT = 56 * 1024 * 1024


def _params(sem, vmem=VMEM_LIMIT):
    return pltpu.CompilerParams(dimension_semantics=sem, vmem_limit_bytes=vmem)


def _rms(x, g):
    return x * lax.rsqrt(jnp.mean(x * x, axis=-1, keepdims=True) + EPS) * g


def _split_bf16(a):
    hi = a.astype(BF16)
    lo = (a - hi.astype(F32)).astype(BF16)
    return hi, lo


def _cast_rows(dst_ref, src_ref, rows=256):
    for r0 in range(0, src_ref.shape[0], rows):
        dst_ref[r0:r0 + rows, :] = src_ref[r0:r0 + rows, :].astype(dst_ref.dtype)


def _inproj_body(x_ref, g_ref, w_ref, *refs, n_chunk):
    out_refs, wb = refs[:-1], refs[-1]

    @pl.when(pl.program_id(0) == 0)
    def _():
        _cast_rows(wb, w_ref)

    hb = _rms(x_ref[...], g_ref[...]).astype(BF16)
    col = 0
    for o_ref in out_refs:
        for n0 in range(0, o_ref.shape[1], n_chunk):
            n1 = min(n0 + n_chunk, o_ref.shape[1])
            acc = jnp.dot(hb, wb[:, col + n0:col + n1], preferred_element_type=F32)
            o_ref[:, n0:n1] = acc.astype(o_ref.dtype)
        col += o_ref.shape[1]


def _layer_resident(stacked, layer):
    rest = stacked.shape[1:]
    return pl.BlockSpec((None,) + rest, lambda *_: (layer,) + (0,) * len(rest), pipeline_mode=pl.Buffered(1))


def _inproj(x, g, w_stack, layer, widths, dtypes, tm=1024, n_chunk=512):
    t, d = x.shape
    assert sum(widths) == w_stack.shape[2]
    return pl.pallas_call(
        functools.partial(_inproj_body, n_chunk=n_chunk),
        grid=(t // tm,),
        in_specs=[pl.BlockSpec((tm, d), lambda i: (i, 0)),
                  pl.BlockSpec((1, d), lambda i: (0, 0)),
                  _layer_resident(w_stack, layer)],
        out_specs=[pl.BlockSpec((tm, n), lambda i: (i, 0)) for n in widths],
        out_shape=[jax.ShapeDtypeStruct((t, n), dt) for n, dt in zip(widths, dtypes)],
        scratch_shapes=[pltpu.VMEM(w_stack.shape[1:], BF16)],
        compiler_params=_params(("arbitrary",)),
        name="inproj",
    )(x, g.reshape(1, d), w_stack)


def _gelu_tanh(x):
    return 0.5 * x * (1.0 + jnp.tanh(np.sqrt(2.0 / np.pi).astype(np.float32) * (x + 0.044715 * (x * x * x))))


def _sgu_body(za_ref, lng_ref, lnb_ref, wcat_ref, bias_ref, o_ref):
    ga = _gelu_tanh(za_ref[...])
    aw = ga.shape[1] // 2
    u = ga[:, :aw]
    v = ga[:, aw:]
    mu = jnp.mean(v, axis=-1, keepdims=True)
    dv = v - mu
    var = jnp.mean(dv * dv, axis=-1, keepdims=True)
    vn = dv * lax.rsqrt(var + EPS) * lng_ref[...] + lnb_ref[...]
    group = lax.broadcasted_iota(I32, (1, aw), 1) // HEAD_DIM
    n_groups = aw // HEAD_DIM
    for c in range(ga.shape[0] // SGU_CHUNK):
        rows = slice(c * SGU_CHUNK, (c + 1) * SGU_CHUNK)
        vc = vn[rows]
        stack = jnp.concatenate([jnp.where(group == g, vc, 0.0) for g in range(n_groups)], axis=0)
        s = jnp.dot(wcat_ref[...], stack.astype(BF16), preferred_element_type=F32) + bias_ref[...]
        o_ref[rows, :] = (u[rows] * s).astype(o_ref.dtype)


def _sgu(z, ln_g, ln_b, w_s, b_s, tm=1024):
    t = z.shape[0]
    n_groups = w_s.shape[0]
    aw = n_groups * HEAD_DIM
    causal = jnp.tril(jnp.ones((SGU_CHUNK, SGU_CHUNK), dtype=bool))
    wcat = jnp.where(causal[None], w_s, 0.0).transpose(1, 0, 2).reshape(SGU_CHUNK, n_groups * SGU_CHUNK)
    bias = jnp.repeat(b_s.T, HEAD_DIM, axis=1)
    return pl.pallas_call(
        _sgu_body,
        grid=(t // tm,),
        in_specs=[pl.BlockSpec((tm, 2 * aw), lambda i: (i, 0)),
                  pl.BlockSpec((1, aw), lambda i: (0, 0)),
                  pl.BlockSpec((1, aw), lambda i: (0, 0)),
                  pl.BlockSpec(wcat.shape, lambda i: (0, 0)),
                  pl.BlockSpec(bias.shape, lambda i: (0, 0))],
        out_specs=pl.BlockSpec((tm, aw), lambda i: (i, 0)),
        out_shape=jax.ShapeDtypeStruct((t, aw), BF16),
        compiler_params=_params(("parallel",)),
        name="sgu",
    )(z, ln_g.reshape(1, aw), ln_b.reshape(1, aw), wcat.astype(BF16), bias)


def _attn_bias():
    blk = ATTN_BLOCK
    qi = np.arange(2 * blk)[:, None] % blk
    ci = np.arange(2 * blk)[None, :]
    first = (ci < blk) & (ci <= qi)
    later = np.where(ci < blk, ci >= qi, ci - blk <= qi)
    return np.where(np.stack([first, later]), 0.0, -np.inf).astype(np.float32)


def _attn_body(q_ref, k_ref, v_ref, bias_ref, o_ref, qkv_scr, ob_scr, lse_scr, *, seq, unroll):
    blk = ATTN_BLOCK
    head0 = lax.broadcasted_iota(I32, (1, LANES), 1) < HEAD_DIM
    scale = HEAD_DIM ** -0.5 * np.log2(np.e)

    widen_rows = 512

    def widen(i, carry):
        sl = pl.ds(pl.multiple_of(i * widen_rows, widen_rows), widen_rows)
        qkv_scr[0, 0, sl, :] = q_ref[0, sl, :].astype(F32) * scale
        qkv_scr[0, 1, sl, :] = k_ref[0, sl, :].astype(F32)
        qkv_scr[0, 2, sl, :] = v_ref[0, sl, :].astype(F32)
        return carry

    lax.fori_loop(0, seq // widen_rows, widen, 0)

    src = {DILATIONS[0]: qkv_scr.at[0]}
    for stage, d in enumerate(DILATIONS[1:], start=1):
        prev_d, ratio, slab = DILATIONS[stage - 1], d // DILATIONS[stage - 1], seq // d
        prev, cur = qkv_scr.at[stage - 1], qkv_scr.at[stage]
        for r_prev in range(prev_d):
            for c in range(ratio):
                r = r_prev + prev_d * c
                for a in range(3):
                    cur[a, r * slab:(r + 1) * slab, :] = prev[a, pl.ds(r_prev * (seq // prev_d) + c, slab,
                                                                     stride=ratio), :]
        src[d] = cur

    def rows(start, n, d):
        return pl.ds(start, n) if d == 1 else pl.ds(start, n, stride=d)

    def block(p, d, r, n):
        base = n * (blk * d) + r
        first = r * (seq // d) + n * blk
        kfirst = r * (seq // d) + jnp.maximum(n - 1, 0) * blk
        first, kfirst = pl.multiple_of(first, blk), pl.multiple_of(kfirst, blk)
        qb = src[d][0, pl.ds(first, blk), :]
        qs = jnp.concatenate([jnp.where(head0, qb, 0.0), jnp.where(head0, 0.0, qb)], axis=0).astype(BF16)
        kb = src[d][1, pl.ds(kfirst, 2 * blk), :].astype(BF16)
        vb = src[d][2, pl.ds(kfirst, 2 * blk), :].astype(BF16)
        va = jnp.concatenate([vb, jnp.ones_like(vb)], axis=1)
        s = lax.dot_general(qs, kb, (((1,), (1,)), ((), ())), preferred_element_type=F32)
        s = s + bias_ref[jnp.minimum(n, 1)]
        m = jnp.max(s, axis=-1, keepdims=True)
        e = jnp.exp2(s - m)
        oa = jnp.dot(e.astype(BF16), va, preferred_element_type=F32)
        l = oa[:, LANES:]
        o = oa[:, :LANES] / l
        lse = m + jnp.log2(l)
        ob_scr[p, rows(base, blk, d), :] = jnp.where(head0, o[:blk], o[blk:])
        lse_scr[p, rows(base, blk, d), :] = jnp.where(head0, lse[:blk], lse[blk:])

    for p, d in enumerate(DILATIONS):
        def group(i, carry, p=p, d=d):
            for u in range(unroll):
                b = i * unroll + u
                block(p, d, b & (d - 1), b >> (d.bit_length() - 1))
            return carry

        lax.fori_loop(0, seq // (blk * unroll), group, 0)

    step = 256

    def mix(i, carry):
        sl = pl.ds(pl.multiple_of(i * step, step), step)
        ls = [lse_scr[p, sl, :] for p in range(len(DILATIONS))]
        m = functools.reduce(jnp.maximum, ls)
        es = [jnp.exp2(l - m) for l in ls]
        num = sum(e * ob_scr[p, sl, :] for p, e in enumerate(es))
        o_ref[0, sl, :] = (num / sum(es)).astype(o_ref.dtype)
        return carry

    lax.fori_loop(0, seq // step, mix, 0)


def _attn(z3, q_col, n_heads):
    b, s, _ = z3.shape
    n_pairs = n_heads * HEAD_DIM // LANES
    assert s % (ATTN_BLOCK * ATTN_UNROLL) == 0 and s % (2 * ATTN_BLOCK * max(DILATIONS)) == 0
    assert all(d & (d - 1) == 0 for d in DILATIONS)
    bias = jnp.asarray(_attn_bias())

    def spec(off):
        return pl.BlockSpec((1, s, LANES), lambda i, j: (i, 0, off + j))

    return pl.pallas_call(
        functools.partial(_attn_body, seq=s, unroll=ATTN_UNROLL),
        grid=(b, n_pairs),
        in_specs=[spec(q_col), spec(q_col + n_pairs), spec(q_col + 2 * n_pairs),
                  pl.BlockSpec(bias.shape, lambda i, j: (0, 0, 0))],
        out_specs=pl.BlockSpec((1, s, LANES), lambda i, j: (i, 0, j)),
        out_shape=jax.ShapeDtypeStruct((b, s, n_pairs * LANES), BF16),
        scratch_shapes=[pltpu.VMEM((len(DILATIONS), 3, s, LANES), F32),
                        pltpu.VMEM((len(DILATIONS), s, LANES), F32),
                        pltpu.VMEM((len(DILATIONS), s, LANES), F32)],
        compiler_params=_params(("parallel", "parallel")),
        name="dilated_attn",
    )(z3, z3, z3, bias)


def _ret_body(q_ref, k_ref, v_ref, g_ref, cos_ref, sa_ref, sb_ref, dec_ref, qdec_ref, kdec_ref, cdec_ref,
              avg_ref, gn_ref, o_ref, state):
    cw = q_ref.shape[2]
    n_heads = cw // HEAD_DIM
    head = lax.broadcasted_iota(I32, (1, cw), 1) // HEAD_DIM
    hr = lax.broadcasted_iota(I32, (cw, cw), 0) // HEAD_DIM
    hc = lax.broadcasted_iota(I32, (cw, cw), 1) // HEAD_DIM
    same_head = hr == hc

    @pl.when(pl.program_id(1) == 0)
    def _():
        state[...] = jnp.zeros_like(state)

    def rotary(x, cos, sa, sb):
        half = HEAD_DIM // 2
        return x * cos + pltpu.roll(x, half, 1) * sa + pltpu.roll(x, cw - half, 1) * sb

    def head_mean(a):
        hi, lo = _split_bf16(a)
        return (jnp.dot(hi, avg_ref[...], preferred_element_type=F32)
                + jnp.dot(lo, avg_ref[...], preferred_element_type=F32))

    for c in range(q_ref.shape[1] // RET_CHUNK):
        rows = slice(c * RET_CHUNK, (c + 1) * RET_CHUNK)
        cos, sa, sb = cos_ref[rows, :], sa_ref[rows, :], sb_ref[rows, :]
        qr = rotary(q_ref[0, rows, :].astype(F32), cos, sa, sb)
        kr = rotary(k_ref[0, rows, :].astype(F32), cos, sa, sb) * (HEAD_DIM ** -0.5)
        v = v_ref[0, rows, :]
        krb = kr.astype(BF16)
        inner = [lax.dot_general(jnp.where(head == h, qr, 0.0).astype(BF16), krb, (((1,), (1,)), ((), ())),
                                 preferred_element_type=F32) for h in range(n_heads)]
        inner = jnp.concatenate(inner, axis=1) * dec_ref[...]
        vstack = jnp.concatenate([jnp.where(head == h, v, 0.0) for h in range(n_heads)], axis=0)
        st = state[...]
        o = (jnp.dot(inner.astype(BF16), vstack.astype(BF16), preferred_element_type=F32)
             + jnp.dot((qr * qdec_ref[...]).astype(BF16), st.astype(BF16), preferred_element_type=F32))
        ktv = lax.dot_general((kr * kdec_ref[...]).astype(BF16), v.astype(BF16), (((0,), (0,)), ((), ())),
                              preferred_element_type=F32)
        state[...] = st * cdec_ref[...] + jnp.where(same_head, ktv, 0.0)
        mu = head_mean(o)
        dd = o - mu
        var = head_mean(dd * dd)
        on = dd * lax.rsqrt(var + EPS) * gn_ref[...]
        gate = g_ref[0, rows, :]
        o_ref[0, rows, :] = (gate * jax.nn.sigmoid(gate) * on).astype(o_ref.dtype)


def _ret_tables(seq, n_heads):
    half = HEAD_DIM // 2
    inv = ROPE_BASE ** (-jnp.arange(half, dtype=F32) / half)
    ang = jnp.arange(seq, dtype=F32)[:, None] * inv[None]
    cos, sin = jnp.cos(ang), jnp.sin(ang)
    zero = jnp.zeros_like(sin)
    tile = lambda a: jnp.tile(a, (1, n_heads))
    cos_t = tile(jnp.concatenate([cos, cos], axis=1))
    sa_t = tile(jnp.concatenate([zero, sin], axis=1))
    sb_t = tile(jnp.concatenate([-sin, zero], axis=1))
    log_g = jnp.log(1.0 - 2.0 ** (-5.0 - jnp.arange(n_heads, dtype=F32)))
    c = RET_CHUNK
    idx = jnp.arange(c)
    diff = idx[:, None] - idx[None, :]
    decay_in = jnp.where(diff >= 0, jnp.exp(log_g[:, None, None] * jnp.maximum(diff, 0)[None]), 0.0)
    dec = decay_in.transpose(1, 0, 2).reshape(c, n_heads * c)
    lane_head = jnp.repeat(jnp.arange(n_heads), HEAD_DIM)
    qdec = jnp.exp(log_g[lane_head][None, :] * (idx + 1)[:, None].astype(F32))
    kdec = jnp.exp(log_g[lane_head][None, :] * (c - 1 - idx)[:, None].astype(F32))
    same = lane_head[:, None] == lane_head[None, :]
    cdec = jnp.where(same, jnp.exp(log_g * c)[lane_head][:, None], 0.0)
    avg = jnp.where(same, 1.0 / HEAD_DIM, 0.0).astype(BF16)
    return cos_t, sa_t, sb_t, dec, qdec, kdec, cdec, avg


def _retention(z3, col, gate3, n_heads, gn, tc=1024):
    b, s, _ = z3.shape
    cw = n_heads * HEAD_DIM
    cos_t, sa_t, sb_t, dec, qdec, kdec, cdec, avg = _ret_tables(s, n_heads)

    def zspec(off):
        return pl.BlockSpec((1, tc, cw), lambda i, j: (i, j, off))

    tab = pl.BlockSpec((tc, cw), lambda i, j: (j, 0))
    full = lambda a: pl.BlockSpec(a.shape, lambda i, j: (0, 0))
    gn2 = gn.reshape(1, cw)
    return pl.pallas_call(
        _ret_body,
        grid=(b, s // tc),
        in_specs=[zspec(col), zspec(col + 1), zspec(col + 2), zspec(0), tab, tab, tab,
                  full(dec), full(qdec), full(kdec), full(cdec), full(avg), full(gn2)],
        out_specs=pl.BlockSpec((1, tc, cw), lambda i, j: (i, j, 0)),
        out_shape=jax.ShapeDtypeStruct((b, s, cw), BF16),
        scratch_shapes=[pltpu.VMEM((cw, cw), F32)],
        compiler_params=_params(("parallel", "arbitrary")),
        name="retention",
    )(z3, z3, z3, gate3, cos_t, sa_t, sb_t, dec, qdec, kdec, cdec, avg, gn2)


def _outproj_body(x_ref, a_ref, b_ref, c_ref, wo_ref, g2_ref, rhi_ref, rlo_ref, rb_ref, x1_ref, route_ref, wb):
    @pl.when(pl.program_id(0) == 0)
    def _():
        _cast_rows(wb, wo_ref)

    aw, bw = a_ref.shape[1], b_ref.shape[1]
    y = (jnp.dot(a_ref[...], wb[0:aw, :], preferred_element_type=F32)
         + jnp.dot(b_ref[...], wb[aw:aw + bw, :], preferred_element_type=F32)
         + jnp.dot(c_ref[...], wb[aw + bw:, :], preferred_element_type=F32))
    x1 = x_ref[...] + y
    x1_ref[...] = x1
    hi, lo = _split_bf16(_rms(x1, g2_ref[...]))
    nt = lambda r, h: lax.dot_general(r, h, (((1,), (1,)), ((), ())), preferred_element_type=F32)
    lt = nt(rhi_ref[...], hi) + nt(rhi_ref[...], lo) + nt(rlo_ref[...], hi) + rb_ref[...]
    row = lambda i: lt[i:i + 1, :]
    best, gi = row(0), jnp.zeros_like(row(0), dtype=I32)
    for i in range(1, N_GROUPS):
        up = row(i) > best
        best = jnp.where(up, row(i), best)
        gi = jnp.where(up, i, gi)
    g_w = 1.0 / sum(jnp.exp(row(i) - best) for i in range(N_GROUPS))
    el = []
    for j in range(EXPERTS_PER_GROUP):
        e = row(N_GROUPS + (N_GROUPS - 1) * EXPERTS_PER_GROUP + j)
        for g in range(N_GROUPS - 2, -1, -1):
            e = jnp.where(gi == g, row(N_GROUPS + g * EXPERTS_PER_GROUP + j), e)
        el.append(e)
    v1, i1 = el[0], jnp.zeros_like(gi)
    for j in range(1, EXPERTS_PER_GROUP):
        up = el[j] > v1
        v1 = jnp.where(up, el[j], v1)
        i1 = jnp.where(up, j, i1)
    v2, i2 = jnp.full_like(v1, -jnp.inf), jnp.zeros_like(gi)
    for j in range(EXPERTS_PER_GROUP):
        up = jnp.logical_and(i1 != j, el[j] > v2)
        v2 = jnp.where(up, el[j], v2)
        i2 = jnp.where(up, j, i2)
    e21 = jnp.exp(v2 - v1)
    w1 = g_w / (1.0 + e21)
    w2 = g_w * e21 / (1.0 + e21)
    swap = i2 < i1
    e_lo, e_hi = jnp.where(swap, i2, i1), jnp.where(swap, i1, i2)
    w_lo, w_hi = jnp.where(swap, w2, w1), jnp.where(swap, w1, w2)
    first_pair = jnp.zeros_like(e_lo)
    for lo in range(1, EXPERTS_PER_GROUP - 1):
        first_pair = jnp.where(e_lo >= lo, _PAIRS.index((lo, lo + 1)), first_pair)
    cls = (gi * len(_PAIRS) + first_pair + e_hi - e_lo - 1).astype(F32)
    zero = jnp.zeros_like(w1)
    route_ref[...] = jnp.concatenate([cls, w_lo, w_hi, zero, zero, zero, zero, zero], axis=0)


def _outproj(x, a, b, c, wo_stack, layer, g2, router_g, router_gb, router_e, router_eb, tm=512):
    t, d = x.shape
    n_logits = N_GROUPS + N_EXPERTS
    n_rows = -(-n_logits // 16) * 16
    r = jnp.pad(jnp.concatenate([router_g, router_e], axis=1).T, ((0, n_rows - n_logits), (0, 0)))
    rhi = r.astype(BF16)
    rlo = (r - rhi.astype(F32)).astype(BF16)
    rb = jnp.pad(jnp.concatenate([router_gb, router_eb]), (0, n_rows - n_logits)).reshape(n_rows, 1)
    row_spec = lambda w: pl.BlockSpec((tm, w), lambda i: (i, 0))
    full = lambda arr: pl.BlockSpec(arr.shape, lambda i: (0, 0))
    g2r = g2.reshape(1, d)
    return pl.pallas_call(
        _outproj_body,
        grid=(t // tm,),
        in_specs=[row_spec(d), row_spec(a.shape[1]), row_spec(b.shape[1]), row_spec(c.shape[1]),
                  _layer_resident(wo_stack, layer), full(g2r), full(rhi), full(rlo), full(rb)],
        out_specs=[row_spec(d), pl.BlockSpec((8, tm), lambda i: (0, i))],
        out_shape=[jax.ShapeDtypeStruct((t, d), F32), jax.ShapeDtypeStruct((8, t), F32)],
        scratch_shapes=[pltpu.VMEM(wo_stack.shape[1:], BF16)],
        compiler_params=_params(("arbitrary",)),
        name="outproj_route",
    )(x, a, b, c, wo_stack, g2r, rhi, rlo, rb)


def _moe_plan(cls, n_tiles):
    onehot = (cls[:, None] == jnp.arange(N_CLASSES, dtype=I32)[None, :]).astype(I32)
    csum = jnp.cumsum(onehot, axis=0)
    counts = csum[-1]
    padded = ((counts + MOE_TILE - 1) // MOE_TILE) * MOE_TILE
    ends = jnp.cumsum(padded)
    starts = ends - padded
    pos = jnp.sum(onehot * (csum - 1 + starts[None, :]), axis=1)
    tile_start = jnp.arange(n_tiles, dtype=I32) * MOE_TILE
    tile_cls = jnp.sum((ends[None, :] <= tile_start[:, None]).astype(I32), axis=1)
    tile_cls = jnp.minimum(tile_cls, N_CLASSES - 1)
    group, pair = np.divmod(np.arange(N_CLASSES), len(_PAIRS))
    pairs = np.asarray(_PAIRS)
    lo_expert = jnp.asarray(group * EXPERTS_PER_GROUP + pairs[pair, 0], I32)[tile_cls]
    hi_expert = jnp.asarray(group * EXPERTS_PER_GROUP + pairs[pair, 1], I32)[tile_cls]
    used = (ends[-1] // MOE_TILE).astype(I32).reshape(1)
    pad_start = (starts + counts).astype(I32)
    pad_count = (padded - counts).astype(I32)
    return pos.astype(I32), lo_expert, hi_expert, used, pad_start, pad_count


def _store_token_major(ref, x, tok0=0):
    n = x.shape[0]
    for c in range(SUBLANES):
        ref[pl.ds(tok0 * SUBLANES + c, n, stride=SUBLANES), :] = x[:, c * LANES:(c + 1) * LANES]


def _load_token_major(ref, n, tok0=0):
    return jnp.concatenate([ref[pl.ds(tok0 * SUBLANES + c, n, stride=SUBLANES), :] for c in range(SUBLANES)],
                           axis=1)


def _tokens(ref, tok, n=1):
    return ref.at[pl.ds(pl.multiple_of(tok * SUBLANES, SUBLANES), n * SUBLANES)]


def _dispatch_body(pad_start_ref, pad_count_ref, used_ref, pos_ref, x_ref, g2_ref, xs_hbm, xt, zeros, sem, fill_sem,
                   *, tm, n_tiles):
    i = pl.program_id(0)
    half_tile = MOE_TILE // 2

    def fill(act):
        def class_pad(e, c):
            off, n = pad_start_ref[e], pad_count_ref[e]
            bit = half_tile
            while bit:
                take = (n & bit) != 0

                @pl.when(take)
                def _(off=off, bit=bit):
                    act(pltpu.make_async_copy(_tokens(zeros, 0, bit), _tokens(xs_hbm, off, bit), fill_sem))

                off = off + jnp.where(take, bit, 0)
                bit //= 2
            return c

        lax.fori_loop(0, N_CLASSES, class_pad, 0)

        def unused_half_tile(j, c):
            act(pltpu.make_async_copy(zeros, _tokens(xs_hbm, j * half_tile, half_tile), fill_sem))
            return c

        lax.fori_loop(2 * used_ref[0], 2 * n_tiles, unused_half_tile, 0)

    @pl.when(i == 0)
    def _():
        zeros[...] = jnp.zeros_like(zeros)
        fill(lambda cp: cp.start())
        fill(lambda cp: cp.wait())

    _store_token_major(xt, _rms(x_ref[...], g2_ref[...]))

    def issue(g, c):
        base = i * tm + g * ROW_UNROLL
        for j in range(ROW_UNROLL):
            dst = _tokens(xs_hbm, pos_ref[base + j])
            pltpu.make_async_copy(_tokens(xt, g * ROW_UNROLL + j), dst, sem).start(priority=j % 2)
        return c

    lax.fori_loop(0, tm // ROW_UNROLL, issue, 0)
    pltpu.make_async_copy(xt, _tokens(xs_hbm, 0, tm), sem).wait()


def _dispatch(x1, g2, pos, n_tiles, pad_start, pad_count, used, tm=1024):
    t, d = x1.shape
    assert d == SUBLANES * LANES
    grid_spec = pltpu.PrefetchScalarGridSpec(
        num_scalar_prefetch=4,
        grid=(t // tm,),
        in_specs=[pl.BlockSpec((tm, d), lambda i, *_: (i, 0)),
                  pl.BlockSpec((1, d), lambda i, *_: (0, 0))],
        out_specs=pl.BlockSpec(memory_space=pl.ANY),
        scratch_shapes=[pltpu.VMEM((tm * SUBLANES, LANES), F32),
                        pltpu.VMEM((MOE_TILE // 2 * SUBLANES, LANES), F32),
                        pltpu.SemaphoreType.DMA, pltpu.SemaphoreType.DMA],
    )
    return pl.pallas_call(
        functools.partial(_dispatch_body, tm=tm, n_tiles=n_tiles),
        grid_spec=grid_spec,
        out_shape=jax.ShapeDtypeStruct((n_tiles * MOE_TILE * SUBLANES, LANES), F32),
        compiler_params=_params(("arbitrary",)),
        name="moe_dispatch",
    )(pad_start, pad_count, used, pos, x1, g2.reshape(1, d))


def _ffn_body(lo_ref, hi_ref, used_ref, xs_ref, *refs):
    w_refs, ys_ref, wgu_b, wd_b = refs[:6], refs[6], refs[7], refs[8]
    i = pl.program_id(0)
    live = i < used_ref[0]
    hid = wd_b.shape[1]
    prev = jnp.maximum(i - 1, 0)

    for k, e_ref in enumerate((lo_ref, hi_ref)):
        wg_ref, wu_ref, wd_ref = w_refs[3 * k:3 * k + 3]

        @pl.when(jnp.logical_and(live, jnp.logical_or(i == 0, e_ref[i] != e_ref[prev])))
        def _(k=k, wg_ref=wg_ref, wu_ref=wu_ref, wd_ref=wd_ref):
            _cast_rows(wgu_b.at[k, :, 0:hid], wg_ref)
            _cast_rows(wgu_b.at[k, :, hid:2 * hid], wu_ref)
            _cast_rows(wd_b.at[k], wd_ref)

    @pl.when(live)
    def _():
        hb = _load_token_major(xs_ref, MOE_TILE).astype(BF16)
        for k in range(2):
            gu = jnp.dot(hb, wgu_b[k], preferred_element_type=F32)
            gate, up = gu[:, :hid], gu[:, hid:]
            act = (gate * jax.nn.sigmoid(gate) * up).astype(BF16)
            _store_token_major(ys_ref.at[k], jnp.dot(act, wd_b[k], preferred_element_type=F32))

    @pl.when(jnp.logical_not(live))
    def _():
        ys_ref[...] = jnp.zeros_like(ys_ref)


def _ffn(xs, w_gate, w_up, w_down, layer, lo_expert, hi_expert, used):
    n_tiles = xs.shape[0] // (MOE_TILE * SUBLANES)
    d, hid = w_down.shape[3], w_down.shape[2]
    tile_rows = MOE_TILE * SUBLANES
    live = lambda i, used: jnp.minimum(i, used[0] - 1)

    def weights(pick):
        up = pl.BlockSpec((None, None, d, hid), lambda i, lo, hi, used: (layer, pick(lo, hi)[i], 0, 0))
        down = pl.BlockSpec((None, None, hid, d), lambda i, lo, hi, used: (layer, pick(lo, hi)[i], 0, 0))
        return [up, up, down]

    grid_spec = pltpu.PrefetchScalarGridSpec(
        num_scalar_prefetch=3,
        grid=(n_tiles,),
        in_specs=[pl.BlockSpec((tile_rows, LANES), lambda i, lo, hi, used: (live(i, used), 0))]
        + weights(lambda lo, hi: lo) + weights(lambda lo, hi: hi),
        out_specs=pl.BlockSpec((2, tile_rows, LANES), lambda i, lo, hi, used: (0, i, 0)),
        scratch_shapes=[pltpu.VMEM((2, d, 2 * hid), BF16), pltpu.VMEM((2, hid, d), BF16)],
    )
    return pl.pallas_call(
        _ffn_body,
        grid_spec=grid_spec,
        out_shape=jax.ShapeDtypeStruct((2,) + xs.shape, F32),
        compiler_params=_params(("arbitrary",)),
        name="moe_experts",
    )(lo_expert, hi_expert, used, xs, w_gate, w_up, w_down, w_gate, w_up, w_down)


def _both_planes(ref, tok, n=1):
    return ref.at[:, pl.ds(pl.multiple_of(tok * SUBLANES, SUBLANES), n * SUBLANES)]


def _combine_body(pos_ref, ys_hbm, x1_ref, w_ref, gf_ref, o_ref, buf, sems, *, tm, final_norm):
    i = pl.program_id(0)
    slot = i % 2

    def gather(step, s):
        def issue(g, c):
            base = step * tm + g * ROW_UNROLL
            for j in range(ROW_UNROLL):
                src = _both_planes(ys_hbm, pos_ref[base + j])
                pltpu.make_async_copy(src, _both_planes(buf.at[s], g * ROW_UNROLL + j),
                                      sems.at[s]).start(priority=j % 2)
            return c

        lax.fori_loop(0, tm // ROW_UNROLL, issue, 0)

    @pl.when(i == 0)
    def _():
        gather(i, slot)

    @pl.when(i + 1 < pl.num_programs(0))
    def _():
        gather(i + 1, 1 - slot)

    pltpu.make_async_copy(_both_planes(ys_hbm, 0, tm), buf.at[slot], sems.at[slot]).wait()
    w = w_ref[...]
    y = (x1_ref[...] + w[:, 0:1] * _load_token_major(buf.at[slot, 0], tm)
         + w[:, 1:2] * _load_token_major(buf.at[slot, 1], tm))
    if final_norm:
        y = _rms(y, gf_ref[...])
    o_ref[...] = y


def _combine(ys, x1, pos, w2, gf, final_norm, tm=512):
    t, d = x1.shape
    grid_spec = pltpu.PrefetchScalarGridSpec(
        num_scalar_prefetch=1,
        grid=(t // tm,),
        in_specs=[pl.BlockSpec(memory_space=pl.ANY),
                  pl.BlockSpec((tm, d), lambda i, *_: (i, 0)),
                  pl.BlockSpec((tm, 2), lambda i, *_: (i, 0)),
                  pl.BlockSpec((1, d), lambda i, *_: (0, 0))],
        out_specs=pl.BlockSpec((tm, d), lambda i, *_: (i, 0)),
        scratch_shapes=[pltpu.VMEM((2, 2, tm * SUBLANES, LANES), F32), pltpu.SemaphoreType.DMA((2,))],
    )
    return pl.pallas_call(
        functools.partial(_combine_body, tm=tm, final_norm=final_norm),
        grid_spec=grid_spec,
        out_shape=jax.ShapeDtypeStruct((t, d), F32),
        compiler_params=_params(("arbitrary",)),
        name="moe_combine",
    )(pos, ys, x1, w2, gf.reshape(1, d))


def _layer(x, batch, seq, layer, norm1, w_in, a_ln_g, a_ln_b, a_ws, a_bs, ret_gn, w_out, norm2,
           router_g, router_gb, router_e, router_eb, w_gate, w_up, w_down, final_gain, is_last):
    t, d = x.shape
    a_groups = a_ws.shape[0]
    aw = a_groups * HEAD_DIM
    c_heads = ret_gn.shape[0] // HEAD_DIM
    cw = c_heads * HEAD_DIM
    bw = w_in.shape[2] - 2 * aw - 4 * cw
    b_heads = (bw // 3) // HEAD_DIM
    za, zqkv, zg = _inproj(x, norm1, w_in, layer, (2 * aw, bw + 3 * cw, cw), (F32, BF16, F32))
    zqkv3 = zqkv.reshape(batch, seq, zqkv.shape[1])
    a_out = _sgu(za, a_ln_g, a_ln_b, a_ws, a_bs)
    b_out = _attn(zqkv3, 0, b_heads).reshape(t, bw // 3)
    c_out = _retention(zqkv3, bw // cw, zg.reshape(batch, seq, cw), c_heads, ret_gn).reshape(t, cw)
    x1, route = _outproj(x, a_out, b_out, c_out, w_out, layer, norm2,
                         router_g, router_gb, router_e, router_eb)
    cls = route[0].astype(I32)
    w2 = route[1:3].T
    n_tiles = t // MOE_TILE + N_CLASSES
    pos, lo_expert, hi_expert, used, pad_start, pad_count = _moe_plan(cls, n_tiles)
    xs = _dispatch(x1, norm2, pos, n_tiles, pad_start, pad_count, used)
    ys = _ffn(xs, w_gate, w_up, w_down, layer, lo_expert, hi_expert, used)
    return _combine(ys, x1, pos, w2, final_gain, is_last)


def kernel(x, norm1, w_in, a_ln_g, a_ln_b, a_ws, a_bs, ret_gn, w_out, norm2, router_g, router_gb, router_e,
           router_eb, w_gate, w_up, w_down, final_norm):
    batch, seq, d = x.shape
    depth = norm1.shape[0]
    h = x.reshape(batch * seq, d)
    for l in range(depth):
        h = _layer(h, batch, seq, l, norm1[l], w_in, a_ln_g[l], a_ln_b[l], a_ws[l], a_bs[l], ret_gn[l],
                   w_out, norm2[l], router_g[l], router_gb[l], router_e[l], router_eb[l],
                   w_gate, w_up, w_down, final_norm, l == depth - 1)
    return h.reshape(batch, seq, d)
```

```python
import functools
import itertools

import numpy as np
import jax
import jax.numpy as jnp
from jax import lax
from jax.experimental import pallas as pl
from jax.experimental.pallas import tpu as pltpu

F32 = jnp.float32
BF16 = jnp.bfloat16
I32 = jnp.int32

EPS = 1e-6
HEAD_DIM = 64
SGU_CHUNK = 128
RET_CHUNK = 128
ATTN_BLOCK = 128
DILATIONS = (1, 4, 16)
ATTN_UNROLL = 8
ROPE_BASE = 10000.0
N_GROUPS = 4
EXPERTS_PER_GROUP = 4
N_EXPERTS = N_GROUPS * EXPERTS_PER_GROUP
_PAIRS = tuple(itertools.combinations(range(EXPERTS_PER_GROUP), 2))
N_CLASSES = N_GROUPS * len(_PAIRS)
LANES = 128
SUBLANES = 8
MOE_TILE = 256
ROW_UNROLL = 8
VMEM_LIMIT = 56 * 1024 * 1024


def _params(sem, vmem=VMEM_LIMIT):
    return pltpu.CompilerParams(dimension_semantics=sem, vmem_limit_bytes=vmem)


def _rms(x, g):
    return x * lax.rsqrt(jnp.mean(x * x, axis=-1, keepdims=True) + EPS) * g


def _split_bf16(a):
    hi = a.astype(BF16)
    lo = (a - hi.astype(F32)).astype(BF16)
    return hi, lo


def _cast_rows(dst_ref, src_ref, rows=256):
    for r0 in range(0, src_ref.shape[0], rows):
        dst_ref[r0:r0 + rows, :] = src_ref[r0:r0 + rows, :].astype(dst_ref.dtype)


def _gelu_tanh(x):
    return 0.5 * x * (1.0 + jnp.tanh(np.sqrt(2.0 / np.pi).astype(np.float32) * (x + 0.044715 * (x * x * x))))


def _spatial_gating(za, lng_ref, lnb_ref, wcat_ref, bias_ref, o_ref):
    ga = _gelu_tanh(za)
    aw = ga.shape[1] // 2
    u = ga[:, :aw]
    v = ga[:, aw:]
    mu = jnp.mean(v, axis=-1, keepdims=True)
    dv = v - mu
    var = jnp.mean(dv * dv, axis=-1, keepdims=True)
    vn = dv * lax.rsqrt(var + EPS) * lng_ref[...] + lnb_ref[...]
    group = lax.broadcasted_iota(I32, (1, aw), 1) // HEAD_DIM
    n_groups = aw // HEAD_DIM
    for c in range(ga.shape[0] // SGU_CHUNK):
        rows = slice(c * SGU_CHUNK, (c + 1) * SGU_CHUNK)
        vc = vn[rows]
        stack = jnp.concatenate([jnp.where(group == g, vc, 0.0) for g in range(n_groups)], axis=0)
        s = jnp.dot(wcat_ref[...], stack.astype(BF16), preferred_element_type=F32) + bias_ref[...]
        o_ref[rows, :] = (u[rows] * s).astype(o_ref.dtype)


def _inproj_body(x_ref, g_ref, w_ref, lng_ref, lnb_ref, wcat_ref, bias_ref, a_ref, qkv_ref, gate_ref, wb,
                 *, n_chunk):
    @pl.when(pl.program_id(0) == 0)
    def _():
        _cast_rows(wb, w_ref)

    hb = _rms(x_ref[...], g_ref[...]).astype(BF16)
    col = 2 * a_ref.shape[1]
    _spatial_gating(jnp.dot(hb, wb[:, 0:col], preferred_element_type=F32), lng_ref, lnb_ref, wcat_ref, bias_ref,
                    a_ref)
    for o_ref in (qkv_ref, gate_ref):
        for n0 in range(0, o_ref.shape[1], n_chunk):
            n1 = min(n0 + n_chunk, o_ref.shape[1])
            acc = jnp.dot(hb, wb[:, col + n0:col + n1], preferred_element_type=F32)
            o_ref[:, n0:n1] = acc.astype(o_ref.dtype)
        col += o_ref.shape[1]


def _layer_resident(stacked, layer):
    rest = stacked.shape[1:]
    return pl.BlockSpec((None,) + rest, lambda *_: (layer,) + (0,) * len(rest), pipeline_mode=pl.Buffered(1))


def _inproj(x, g, w_stack, layer, ln_g, ln_b, w_s, b_s, qkv_width, gate_width, tm=1024, n_chunk=512):
    t, d = x.shape
    n_groups = w_s.shape[0]
    aw = n_groups * HEAD_DIM
    assert 2 * aw + qkv_width + gate_width == w_stack.shape[2]
    causal = jnp.tril(jnp.ones((SGU_CHUNK, SGU_CHUNK), dtype=bool))
    wcat = jnp.where(causal[None], w_s, 0.0).transpose(1, 0, 2).reshape(SGU_CHUNK, n_groups * SGU_CHUNK)
    bias = jnp.repeat(b_s.T, HEAD_DIM, axis=1)
    full = lambda shape: pl.BlockSpec(shape, lambda i: (0, 0))
    widths = (aw, qkv_width, gate_width)
    dtypes = (BF16, BF16, F32)
    return pl.pallas_call(
        functools.partial(_inproj_body, n_chunk=n_chunk),
        grid=(t // tm,),
        in_specs=[pl.BlockSpec((tm, d), lambda i: (i, 0)), full((1, d)), _layer_resident(w_stack, layer),
                  full((1, aw)), full((1, aw)), full(wcat.shape), full(bias.shape)],
        out_specs=[pl.BlockSpec((tm, n), lambda i: (i, 0)) for n in widths],
        out_shape=[jax.ShapeDtypeStruct((t, n), dt) for n, dt in zip(widths, dtypes)],
        scratch_shapes=[pltpu.VMEM(w_stack.shape[1:], BF16)],
        compiler_params=_params(("arbitrary",)),
        name="inproj_sgu",
    )(x, g.reshape(1, d), w_stack, ln_g.reshape(1, aw), ln_b.reshape(1, aw), wcat.astype(BF16), bias)


def _attn_bias():
    blk = ATTN_BLOCK
    qi = np.arange(2 * blk)[:, None] % blk
    ci = np.arange(2 * blk)[None, :]
    first = (ci < blk) & (ci <= qi)
    later = np.where(ci < blk, ci >= qi, ci - blk <= qi)
    return np.where(np.stack([first, later]), 0.0, -np.inf).astype(np.float32)


def _attn_body(q_ref, k_ref, v_ref, bias_ref, o_ref, qkv_scr, ob_scr, m_scr, l_scr, *, seq, unroll):
    blk = ATTN_BLOCK
    head0 = lax.broadcasted_iota(I32, (1, LANES), 1) < HEAD_DIM
    scale = HEAD_DIM ** -0.5 * np.log2(np.e)

    widen_rows = 512

    def widen(i, carry):
        sl = pl.ds(pl.multiple_of(i * widen_rows, widen_rows), widen_rows)
        qkv_scr[0, 0, sl, :] = q_ref[0, sl, :].astype(F32) * scale
        qkv_scr[0, 1, sl, :] = k_ref[0, sl, :].astype(F32)
        qkv_scr[0, 2, sl, :] = v_ref[0, sl, :].astype(F32)
        return carry

    lax.fori_loop(0, seq // widen_rows, widen, 0)

    src = {DILATIONS[0]: qkv_scr.at[0]}
    for stage, d in enumerate(DILATIONS[1:], start=1):
        prev_d, ratio, slab = DILATIONS[stage - 1], d // DILATIONS[stage - 1], seq // d
        prev, cur = qkv_scr.at[stage - 1], qkv_scr.at[stage]
        for r_prev in range(prev_d):
            for c in range(ratio):
                r = r_prev + prev_d * c
                for a in range(3):
                    cur[a, r * slab:(r + 1) * slab, :] = prev[a, pl.ds(r_prev * (seq // prev_d) + c, slab,
                                                                     stride=ratio), :]
        src[d] = cur

    def rows(start, n, d):
        return pl.ds(start, n) if d == 1 else pl.ds(start, n, stride=d)

    def block(p, d, r, n):
        base = n * (blk * d) + r
        first = r * (seq // d) + n * blk
        kfirst = r * (seq // d) + jnp.maximum(n - 1, 0) * blk
        first, kfirst = pl.multiple_of(first, blk), pl.multiple_of(kfirst, blk)
        qb = src[d][0, pl.ds(first, blk), :]
        qs = jnp.concatenate([jnp.where(head0, qb, 0.0), jnp.where(head0, 0.0, qb)], axis=0).astype(BF16)
        kb = src[d][1, pl.ds(kfirst, 2 * blk), :].astype(BF16)
        vb = src[d][2, pl.ds(kfirst, 2 * blk), :].astype(BF16)
        va = jnp.concatenate([vb, jnp.ones_like(vb)], axis=1)
        s = lax.dot_general(qs, kb, (((1,), (1,)), ((), ())), preferred_element_type=F32)
        s = s + bias_ref[jnp.minimum(n, 1)]
        m = jnp.max(s, axis=-1, keepdims=True)
        e = jnp.exp2(s - m)
        oa = jnp.dot(e.astype(BF16), va, preferred_element_type=F32)
        ob_scr[p, rows(base, blk, d), :] = jnp.where(head0, oa[:blk, :LANES], oa[blk:, :LANES])
        l_scr[p, rows(base, blk, d), :] = jnp.where(head0, oa[:blk, LANES:], oa[blk:, LANES:])
        m_scr[p, rows(base, blk, d), :] = jnp.where(head0, m[:blk], m[blk:])

    for p, d in enumerate(DILATIONS):
        def group(i, carry, p=p, d=d):
            for u in range(unroll):
                b = i * unroll + u
                block(p, d, b & (d - 1), b >> (d.bit_length() - 1))
            return carry

        lax.fori_loop(0, seq // (blk * unroll), group, 0)

    step = 256

    def mix(i, carry):
        sl = pl.ds(pl.multiple_of(i * step, step), step)
        ms = [m_scr[p, sl, :] for p in range(len(DILATIONS))]
        top = functools.reduce(jnp.maximum, ms)
        es = [jnp.exp2(m - top) for m in ms]
        num = sum(e * ob_scr[p, sl, :] for p, e in enumerate(es))
        den = sum(e * l_scr[p, sl, :] for p, e in enumerate(es))
        o_ref[0, sl, :] = (num / den).astype(o_ref.dtype)
        return carry

    lax.fori_loop(0, seq // step, mix, 0)


def _attn(z3, q_col, n_heads):
    b, s, _ = z3.shape
    n_pairs = n_heads * HEAD_DIM // LANES
    assert s % (ATTN_BLOCK * ATTN_UNROLL) == 0 and s % (2 * ATTN_BLOCK * max(DILATIONS)) == 0
    assert all(d & (d - 1) == 0 for d in DILATIONS)
    bias = jnp.asarray(_attn_bias())

    def spec(off):
        return pl.BlockSpec((1, s, LANES), lambda i, j: (i, 0, off + j))

    return pl.pallas_call(
        functools.partial(_attn_body, seq=s, unroll=ATTN_UNROLL),
        grid=(b, n_pairs),
        in_specs=[spec(q_col), spec(q_col + n_pairs), spec(q_col + 2 * n_pairs),
                  pl.BlockSpec(bias.shape, lambda i, j: (0, 0, 0))],
        out_specs=pl.BlockSpec((1, s, LANES), lambda i, j: (i, 0, j)),
        out_shape=jax.ShapeDtypeStruct((b, s, n_pairs * LANES), BF16),
        scratch_shapes=[pltpu.VMEM((len(DILATIONS), 3, s, LANES), F32),
                        pltpu.VMEM((len(DILATIONS), s, LANES), F32),
                        pltpu.VMEM((len(DILATIONS), s, LANES), F32),
                        pltpu.VMEM((len(DILATIONS), s, LANES), F32)],
        compiler_params=_params(("parallel", "parallel")),
        name="dilated_attn",
    )(z3, z3, z3, bias)


def _ret_body(q_ref, k_ref, v_ref, g_ref, cos_ref, sa_ref, sb_ref, dec_ref, qdec_ref, kdec_ref, cdec_ref,
              avg_ref, gn_ref, o_ref, state):
    cw = q_ref.shape[2]
    n_heads = cw // HEAD_DIM
    head = lax.broadcasted_iota(I32, (1, cw), 1) // HEAD_DIM
    hr = lax.broadcasted_iota(I32, (cw, cw), 0) // HEAD_DIM
    hc = lax.broadcasted_iota(I32, (cw, cw), 1) // HEAD_DIM
    same_head = hr == hc

    @pl.when(pl.program_id(1) == 0)
    def _():
        state[...] = jnp.zeros_like(state)

    def rotary(x, cos, sa, sb):
        half = HEAD_DIM // 2
        return x * cos + pltpu.roll(x, half, 1) * sa + pltpu.roll(x, cw - half, 1) * sb

    def head_mean(a):
        hi, lo = _split_bf16(a)
        return (jnp.dot(hi, avg_ref[...], preferred_element_type=F32)
                + jnp.dot(lo, avg_ref[...], preferred_element_type=F32))

    for c in range(q_ref.shape[1] // RET_CHUNK):
        rows = slice(c * RET_CHUNK, (c + 1) * RET_CHUNK)
        cos, sa, sb = cos_ref[rows, :], sa_ref[rows, :], sb_ref[rows, :]
        qr = rotary(q_ref[0, rows, :].astype(F32), cos, sa, sb)
        kr = rotary(k_ref[0, rows, :].astype(F32), cos, sa, sb) * (HEAD_DIM ** -0.5)
        v = v_ref[0, rows, :]
        krb = kr.astype(BF16)
        inner = [lax.dot_general(jnp.where(head == h, qr, 0.0).astype(BF16), krb, (((1,), (1,)), ((), ())),
                                 preferred_element_type=F32) for h in range(n_heads)]
        inner = jnp.concatenate(inner, axis=1) * dec_ref[...]
        vstack = jnp.concatenate([jnp.where(head == h, v, 0.0) for h in range(n_heads)], axis=0)
        st = state[...]
        o = (jnp.dot(inner.astype(BF16), vstack.astype(BF16), preferred_element_type=F32)
             + jnp.dot((qr * qdec_ref[...]).astype(BF16), st.astype(BF16), preferred_element_type=F32))
        ktv = lax.dot_general((kr * kdec_ref[...]).astype(BF16), v.astype(BF16), (((0,), (0,)), ((), ())),
                              preferred_element_type=F32)
        state[...] = st * cdec_ref[...] + jnp.where(same_head, ktv, 0.0)
        mu = head_mean(o)
        dd = o - mu
        var = head_mean(dd * dd)
        on = dd * lax.rsqrt(var + EPS) * gn_ref[...]
        gate = g_ref[0, rows, :]
        o_ref[0, rows, :] = (gate * jax.nn.sigmoid(gate) * on).astype(o_ref.dtype)


def _ret_tables(seq, n_heads):
    half = HEAD_DIM // 2
    inv = ROPE_BASE ** (-jnp.arange(half, dtype=F32) / half)
    ang = jnp.arange(seq, dtype=F32)[:, None] * inv[None]
    cos, sin = jnp.cos(ang), jnp.sin(ang)
    zero = jnp.zeros_like(sin)
    tile = lambda a: jnp.tile(a, (1, n_heads))
    cos_t = tile(jnp.concatenate([cos, cos], axis=1))
    sa_t = tile(jnp.concatenate([zero, sin], axis=1))
    sb_t = tile(jnp.concatenate([-sin, zero], axis=1))
    log_g = jnp.log(1.0 - 2.0 ** (-5.0 - jnp.arange(n_heads, dtype=F32)))
    c = RET_CHUNK
    idx = jnp.arange(c)
    diff = idx[:, None] - idx[None, :]
    decay_in = jnp.where(diff >= 0, jnp.exp(log_g[:, None, None] * jnp.maximum(diff, 0)[None]), 0.0)
    dec = decay_in.transpose(1, 0, 2).reshape(c, n_heads * c)
    lane_head = jnp.repeat(jnp.arange(n_heads), HEAD_DIM)
    qdec = jnp.exp(log_g[lane_head][None, :] * (idx + 1)[:, None].astype(F32))
    kdec = jnp.exp(log_g[lane_head][None, :] * (c - 1 - idx)[:, None].astype(F32))
    same = lane_head[:, None] == lane_head[None, :]
    cdec = jnp.where(same, jnp.exp(log_g * c)[lane_head][:, None], 0.0)
    avg = jnp.where(same, 1.0 / HEAD_DIM, 0.0).astype(BF16)
    return cos_t, sa_t, sb_t, dec, qdec, kdec, cdec, avg


def _retention(z3, col, gate3, n_heads, gn, tc=1024):
    b, s, _ = z3.shape
    cw = n_heads * HEAD_DIM
    cos_t, sa_t, sb_t, dec, qdec, kdec, cdec, avg = _ret_tables(s, n_heads)

    def zspec(off):
        return pl.BlockSpec((1, tc, cw), lambda i, j: (i, j, off))

    tab = pl.BlockSpec((tc, cw), lambda i, j: (j, 0))
    full = lambda a: pl.BlockSpec(a.shape, lambda i, j: (0, 0))
    gn2 = gn.reshape(1, cw)
    return pl.pallas_call(
        _ret_body,
        grid=(b, s // tc),
        in_specs=[zspec(col), zspec(col + 1), zspec(col + 2), zspec(0), tab, tab, tab,
                  full(dec), full(qdec), full(kdec), full(cdec), full(avg), full(gn2)],
        out_specs=pl.BlockSpec((1, tc, cw), lambda i, j: (i, j, 0)),
        out_shape=jax.ShapeDtypeStruct((b, s, cw), BF16),
        scratch_shapes=[pltpu.VMEM((cw, cw), F32)],
        compiler_params=_params(("parallel", "arbitrary")),
        name="retention",
    )(z3, z3, z3, gate3, cos_t, sa_t, sb_t, dec, qdec, kdec, cdec, avg, gn2)


def _outproj_body(x_ref, a_ref, b_ref, c_ref, wo_ref, g2_ref, rhi_ref, rlo_ref, rb_ref, x1_ref, route_ref, wb):
    @pl.when(pl.program_id(0) == 0)
    def _():
        _cast_rows(wb, wo_ref)

    aw, bw = a_ref.shape[1], b_ref.shape[1]
    y = (jnp.dot(a_ref[...], wb[0:aw, :], preferred_element_type=F32)
         + jnp.dot(b_ref[...], wb[aw:aw + bw, :], preferred_element_type=F32)
         + jnp.dot(c_ref[...], wb[aw + bw:, :], preferred_element_type=F32))
    x1 = x_ref[...] + y
    x1_ref[...] = x1
    hi, lo = _split_bf16(_rms(x1, g2_ref[...]))
    nt = lambda r, h: lax.dot_general(r, h, (((1,), (1,)), ((), ())), preferred_element_type=F32)
    lt = nt(rhi_ref[...], hi) + nt(rhi_ref[...], lo) + nt(rlo_ref[...], hi) + rb_ref[...]
    row = lambda i: lt[i:i + 1, :]
    best, gi = row(0), jnp.zeros_like(row(0), dtype=I32)
    for i in range(1, N_GROUPS):
        up = row(i) > best
        best = jnp.where(up, row(i), best)
        gi = jnp.where(up, i, gi)
    g_w = 1.0 / sum(jnp.exp(row(i) - best) for i in range(N_GROUPS))
    el = []
    for j in range(EXPERTS_PER_GROUP):
        e = row(N_GROUPS + (N_GROUPS - 1) * EXPERTS_PER_GROUP + j)
        for g in range(N_GROUPS - 2, -1, -1):
            e = jnp.where(gi == g, row(N_GROUPS + g * EXPERTS_PER_GROUP + j), e)
        el.append(e)
    v1, i1 = el[0], jnp.zeros_like(gi)
    for j in range(1, EXPERTS_PER_GROUP):
        up = el[j] > v1
        v1 = jnp.where(up, el[j], v1)
        i1 = jnp.where(up, j, i1)
    v2, i2 = jnp.full_like(v1, -jnp.inf), jnp.zeros_like(gi)
    for j in range(EXPERTS_PER_GROUP):
        up = jnp.logical_and(i1 != j, el[j] > v2)
        v2 = jnp.where(up, el[j], v2)
        i2 = jnp.where(up, j, i2)
    e21 = jnp.exp(v2 - v1)
    w1 = g_w / (1.0 + e21)
    w2 = g_w * e21 / (1.0 + e21)
    swap = i2 < i1
    e_lo, e_hi = jnp.where(swap, i2, i1), jnp.where(swap, i1, i2)
    w_lo, w_hi = jnp.where(swap, w2, w1), jnp.where(swap, w1, w2)
    first_pair = jnp.zeros_like(e_lo)
    for lo in range(1, EXPERTS_PER_GROUP - 1):
        first_pair = jnp.where(e_lo >= lo, _PAIRS.index((lo, lo + 1)), first_pair)
    cls = (gi * len(_PAIRS) + first_pair + e_hi - e_lo - 1).astype(F32)
    zero = jnp.zeros_like(w1)
    route_ref[...] = jnp.concatenate([cls, w_lo, w_hi, zero, zero, zero, zero, zero], axis=0)


def _outproj(x, a, b, c, wo_stack, layer, g2, router_g, router_gb, router_e, router_eb, tm=512):
    t, d = x.shape
    n_logits = N_GROUPS + N_EXPERTS
    n_rows = -(-n_logits // 16) * 16
    r = jnp.pad(jnp.concatenate([router_g, router_e], axis=1).T, ((0, n_rows - n_logits), (0, 0)))
    rhi = r.astype(BF16)
    rlo = (r - rhi.astype(F32)).astype(BF16)
    rb = jnp.pad(jnp.concatenate([router_gb, router_eb]), (0, n_rows - n_logits)).reshape(n_rows, 1)
    row_spec = lambda w: pl.BlockSpec((tm, w), lambda i: (i, 0))
    full = lambda arr: pl.BlockSpec(arr.shape, lambda i: (0, 0))
    g2r = g2.reshape(1, d)
    return pl.pallas_call(
        _outproj_body,
        grid=(t // tm,),
        in_specs=[row_spec(d), row_spec(a.shape[1]), row_spec(b.shape[1]), row_spec(c.shape[1]),
                  _layer_resident(wo_stack, layer), full(g2r), full(rhi), full(rlo), full(rb)],
        out_specs=[row_spec(d), pl.BlockSpec((8, tm), lambda i: (0, i))],
        out_shape=[jax.ShapeDtypeStruct((t, d), F32), jax.ShapeDtypeStruct((8, t), F32)],
        scratch_shapes=[pltpu.VMEM(wo_stack.shape[1:], BF16)],
        compiler_params=_params(("arbitrary",)),
        name="outproj_route",
    )(x, a, b, c, wo_stack, g2r, rhi, rlo, rb)


def _moe_plan(cls, n_tiles):
    onehot = (cls[:, None] == jnp.arange(N_CLASSES, dtype=I32)[None, :]).astype(I32)
    csum = jnp.cumsum(onehot, axis=0)
    counts = csum[-1]
    padded = ((counts + MOE_TILE - 1) // MOE_TILE) * MOE_TILE
    ends = jnp.cumsum(padded)
    starts = ends - padded
    pos = jnp.sum(onehot * (csum - 1 + starts[None, :]), axis=1)
    tile_start = jnp.arange(n_tiles, dtype=I32) * MOE_TILE
    tile_cls = jnp.sum((ends[None, :] <= tile_start[:, None]).astype(I32), axis=1)
    tile_cls = jnp.minimum(tile_cls, N_CLASSES - 1)
    group, pair = np.divmod(np.arange(N_CLASSES), len(_PAIRS))
    pairs = np.asarray(_PAIRS)
    lo_expert = jnp.asarray(group * EXPERTS_PER_GROUP + pairs[pair, 0], I32)[tile_cls]
    hi_expert = jnp.asarray(group * EXPERTS_PER_GROUP + pairs[pair, 1], I32)[tile_cls]
    used = (ends[-1] // MOE_TILE).astype(I32).reshape(1)
    pad_start = (starts + counts).astype(I32)
    pad_count = (padded - counts).astype(I32)
    return pos.astype(I32), lo_expert, hi_expert, used, pad_start, pad_count


def _store_token_major(ref, x, tok0=0):
    n = x.shape[0]
    for c in range(SUBLANES):
        ref[pl.ds(tok0 * SUBLANES + c, n, stride=SUBLANES), :] = x[:, c * LANES:(c + 1) * LANES]


def _load_token_major(ref, n, tok0=0):
    return jnp.concatenate([ref[pl.ds(tok0 * SUBLANES + c, n, stride=SUBLANES), :] for c in range(SUBLANES)],
                           axis=1)


def _tokens(ref, tok, n=1):
    return ref.at[pl.ds(pl.multiple_of(tok * SUBLANES, SUBLANES), n * SUBLANES)]


def _dispatch_body(pad_start_ref, pad_count_ref, used_ref, pos_ref, x_ref, g2_ref, xs_hbm, xt, zeros, sem, fill_sem,
                   *, tm, n_tiles):
    i = pl.program_id(0)
    half_tile = MOE_TILE // 2

    def fill(act):
        def class_pad(e, c):
            off, n = pad_start_ref[e], pad_count_ref[e]
            bit = half_tile
            while bit:
                take = (n & bit) != 0

                @pl.when(take)
                def _(off=off, bit=bit):
                    act(pltpu.make_async_copy(_tokens(zeros, 0, bit), _tokens(xs_hbm, off, bit), fill_sem))

                off = off + jnp.where(take, bit, 0)
                bit //= 2
            return c

        lax.fori_loop(0, N_CLASSES, class_pad, 0)

        def unused_half_tile(j, c):
            act(pltpu.make_async_copy(zeros, _tokens(xs_hbm, j * half_tile, half_tile), fill_sem))
            return c

        lax.fori_loop(2 * used_ref[0], 2 * n_tiles, unused_half_tile, 0)

    @pl.when(i == 0)
    def _():
        zeros[...] = jnp.zeros_like(zeros)
        fill(lambda cp: cp.start())
        fill(lambda cp: cp.wait())

    _store_token_major(xt, _rms(x_ref[...], g2_ref[...]))

    def issue(g, c):
        base = i * tm + g * ROW_UNROLL
        for j in range(ROW_UNROLL):
            dst = _tokens(xs_hbm, pos_ref[base + j])
            pltpu.make_async_copy(_tokens(xt, g * ROW_UNROLL + j), dst, sem).start(priority=j % 2)
        return c

    lax.fori_loop(0, tm // ROW_UNROLL, issue, 0)
    pltpu.make_async_copy(xt, _tokens(xs_hbm, 0, tm), sem).wait()


def _dispatch(x1, g2, pos, n_tiles, pad_start, pad_count, used, tm=1024):
    t, d = x1.shape
    assert d == SUBLANES * LANES
    grid_spec = pltpu.PrefetchScalarGridSpec(
        num_scalar_prefetch=4,
        grid=(t // tm,),
        in_specs=[pl.BlockSpec((tm, d), lambda i, *_: (i, 0)),
                  pl.BlockSpec((1, d), lambda i, *_: (0, 0))],
        out_specs=pl.BlockSpec(memory_space=pl.ANY),
        scratch_shapes=[pltpu.VMEM((tm * SUBLANES, LANES), F32),
                        pltpu.VMEM((MOE_TILE // 2 * SUBLANES, LANES), F32),
                        pltpu.SemaphoreType.DMA, pltpu.SemaphoreType.DMA],
    )
    return pl.pallas_call(
        functools.partial(_dispatch_body, tm=tm, n_tiles=n_tiles),
        grid_spec=grid_spec,
        out_shape=jax.ShapeDtypeStruct((n_tiles * MOE_TILE * SUBLANES, LANES), F32),
        compiler_params=_params(("arbitrary",)),
        name="moe_dispatch",
    )(pad_start, pad_count, used, pos, x1, g2.reshape(1, d))


def _ffn_body(lo_ref, hi_ref, used_ref, xs_ref, *refs):
    w_refs, ys_ref, wgu_b, wd_b = refs[:6], refs[6], refs[7], refs[8]
    i = pl.program_id(0)
    live = i < used_ref[0]
    hid = wd_b.shape[1]
    prev = jnp.maximum(i - 1, 0)

    for k, e_ref in enumerate((lo_ref, hi_ref)):
        wg_ref, wu_ref, wd_ref = w_refs[3 * k:3 * k + 3]

        @pl.when(jnp.logical_and(live, jnp.logical_or(i == 0, e_ref[i] != e_ref[prev])))
        def _(k=k, wg_ref=wg_ref, wu_ref=wu_ref, wd_ref=wd_ref):
            _cast_rows(wgu_b.at[k, :, 0:hid], wg_ref)
            _cast_rows(wgu_b.at[k, :, hid:2 * hid], wu_ref)
            _cast_rows(wd_b.at[k], wd_ref)

    @pl.when(live)
    def _():
        hb = _load_token_major(xs_ref, MOE_TILE).astype(BF16)
        for k in range(2):
            gu = jnp.dot(hb, wgu_b[k], preferred_element_type=F32)
            gate, up = gu[:, :hid], gu[:, hid:]
            act = (gate * jax.nn.sigmoid(gate) * up).astype(BF16)
            _store_token_major(ys_ref.at[k], jnp.dot(act, wd_b[k], preferred_element_type=F32))

    @pl.when(jnp.logical_not(live))
    def _():
        ys_ref[...] = jnp.zeros_like(ys_ref)


def _ffn(xs, w_gate, w_up, w_down, layer, lo_expert, hi_expert, used):
    n_tiles = xs.shape[0] // (MOE_TILE * SUBLANES)
    d, hid = w_down.shape[3], w_down.shape[2]
    tile_rows = MOE_TILE * SUBLANES
    live = lambda i, used: jnp.minimum(i, used[0] - 1)

    def weights(pick):
        up = pl.BlockSpec((None, None, d, hid), lambda i, lo, hi, used: (layer, pick(lo, hi)[i], 0, 0))
        down = pl.BlockSpec((None, None, hid, d), lambda i, lo, hi, used: (layer, pick(lo, hi)[i], 0, 0))
        return [up, up, down]

    grid_spec = pltpu.PrefetchScalarGridSpec(
        num_scalar_prefetch=3,
        grid=(n_tiles,),
        in_specs=[pl.BlockSpec((tile_rows, LANES), lambda i, lo, hi, used: (live(i, used), 0))]
        + weights(lambda lo, hi: lo) + weights(lambda lo, hi: hi),
        out_specs=pl.BlockSpec((2, tile_rows, LANES), lambda i, lo, hi, used: (0, i, 0)),
        scratch_shapes=[pltpu.VMEM((2, d, 2 * hid), BF16), pltpu.VMEM((2, hid, d), BF16)],
    )
    return pl.pallas_call(
        _ffn_body,
        grid_spec=grid_spec,
        out_shape=jax.ShapeDtypeStruct((2,) + xs.shape, F32),
        compiler_params=_params(("arbitrary",)),
        name="moe_experts",
    )(lo_expert, hi_expert, used, xs, w_gate, w_up, w_down, w_gate, w_up, w_down)


def _both_planes(ref, tok, n=1):
    return ref.at[:, pl.ds(pl.multiple_of(tok * SUBLANES, SUBLANES), n * SUBLANES)]


def _combine_body(pos_ref, ys_hbm, x1_ref, w_ref, gf_ref, o_ref, buf, sems, *, tm, final_norm):
    i = pl.program_id(0)
    slot = i % 2

    def gather(step, s):
        def issue(g, c):
            base = step * tm + g * ROW_UNROLL
            for j in range(ROW_UNROLL):
                src = _both_planes(ys_hbm, pos_ref[base + j])
                pltpu.make_async_copy(src, _both_planes(buf.at[s], g * ROW_UNROLL + j),
                                      sems.at[s]).start(priority=j % 2)
            return c

        lax.fori_loop(0, tm // ROW_UNROLL, issue, 0)

    @pl.when(i == 0)
    def _():
        gather(i, slot)

    @pl.when(i + 1 < pl.num_programs(0))
    def _():
        gather(i + 1, 1 - slot)

    pltpu.make_async_copy(_both_planes(ys_hbm, 0, tm), buf.at[slot], sems.at[slot]).wait()
    w = w_ref[...]
    y = (x1_ref[...] + w[:, 0:1] * _load_token_major(buf.at[slot, 0], tm)
         + w[:, 1:2] * _load_token_major(buf.at[slot, 1], tm))
    if final_norm:
        y = _rms(y, gf_ref[...])
    o_ref[...] = y


def _combine(ys, x1, pos, w2, gf, final_norm, tm=512):
    t, d = x1.shape
    grid_spec = pltpu.PrefetchScalarGridSpec(
        num_scalar_prefetch=1,
        grid=(t // tm,),
        in_specs=[pl.BlockSpec(memory_space=pl.ANY),
                  pl.BlockSpec((tm, d), lambda i, *_: (i, 0)),
                  pl.BlockSpec((tm, 2), lambda i, *_: (i, 0)),
                  pl.BlockSpec((1, d), lambda i, *_: (0, 0))],
        out_specs=pl.BlockSpec((tm, d), lambda i, *_: (i, 0)),
        scratch_shapes=[pltpu.VMEM((2, 2, tm * SUBLANES, LANES), F32), pltpu.SemaphoreType.DMA((2,))],
    )
    return pl.pallas_call(
        functools.partial(_combine_body, tm=tm, final_norm=final_norm),
        grid_spec=grid_spec,
        out_shape=jax.ShapeDtypeStruct((t, d), F32),
        compiler_params=_params(("arbitrary",)),
        name="moe_combine",
    )(pos, ys, x1, w2, gf.reshape(1, d))


def _layer(x, batch, seq, layer, norm1, w_in, a_ln_g, a_ln_b, a_ws, a_bs, ret_gn, w_out, norm2,
           router_g, router_gb, router_e, router_eb, w_gate, w_up, w_down, final_gain, is_last):
    t, d = x.shape
    a_groups = a_ws.shape[0]
    aw = a_groups * HEAD_DIM
    c_heads = ret_gn.shape[0] // HEAD_DIM
    cw = c_heads * HEAD_DIM
    bw = w_in.shape[2] - 2 * aw - 4 * cw
    b_heads = (bw // 3) // HEAD_DIM
    a_out, zqkv, zg = _inproj(x, norm1, w_in, layer, a_ln_g, a_ln_b, a_ws, a_bs, bw + 3 * cw, cw)
    zqkv3 = zqkv.reshape(batch, seq, zqkv.shape[1])
    b_out = _attn(zqkv3, 0, b_heads).reshape(t, bw // 3)
    c_out = _retention(zqkv3, bw // cw, zg.reshape(batch, seq, cw), c_heads, ret_gn).reshape(t, cw)
    x1, route = _outproj(x, a_out, b_out, c_out, w_out, layer, norm2,
                         router_g, router_gb, router_e, router_eb)
    cls = route[0].astype(I32)
    w2 = route[1:3].T
    n_tiles = t // MOE_TILE + N_CLASSES
    pos, lo_expert, hi_expert, used, pad_start, pad_count = _moe_plan(cls, n_tiles)
    xs = _dispatch(x1, norm2, pos, n_tiles, pad_start, pad_count, used)
    ys = _ffn(xs, w_gate, w_up, w_down, layer, lo_expert, hi_expert, used)
    return _combine(ys, x1, pos, w2, final_gain, is_last)


def kernel(x, norm1, w_in, a_ln_g, a_ln_b, a_ws, a_bs, ret_gn, w_out, norm2, router_g, router_gb, router_e,
           router_eb, w_gate, w_up, w_down, final_norm):
    batch, seq, d = x.shape
    depth = norm1.shape[0]
    h = x.reshape(batch * seq, d)
    for l in range(depth):
        h = _layer(h, batch, seq, l, norm1[l], w_in, a_ln_g[l], a_ln_b[l], a_ws[l], a_bs[l], ret_gn[l],
                   w_out, norm2[l], router_g[l], router_gb[l], router_e[l], router_eb[l],
                   w_gate, w_up, w_down, final_norm, l == depth - 1)
    return h.reshape(batch, seq, d)
```

```python
import functools
import itertools

import numpy as np
import jax
import jax.numpy as jnp
from jax import lax
from jax.experimental import pallas as pl
from jax.experimental.pallas import tpu as pltpu

F32 = jnp.float32
BF16 = jnp.bfloat16
I32 = jnp.int32

EPS = 1e-6
HEAD_DIM = 64
SGU_CHUNK = 128
RET_CHUNK = 128
ATTN_BLOCK = 128
DILATIONS = (1, 4, 16)
ATTN_UNROLL = 32
ATTN_SLAB = 256
ROPE_BASE = 10000.0
N_GROUPS = 4
EXPERTS_PER_GROUP = 4
N_EXPERTS = N_GROUPS * EXPERTS_PER_GROUP
_PAIRS = tuple(itertools.combinations(range(EXPERTS_PER_GROUP), 2))
N_CLASSES = N_GROUPS * len(_PAIRS)
LANES = 128
SUBLANES = 8
PACKED_ROWS = 16
MOE_TILE = 256
ROW_UNROLL = 8
CAST_SLAB = 256
VMEM_LIMIT = 56 * 1024 * 1024


def _params(sem, vmem=VMEM_LIMIT):
    return pltpu.CompilerParams(dimension_semantics=sem, vmem_limit_bytes=vmem)


def _rms(x, g):
    return x * lax.rsqrt(jnp.mean(x * x, axis=-1, keepdims=True) + EPS) * g


def _split_bf16(a):
    hi = a.astype(BF16)
    lo = (a - hi.astype(F32)).astype(BF16)
    return hi, lo


def _cast_rows(dst_ref, src_ref, rows=CAST_SLAB):
    for r0 in range(0, src_ref.shape[0], rows):
        dst_ref[r0:r0 + rows, :] = src_ref[r0:r0 + rows, :].astype(dst_ref.dtype)


def _gelu_tanh(x):
    return 0.5 * x * (1.0 + jnp.tanh(np.sqrt(2.0 / np.pi).astype(np.float32) * (x + 0.044715 * (x * x * x))))


def _spatial_gating(za, lng_ref, lnb_ref, wcat_ref, bias_ref, o_ref):
    ga = _gelu_tanh(za)
    aw = ga.shape[1] // 2
    u = ga[:, :aw]
    v = ga[:, aw:]
    mu = jnp.mean(v, axis=-1, keepdims=True)
    dv = v - mu
    var = jnp.mean(dv * dv, axis=-1, keepdims=True)
    vn = dv * lax.rsqrt(var + EPS) * lng_ref[...] + lnb_ref[...]
    group = lax.broadcasted_iota(I32, (1, aw), 1) // HEAD_DIM
    n_groups = aw // HEAD_DIM
    for c in range(ga.shape[0] // SGU_CHUNK):
        rows = slice(c * SGU_CHUNK, (c + 1) * SGU_CHUNK)
        vc = vn[rows]
        stack = jnp.concatenate([jnp.where(group == g, vc, 0.0) for g in range(n_groups)], axis=0)
        s = jnp.dot(wcat_ref[...], stack.astype(BF16), preferred_element_type=F32) + bias_ref[...]
        o_ref[rows, :] = (u[rows] * s).astype(o_ref.dtype)


def _inproj_body(x_ref, g_ref, w_ref, lng_ref, lnb_ref, wcat_ref, bias_ref, a_ref, qkv_ref, gate_ref, wb,
                 *, n_chunk):
    @pl.when(pl.program_id(0) == 0)
    def _():
        _cast_rows(wb, w_ref)

    hb = _rms(x_ref[...], g_ref[...]).astype(BF16)
    col = 2 * a_ref.shape[1]
    _spatial_gating(jnp.dot(hb, wb[:, 0:col], preferred_element_type=F32), lng_ref, lnb_ref, wcat_ref, bias_ref,
                    a_ref)
    for o_ref in (qkv_ref, gate_ref):
        for n0 in range(0, o_ref.shape[1], n_chunk):
            n1 = min(n0 + n_chunk, o_ref.shape[1])
            acc = jnp.dot(hb, wb[:, col + n0:col + n1], preferred_element_type=F32)
            o_ref[:, n0:n1] = acc.astype(o_ref.dtype)
        col += o_ref.shape[1]


def _layer_resident(stacked, layer):
    rest = stacked.shape[1:]
    return pl.BlockSpec((None,) + rest, lambda *_: (layer,) + (0,) * len(rest), pipeline_mode=pl.Buffered(1))


def _inproj(x, g, w_stack, layer, ln_g, ln_b, w_s, b_s, qkv_width, gate_width, tm=1024, n_chunk=512):
    t, d = x.shape
    n_groups = w_s.shape[0]
    aw = n_groups * HEAD_DIM
    assert 2 * aw + qkv_width + gate_width == w_stack.shape[2]
    causal = jnp.tril(jnp.ones((SGU_CHUNK, SGU_CHUNK), dtype=bool))
    wcat = jnp.where(causal[None], w_s, 0.0).transpose(1, 0, 2).reshape(SGU_CHUNK, n_groups * SGU_CHUNK)
    bias = jnp.repeat(b_s.T, HEAD_DIM, axis=1)
    full = lambda shape: pl.BlockSpec(shape, lambda i: (0, 0))
    widths = (aw, qkv_width, gate_width)
    dtypes = (BF16, BF16, F32)
    return pl.pallas_call(
        functools.partial(_inproj_body, n_chunk=n_chunk),
        grid=(t // tm,),
        in_specs=[pl.BlockSpec((tm, d), lambda i: (i, 0)), full((1, d)), _layer_resident(w_stack, layer),
                  full((1, aw)), full((1, aw)), full(wcat.shape), full(bias.shape)],
        out_specs=[pl.BlockSpec((tm, n), lambda i: (i, 0)) for n in widths],
        out_shape=[jax.ShapeDtypeStruct((t, n), dt) for n, dt in zip(widths, dtypes)],
        scratch_shapes=[pltpu.VMEM(w_stack.shape[1:], BF16)],
        compiler_params=_params(("arbitrary",)),
        name="inproj_sgu",
    )(x, g.reshape(1, d), w_stack, ln_g.reshape(1, aw), ln_b.reshape(1, aw), wcat.astype(BF16), bias)


def _attn_bias():
    blk = ATTN_BLOCK
    qi = np.arange(2 * blk)[:, None] % blk
    ci = np.arange(2 * blk)[None, :]
    first = (ci < blk) & (ci <= qi)
    later = np.where(ci < blk, ci >= qi, ci - blk <= qi)
    return np.where(np.stack([first, later]), 0.0, -np.inf).astype(np.float32)


def _attn_body(q_ref, k_ref, v_ref, bias_ref, o_ref, qkv_scr, ob_scr, m_scr, l_scr, *, seq, unroll):
    blk = ATTN_BLOCK
    head0 = lax.broadcasted_iota(I32, (1, LANES), 1) < HEAD_DIM
    scale = HEAD_DIM ** -0.5 * np.log2(np.e)

    widen_rows = ATTN_SLAB

    def widen(i, carry):
        sl = pl.ds(pl.multiple_of(i * widen_rows, widen_rows), widen_rows)
        qkv_scr[0, 0, sl, :] = q_ref[0, sl, :].astype(F32) * scale
        qkv_scr[0, 1, sl, :] = k_ref[0, sl, :].astype(F32)
        qkv_scr[0, 2, sl, :] = v_ref[0, sl, :].astype(F32)
        return carry

    lax.fori_loop(0, seq // widen_rows, widen, 0)

    src = {DILATIONS[0]: qkv_scr.at[0]}
    for stage, d in enumerate(DILATIONS[1:], start=1):
        prev_d, ratio, slab = DILATIONS[stage - 1], d // DILATIONS[stage - 1], seq // d
        prev, cur = qkv_scr.at[stage - 1], qkv_scr.at[stage]
        for r_prev in range(prev_d):
            for c in range(ratio):
                r = r_prev + prev_d * c
                for a in range(3):
                    cur[a, r * slab:(r + 1) * slab, :] = prev[a, pl.ds(r_prev * (seq // prev_d) + c, slab,
                                                                     stride=ratio), :]
        src[d] = cur

    def rows(start, n, d):
        return pl.ds(start, n) if d == 1 else pl.ds(start, n, stride=d)

    def block(p, d, r, n):
        base = n * (blk * d) + r
        first = r * (seq // d) + n * blk
        kfirst = r * (seq // d) + jnp.maximum(n - 1, 0) * blk
        first, kfirst = pl.multiple_of(first, blk), pl.multiple_of(kfirst, blk)
        qb = src[d][0, pl.ds(first, blk), :]
        qs = jnp.concatenate([jnp.where(head0, qb, 0.0), jnp.where(head0, 0.0, qb)], axis=0).astype(BF16)
        kb = src[d][1, pl.ds(kfirst, 2 * blk), :].astype(BF16)
        vb = src[d][2, pl.ds(kfirst, 2 * blk), :].astype(BF16)
        va = jnp.concatenate([vb, jnp.ones_like(vb)], axis=1)
        s = lax.dot_general(qs, kb, (((1,), (1,)), ((), ())), preferred_element_type=F32)
        s = s + bias_ref[jnp.minimum(n, 1)]
        m = jnp.max(s, axis=-1, keepdims=True)
        e = jnp.exp2(s - m)
        oa = jnp.dot(e.astype(BF16), va, preferred_element_type=F32)
        ob_scr[p, rows(base, blk, d), :] = jnp.where(head0, oa[:blk, :LANES], oa[blk:, :LANES])
        l_scr[p, rows(base, blk, d), :] = jnp.where(head0, oa[:blk, LANES:], oa[blk:, LANES:])
        m_scr[p, rows(base, blk, d), :] = jnp.where(head0, m[:blk], m[blk:])

    for p, d in enumerate(DILATIONS):
        def group(i, carry, p=p, d=d):
            for u in range(unroll):
                b = i * unroll + u
                block(p, d, b & (d - 1), b >> (d.bit_length() - 1))
            return carry

        lax.fori_loop(0, seq // (blk * unroll), group, 0)

    step = ATTN_SLAB

    def mix(i, carry):
        sl = pl.ds(pl.multiple_of(i * step, step), step)
        ms = [m_scr[p, sl, :] for p in range(len(DILATIONS))]
        top = functools.reduce(jnp.maximum, ms)
        es = [jnp.exp2(m - top) for m in ms]
        num = sum(e * ob_scr[p, sl, :] for p, e in enumerate(es))
        den = sum(e * l_scr[p, sl, :] for p, e in enumerate(es))
        o_ref[0, sl, :] = (num / den).astype(o_ref.dtype)
        return carry

    lax.fori_loop(0, seq // step, mix, 0)


def _attn(z3, q_col, n_heads):
    b, s, _ = z3.shape
    n_pairs = n_heads * HEAD_DIM // LANES
    assert s % (ATTN_BLOCK * ATTN_UNROLL) == 0 and s % (2 * ATTN_BLOCK * max(DILATIONS)) == 0
    assert all(d & (d - 1) == 0 for d in DILATIONS)
    bias = jnp.asarray(_attn_bias())

    def spec(off):
        return pl.BlockSpec((1, s, LANES), lambda i, j: (i, 0, off + j))

    return pl.pallas_call(
        functools.partial(_attn_body, seq=s, unroll=ATTN_UNROLL),
        grid=(b, n_pairs),
        in_specs=[spec(q_col), spec(q_col + n_pairs), spec(q_col + 2 * n_pairs),
                  pl.BlockSpec(bias.shape, lambda i, j: (0, 0, 0))],
        out_specs=pl.BlockSpec((1, s, LANES), lambda i, j: (i, 0, j)),
        out_shape=jax.ShapeDtypeStruct((b, s, n_pairs * LANES), BF16),
        scratch_shapes=[pltpu.VMEM((len(DILATIONS), 3, s, LANES), F32),
                        pltpu.VMEM((len(DILATIONS), s, LANES), F32),
                        pltpu.VMEM((len(DILATIONS), s, LANES), F32),
                        pltpu.VMEM((len(DILATIONS), s, LANES), F32)],
        compiler_params=_params(("parallel", "parallel")),
        name="dilated_attn",
    )(z3, z3, z3, bias)


def _ret_body(q_ref, k_ref, v_ref, g_ref, cos_ref, sa_ref, sb_ref, dec_ref, qdec_ref, kdec_ref, cdec_ref,
              avg_ref, gn_ref, o_ref, state):
    cw = q_ref.shape[2]
    n_heads = cw // HEAD_DIM
    head = lax.broadcasted_iota(I32, (1, cw), 1) // HEAD_DIM
    hr = lax.broadcasted_iota(I32, (cw, cw), 0) // HEAD_DIM
    hc = lax.broadcasted_iota(I32, (cw, cw), 1) // HEAD_DIM
    same_head = hr == hc

    @pl.when(pl.program_id(1) == 0)
    def _():
        state[...] = jnp.zeros_like(state)

    def rotary(x, cos, sa, sb):
        half = HEAD_DIM // 2
        return x * cos + pltpu.roll(x, half, 1) * sa + pltpu.roll(x, cw - half, 1) * sb

    def head_mean(a):
        hi, lo = _split_bf16(a)
        return (jnp.dot(hi, avg_ref[...], preferred_element_type=F32)
                + jnp.dot(lo, avg_ref[...], preferred_element_type=F32))

    for c in range(q_ref.shape[1] // RET_CHUNK):
        rows = slice(c * RET_CHUNK, (c + 1) * RET_CHUNK)
        cos, sa, sb = cos_ref[rows, :], sa_ref[rows, :], sb_ref[rows, :]
        qr = rotary(q_ref[0, rows, :].astype(F32), cos, sa, sb)
        kr = rotary(k_ref[0, rows, :].astype(F32), cos, sa, sb) * (HEAD_DIM ** -0.5)
        v = v_ref[0, rows, :].astype(F32)
        krb = kr.astype(BF16)
        inner = [lax.dot_general(jnp.where(head == h, qr, 0.0).astype(BF16), krb, (((1,), (1,)), ((), ())),
                                 preferred_element_type=F32) for h in range(n_heads)]
        inner = jnp.concatenate(inner, axis=1) * dec_ref[...]
        vstack = jnp.concatenate([jnp.where(head == h, v, 0.0) for h in range(n_heads)], axis=0)
        st = state[...]
        o = (jnp.dot(inner.astype(BF16), vstack.astype(BF16), preferred_element_type=F32)
             + jnp.dot((qr * qdec_ref[...]).astype(BF16), st.astype(BF16), preferred_element_type=F32))
        ktv = lax.dot_general((kr * kdec_ref[...]).astype(BF16), v.astype(BF16), (((0,), (0,)), ((), ())),
                              preferred_element_type=F32)
        state[...] = st * cdec_ref[...] + jnp.where(same_head, ktv, 0.0)
        mu = head_mean(o)
        dd = o - mu
        var = head_mean(dd * dd)
        on = dd * lax.rsqrt(var + EPS) * gn_ref[...]
        gate = g_ref[0, rows, :]
        o_ref[0, rows, :] = (gate * jax.nn.sigmoid(gate) * on).astype(o_ref.dtype)


def _ret_tables(seq, n_heads):
    half = HEAD_DIM // 2
    inv = ROPE_BASE ** (-jnp.arange(half, dtype=F32) / half)
    ang = jnp.arange(seq, dtype=F32)[:, None] * inv[None]
    cos, sin = jnp.cos(ang), jnp.sin(ang)
    zero = jnp.zeros_like(sin)
    tile = lambda a: jnp.tile(a, (1, n_heads))
    cos_t = tile(jnp.concatenate([cos, cos], axis=1))
    sa_t = tile(jnp.concatenate([zero, sin], axis=1))
    sb_t = tile(jnp.concatenate([-sin, zero], axis=1))
    log_g = jnp.log(1.0 - 2.0 ** (-5.0 - jnp.arange(n_heads, dtype=F32)))
    c = RET_CHUNK
    idx = jnp.arange(c)
    diff = idx[:, None] - idx[None, :]
    decay_in = jnp.where(diff >= 0, jnp.exp(log_g[:, None, None] * jnp.maximum(diff, 0)[None]), 0.0)
    dec = decay_in.transpose(1, 0, 2).reshape(c, n_heads * c)
    lane_head = jnp.repeat(jnp.arange(n_heads), HEAD_DIM)
    qdec = jnp.exp(log_g[lane_head][None, :] * (idx + 1)[:, None].astype(F32))
    kdec = jnp.exp(log_g[lane_head][None, :] * (c - 1 - idx)[:, None].astype(F32))
    same = lane_head[:, None] == lane_head[None, :]
    cdec = jnp.where(same, jnp.exp(log_g * c)[lane_head][:, None], 0.0)
    avg = jnp.where(same, 1.0 / HEAD_DIM, 0.0).astype(BF16)
    return cos_t, sa_t, sb_t, dec, qdec, kdec, cdec, avg


def _retention(z3, col, gate3, n_heads, gn, tc=2048):
    b, s, _ = z3.shape
    cw = n_heads * HEAD_DIM
    cos_t, sa_t, sb_t, dec, qdec, kdec, cdec, avg = _ret_tables(s, n_heads)

    def zspec(off):
        return pl.BlockSpec((1, tc, cw), lambda i, j: (i, j, off))

    tab = pl.BlockSpec((tc, cw), lambda i, j: (j, 0))
    full = lambda a: pl.BlockSpec(a.shape, lambda i, j: (0, 0))
    gn2 = gn.reshape(1, cw)
    return pl.pallas_call(
        _ret_body,
        grid=(b, s // tc),
        in_specs=[zspec(col), zspec(col + 1), zspec(col + 2), zspec(0), tab, tab, tab,
                  full(dec), full(qdec), full(kdec), full(cdec), full(avg), full(gn2)],
        out_specs=pl.BlockSpec((1, tc, cw), lambda i, j: (i, j, 0)),
        out_shape=jax.ShapeDtypeStruct((b, s, cw), BF16),
        scratch_shapes=[pltpu.VMEM((cw, cw), F32)],
        compiler_params=_params(("parallel", "arbitrary")),
        name="retention",
    )(z3, z3, z3, gate3, cos_t, sa_t, sb_t, dec, qdec, kdec, cdec, avg, gn2)


def _outproj_body(x_ref, a_ref, b_ref, c_ref, wo_ref, g2_ref, rhi_ref, rlo_ref, rb_ref, x1_ref, route_ref, wb):
    @pl.when(pl.program_id(0) == 0)
    def _():
        _cast_rows(wb, wo_ref)

    aw, bw = a_ref.shape[1], b_ref.shape[1]
    y = (jnp.dot(a_ref[...], wb[0:aw, :], preferred_element_type=F32)
         + jnp.dot(b_ref[...], wb[aw:aw + bw, :], preferred_element_type=F32)
         + jnp.dot(c_ref[...], wb[aw + bw:, :], preferred_element_type=F32))
    x1 = x_ref[...] + y
    x1_ref[...] = x1
    hi, lo = _split_bf16(_rms(x1, g2_ref[...]))
    nt = lambda r, h: lax.dot_general(r, h, (((1,), (1,)), ((), ())), preferred_element_type=F32)
    lt = nt(rhi_ref[...], hi) + nt(rhi_ref[...], lo) + nt(rlo_ref[...], hi) + rb_ref[...]
    row = lambda i: lt[i:i + 1, :]
    best, gi = row(0), jnp.zeros_like(row(0), dtype=I32)
    for i in range(1, N_GROUPS):
        up = row(i) > best
        best = jnp.where(up, row(i), best)
        gi = jnp.where(up, i, gi)
    g_w = 1.0 / sum(jnp.exp(row(i) - best) for i in range(N_GROUPS))
    el = []
    for j in range(EXPERTS_PER_GROUP):
        e = row(N_GROUPS + (N_GROUPS - 1) * EXPERTS_PER_GROUP + j)
        for g in range(N_GROUPS - 2, -1, -1):
            e = jnp.where(gi == g, row(N_GROUPS + g * EXPERTS_PER_GROUP + j), e)
        el.append(e)
    v1, i1 = el[0], jnp.zeros_like(gi)
    for j in range(1, EXPERTS_PER_GROUP):
        up = el[j] > v1
        v1 = jnp.where(up, el[j], v1)
        i1 = jnp.where(up, j, i1)
    v2, i2 = jnp.full_like(v1, -jnp.inf), jnp.zeros_like(gi)
    for j in range(EXPERTS_PER_GROUP):
        up = jnp.logical_and(i1 != j, el[j] > v2)
        v2 = jnp.where(up, el[j], v2)
        i2 = jnp.where(up, j, i2)
    e21 = jnp.exp(v2 - v1)
    w1 = g_w / (1.0 + e21)
    w2 = g_w * e21 / (1.0 + e21)
    swap = i2 < i1
    e_lo, e_hi = jnp.where(swap, i2, i1), jnp.where(swap, i1, i2)
    w_lo, w_hi = jnp.where(swap, w2, w1), jnp.where(swap, w1, w2)
    first_pair = jnp.zeros_like(e_lo)
    for lo in range(1, EXPERTS_PER_GROUP - 1):
        first_pair = jnp.where(e_lo >= lo, _PAIRS.index((lo, lo + 1)), first_pair)
    cls = (gi * len(_PAIRS) + first_pair + e_hi - e_lo - 1).astype(F32)
    zero = jnp.zeros_like(w1)
    route_ref[...] = jnp.concatenate([cls, w_lo, w_hi, zero, zero, zero, zero, zero], axis=0)


def _outproj(x, a, b, c, wo_stack, layer, g2, router_g, router_gb, router_e, router_eb, tm=512):
    t, d = x.shape
    n_logits = N_GROUPS + N_EXPERTS
    n_rows = -(-n_logits // PACKED_ROWS) * PACKED_ROWS
    r = jnp.pad(jnp.concatenate([router_g, router_e], axis=1).T, ((0, n_rows - n_logits), (0, 0)))
    rhi = r.astype(BF16)
    rlo = (r - rhi.astype(F32)).astype(BF16)
    rb = jnp.pad(jnp.concatenate([router_gb, router_eb]), (0, n_rows - n_logits)).reshape(n_rows, 1)
    row_spec = lambda w: pl.BlockSpec((tm, w), lambda i: (i, 0))
    full = lambda arr: pl.BlockSpec(arr.shape, lambda i: (0, 0))
    g2r = g2.reshape(1, d)
    return pl.pallas_call(
        _outproj_body,
        grid=(t // tm,),
        in_specs=[row_spec(d), row_spec(a.shape[1]), row_spec(b.shape[1]), row_spec(c.shape[1]),
                  _layer_resident(wo_stack, layer), full(g2r), full(rhi), full(rlo), full(rb)],
        out_specs=[row_spec(d), pl.BlockSpec((8, tm), lambda i: (0, i))],
        out_shape=[jax.ShapeDtypeStruct((t, d), F32), jax.ShapeDtypeStruct((8, t), F32)],
        scratch_shapes=[pltpu.VMEM(wo_stack.shape[1:], BF16)],
        compiler_params=_params(("arbitrary",)),
        name="outproj_route",
    )(x, a, b, c, wo_stack, g2r, rhi, rlo, rb)


def _moe_plan(cls, n_tiles):
    onehot = (cls[:, None] == jnp.arange(N_CLASSES, dtype=I32)[None, :]).astype(I32)
    csum = jnp.cumsum(onehot, axis=0)
    counts = csum[-1]
    padded = ((counts + MOE_TILE - 1) // MOE_TILE) * MOE_TILE
    ends = jnp.cumsum(padded)
    starts = ends - padded
    pos = jnp.sum(onehot * (csum - 1 + starts[None, :]), axis=1)
    tile_start = jnp.arange(n_tiles, dtype=I32) * MOE_TILE
    tile_cls = jnp.sum((ends[None, :] <= tile_start[:, None]).astype(I32), axis=1)
    tile_cls = jnp.minimum(tile_cls, N_CLASSES - 1)
    group, pair = np.divmod(np.arange(N_CLASSES), len(_PAIRS))
    pairs = np.asarray(_PAIRS)
    lo_expert = jnp.asarray(group * EXPERTS_PER_GROUP + pairs[pair, 0], I32)[tile_cls]
    hi_expert = jnp.asarray(group * EXPERTS_PER_GROUP + pairs[pair, 1], I32)[tile_cls]
    used = (ends[-1] // MOE_TILE).astype(I32).reshape(1)
    pad_start = (starts + counts).astype(I32)
    pad_count = (padded - counts).astype(I32)
    return pos.astype(I32), lo_expert, hi_expert, used, pad_start, pad_count


def _store_token_major(ref, x, tok0=0):
    n = x.shape[0]
    for c in range(SUBLANES):
        ref[pl.ds(tok0 * SUBLANES + c, n, stride=SUBLANES), :] = x[:, c * LANES:(c + 1) * LANES]


def _load_token_major(ref, n, tok0=0):
    return jnp.concatenate([ref[pl.ds(tok0 * SUBLANES + c, n, stride=SUBLANES), :] for c in range(SUBLANES)],
                           axis=1)


def _tokens(ref, tok, n=1):
    return ref.at[pl.ds(pl.multiple_of(tok * SUBLANES, SUBLANES), n * SUBLANES)]


def _dispatch_body(pad_start_ref, pad_count_ref, used_ref, pos_ref, x_ref, g2_ref, xs_hbm, xt, zeros, sem, fill_sem,
                   *, tm, n_tiles):
    i = pl.program_id(0)
    half_tile = MOE_TILE // 2

    def fill(act):
        def class_pad(e, c):
            off, n = pad_start_ref[e], pad_count_ref[e]
            bit = half_tile
            while bit:
                take = (n & bit) != 0

                @pl.when(take)
                def _(off=off, bit=bit):
                    act(pltpu.make_async_copy(_tokens(zeros, 0, bit), _tokens(xs_hbm, off, bit), fill_sem))

                off = off + jnp.where(take, bit, 0)
                bit //= 2
            return c

        lax.fori_loop(0, N_CLASSES, class_pad, 0)

        def unused_half_tile(j, c):
            act(pltpu.make_async_copy(zeros, _tokens(xs_hbm, j * half_tile, half_tile), fill_sem))
            return c

        lax.fori_loop(2 * used_ref[0], 2 * n_tiles, unused_half_tile, 0)

    @pl.when(i == 0)
    def _():
        zeros[...] = jnp.zeros_like(zeros)
        fill(lambda cp: cp.start())
        fill(lambda cp: cp.wait())

    _store_token_major(xt, _rms(x_ref[...], g2_ref[...]))

    def issue(g, c):
        base = i * tm + g * ROW_UNROLL
        for j in range(ROW_UNROLL):
            dst = _tokens(xs_hbm, pos_ref[base + j])
            pltpu.make_async_copy(_tokens(xt, g * ROW_UNROLL + j), dst, sem).start(priority=j % 2)
        return c

    lax.fori_loop(0, tm // ROW_UNROLL, issue, 0)
    pltpu.make_async_copy(xt, _tokens(xs_hbm, 0, tm), sem).wait()


def _dispatch(x1, g2, pos, n_tiles, pad_start, pad_count, used, tm=1024):
    t, d = x1.shape
    assert d == SUBLANES * LANES
    grid_spec = pltpu.PrefetchScalarGridSpec(
        num_scalar_prefetch=4,
        grid=(t // tm,),
        in_specs=[pl.BlockSpec((tm, d), lambda i, *_: (i, 0)),
                  pl.BlockSpec((1, d), lambda i, *_: (0, 0))],
        out_specs=pl.BlockSpec(memory_space=pl.ANY),
        scratch_shapes=[pltpu.VMEM((tm * SUBLANES, LANES), F32),
                        pltpu.VMEM((MOE_TILE // 2 * SUBLANES, LANES), F32),
                        pltpu.SemaphoreType.DMA, pltpu.SemaphoreType.DMA],
    )
    return pl.pallas_call(
        functools.partial(_dispatch_body, tm=tm, n_tiles=n_tiles),
        grid_spec=grid_spec,
        out_shape=jax.ShapeDtypeStruct((n_tiles * MOE_TILE * SUBLANES, LANES), F32),
        compiler_params=_params(("arbitrary",)),
        name="moe_dispatch",
    )(pad_start, pad_count, used, pos, x1, g2.reshape(1, d))


def _ffn_body(lo_ref, hi_ref, used_ref, xs_ref, *refs):
    w_refs, ys_ref, wgu_b, wd_b = refs[:6], refs[6], refs[7], refs[8]
    i = pl.program_id(0)
    live = i < used_ref[0]
    hid = wd_b.shape[1]
    prev = jnp.maximum(i - 1, 0)

    for k, e_ref in enumerate((lo_ref, hi_ref)):
        wg_ref, wu_ref, wd_ref = w_refs[3 * k:3 * k + 3]

        @pl.when(jnp.logical_and(live, jnp.logical_or(i == 0, e_ref[i] != e_ref[prev])))
        def _(k=k, wg_ref=wg_ref, wu_ref=wu_ref, wd_ref=wd_ref):
            _cast_rows(wgu_b.at[k, :, 0:hid], wg_ref)
            _cast_rows(wgu_b.at[k, :, hid:2 * hid], wu_ref)
            _cast_rows(wd_b.at[k], wd_ref)

    @pl.when(live)
    def _():
        hb = _load_token_major(xs_ref, MOE_TILE).astype(BF16)
        for k in range(2):
            gu = jnp.dot(hb, wgu_b[k], preferred_element_type=F32)
            gate, up = gu[:, :hid], gu[:, hid:]
            act = (gate * jax.nn.sigmoid(gate) * up).astype(BF16)
            _store_token_major(ys_ref.at[k], jnp.dot(act, wd_b[k], preferred_element_type=F32))

    @pl.when(jnp.logical_not(live))
    def _():
        ys_ref[...] = jnp.zeros_like(ys_ref)


def _ffn(xs, w_gate, w_up, w_down, layer, lo_expert, hi_expert, used):
    n_tiles = xs.shape[0] // (MOE_TILE * SUBLANES)
    d, hid = w_down.shape[3], w_down.shape[2]
    tile_rows = MOE_TILE * SUBLANES
    live = lambda i, used: jnp.minimum(i, used[0] - 1)

    def weights(pick):
        up = pl.BlockSpec((None, None, d, hid), lambda i, lo, hi, used: (layer, pick(lo, hi)[i], 0, 0))
        down = pl.BlockSpec((None, None, hid, d), lambda i, lo, hi, used: (layer, pick(lo, hi)[i], 0, 0))
        return [up, up, down]

    grid_spec = pltpu.PrefetchScalarGridSpec(
        num_scalar_prefetch=3,
        grid=(n_tiles,),
        in_specs=[pl.BlockSpec((tile_rows, LANES), lambda i, lo, hi, used: (live(i, used), 0))]
        + weights(lambda lo, hi: lo) + weights(lambda lo, hi: hi),
        out_specs=pl.BlockSpec((2, tile_rows, LANES), lambda i, lo, hi, used: (0, i, 0)),
        scratch_shapes=[pltpu.VMEM((2, d, 2 * hid), BF16), pltpu.VMEM((2, hid, d), BF16)],
    )
    return pl.pallas_call(
        _ffn_body,
        grid_spec=grid_spec,
        out_shape=jax.ShapeDtypeStruct((2,) + xs.shape, F32),
        compiler_params=_params(("arbitrary",)),
        name="moe_experts",
    )(lo_expert, hi_expert, used, xs, w_gate, w_up, w_down, w_gate, w_up, w_down)


def _both_planes(ref, tok, n=1):
    return ref.at[:, pl.ds(pl.multiple_of(tok * SUBLANES, SUBLANES), n * SUBLANES)]


def _combine_body(pos_ref, ys_hbm, x1_ref, w_ref, gf_ref, o_ref, buf, sems, *, tm, final_norm):
    i = pl.program_id(0)
    slot = i % 2

    def gather(step, s):
        def issue(g, c):
            base = step * tm + g * ROW_UNROLL
            for j in range(ROW_UNROLL):
                src = _both_planes(ys_hbm, pos_ref[base + j])
                pltpu.make_async_copy(src, _both_planes(buf.at[s], g * ROW_UNROLL + j),
                                      sems.at[s]).start(priority=j % 2)
            return c

        lax.fori_loop(0, tm // ROW_UNROLL, issue, 0)

    @pl.when(i == 0)
    def _():
        gather(i, slot)

    @pl.when(i + 1 < pl.num_programs(0))
    def _():
        gather(i + 1, 1 - slot)

    pltpu.make_async_copy(_both_planes(ys_hbm, 0, tm), buf.at[slot], sems.at[slot]).wait()
    w = w_ref[...]
    y = (x1_ref[...] + w[:, 0:1] * _load_token_major(buf.at[slot, 0], tm)
         + w[:, 1:2] * _load_token_major(buf.at[slot, 1], tm))
    if final_norm:
        y = _rms(y, gf_ref[...])
    o_ref[...] = y


def _combine(ys, x1, pos, w2, gf, final_norm, tm=512):
    t, d = x1.shape
    grid_spec = pltpu.PrefetchScalarGridSpec(
        num_scalar_prefetch=1,
        grid=(t // tm,),
        in_specs=[pl.BlockSpec(memory_space=pl.ANY),
                  pl.BlockSpec((tm, d), lambda i, *_: (i, 0)),
                  pl.BlockSpec((tm, 2), lambda i, *_: (i, 0)),
                  pl.BlockSpec((1, d), lambda i, *_: (0, 0))],
        out_specs=pl.BlockSpec((tm, d), lambda i, *_: (i, 0)),
        scratch_shapes=[pltpu.VMEM((2, 2, tm * SUBLANES, LANES), F32), pltpu.SemaphoreType.DMA((2,))],
    )
    return pl.pallas_call(
        functools.partial(_combine_body, tm=tm, final_norm=final_norm),
        grid_spec=grid_spec,
        out_shape=jax.ShapeDtypeStruct((t, d), F32),
        compiler_params=_params(("arbitrary",)),
        name="moe_combine",
    )(pos, ys, x1, w2, gf.reshape(1, d))


def _layer(x, batch, seq, layer, norm1, w_in, a_ln_g, a_ln_b, a_ws, a_bs, ret_gn, w_out, norm2,
           router_g, router_gb, router_e, router_eb, w_gate, w_up, w_down, final_gain, is_last):
    t, d = x.shape
    a_groups = a_ws.shape[0]
    aw = a_groups * HEAD_DIM
    c_heads = ret_gn.shape[0] // HEAD_DIM
    cw = c_heads * HEAD_DIM
    bw = w_in.shape[2] - 2 * aw - 4 * cw
    b_heads = (bw // 3) // HEAD_DIM
    a_out, zqkv, zg = _inproj(x, norm1, w_in, layer, a_ln_g, a_ln_b, a_ws, a_bs, bw + 3 * cw, cw)
    zqkv3 = zqkv.reshape(batch, seq, zqkv.shape[1])
    b_out = _attn(zqkv3, 0, b_heads).reshape(t, bw // 3)
    c_out = _retention(zqkv3, bw // cw, zg.reshape(batch, seq, cw), c_heads, ret_gn).reshape(t, cw)
    x1, route = _outproj(x, a_out, b_out, c_out, w_out, layer, norm2,
                         router_g, router_gb, router_e, router_eb)
    cls = route[0].astype(I32)
    w2 = route[1:3].T
    n_tiles = t // MOE_TILE + N_CLASSES
    pos, lo_expert, hi_expert, used, pad_start, pad_count = _moe_plan(cls, n_tiles)
    xs = _dispatch(x1, norm2, pos, n_tiles, pad_start, pad_count, used)
    ys = _ffn(xs, w_gate, w_up, w_down, layer, lo_expert, hi_expert, used)
    return _combine(ys, x1, pos, w2, final_gain, is_last)


def kernel(x, norm1, w_in, a_ln_g, a_ln_b, a_ws, a_bs, ret_gn, w_out, norm2, router_g, router_gb, router_e,
           router_eb, w_gate, w_up, w_down, final_norm):
    batch, seq, d = x.shape
    depth = norm1.shape[0]
    h = x.reshape(batch * seq, d)
    for l in range(depth):
        h = _layer(h, batch, seq, l, norm1[l], w_in, a_ln_g[l], a_ln_b[l], a_ws[l], a_bs[l], ret_gn[l],
                   w_out, norm2[l], router_g[l], router_gb[l], router_e[l], router_eb[l],
                   w_gate, w_up, w_down, final_norm, l == depth - 1)
    return h.reshape(batch, seq, d)
```

```python
import functools
import itertools

import numpy as np
import jax
import jax.numpy as jnp
from jax import lax
from jax.experimental import pallas as pl
from jax.experimental.pallas import tpu as pltpu

F32 = jnp.float32
BF16 = jnp.bfloat16
I32 = jnp.int32

EPS = 1e-6
HEAD_DIM = 64
SGU_CHUNK = 128
RET_CHUNK = 128
ATTN_BLOCK = 128
DILATIONS = (1, 4, 16)
ATTN_UNROLL = 32
ATTN_SLAB = 256
ROPE_BASE = 10000.0
N_GROUPS = 4
EXPERTS_PER_GROUP = 4
N_EXPERTS = N_GROUPS * EXPERTS_PER_GROUP
_PAIRS = tuple(itertools.combinations(range(EXPERTS_PER_GROUP), 2))
N_CLASSES = N_GROUPS * len(_PAIRS)
LANES = 128
SUBLANES = 8
PACKED_ROWS = 16
MOE_TILE = 256
ROW_UNROLL = 8
CAST_SLAB = 256
OUTPROJ_CHAIN_ROWS = 512
VMEM_LIMIT = 56 * 1024 * 1024


def _params(sem, vmem=VMEM_LIMIT):
    return pltpu.CompilerParams(dimension_semantics=sem, vmem_limit_bytes=vmem)


def _rms(x, g):
    return x * lax.rsqrt(jnp.mean(x * x, axis=-1, keepdims=True) + EPS) * g


def _split_bf16(a):
    hi = a.astype(BF16)
    lo = (a - hi.astype(F32)).astype(BF16)
    return hi, lo


def _cast_rows(dst_ref, src_ref, rows=CAST_SLAB):
    for r0 in range(0, src_ref.shape[0], rows):
        dst_ref[r0:r0 + rows, :] = src_ref[r0:r0 + rows, :].astype(dst_ref.dtype)


def _gelu_tanh(x):
    return 0.5 * x * (1.0 + jnp.tanh(np.sqrt(2.0 / np.pi).astype(np.float32) * (x + 0.044715 * (x * x * x))))


def _spatial_gating(za, lng_ref, lnb_ref, wcat_ref, bias_ref, o_ref):
    ga = _gelu_tanh(za)
    aw = ga.shape[1] // 2
    u = ga[:, :aw]
    v = ga[:, aw:]
    mu = jnp.mean(v, axis=-1, keepdims=True)
    dv = v - mu
    var = jnp.mean(dv * dv, axis=-1, keepdims=True)
    vn = dv * lax.rsqrt(var + EPS) * lng_ref[...] + lnb_ref[...]
    group = lax.broadcasted_iota(I32, (1, aw), 1) // HEAD_DIM
    n_groups = aw // HEAD_DIM
    for c in range(ga.shape[0] // SGU_CHUNK):
        rows = slice(c * SGU_CHUNK, (c + 1) * SGU_CHUNK)
        vc = vn[rows]
        stack = jnp.concatenate([jnp.where(group == g, vc, 0.0) for g in range(n_groups)], axis=0)
        s = jnp.dot(wcat_ref[...], stack.astype(BF16), preferred_element_type=F32) + bias_ref[...]
        o_ref[rows, :] = (u[rows] * s).astype(o_ref.dtype)


def _inproj_body(x_ref, g_ref, w_ref, lng_ref, lnb_ref, wcat_ref, bias_ref, a_ref, qkv_ref, gate_ref, wb,
                 *, n_chunk):
    @pl.when(pl.program_id(0) == 0)
    def _():
        _cast_rows(wb, w_ref)

    hb = _rms(x_ref[...], g_ref[...]).astype(BF16)
    col = 2 * a_ref.shape[1]
    _spatial_gating(jnp.dot(hb, wb[:, 0:col], preferred_element_type=F32), lng_ref, lnb_ref, wcat_ref, bias_ref,
                    a_ref)
    for o_ref in (qkv_ref, gate_ref):
        for n0 in range(0, o_ref.shape[1], n_chunk):
            n1 = min(n0 + n_chunk, o_ref.shape[1])
            acc = jnp.dot(hb, wb[:, col + n0:col + n1], preferred_element_type=F32)
            o_ref[:, n0:n1] = acc.astype(o_ref.dtype)
        col += o_ref.shape[1]


def _layer_resident(stacked, layer):
    rest = stacked.shape[1:]
    return pl.BlockSpec((None,) + rest, lambda *_: (layer,) + (0,) * len(rest), pipeline_mode=pl.Buffered(1))


def _inproj(x, g, w_stack, layer, ln_g, ln_b, w_s, b_s, qkv_width, gate_width, tm=1024, n_chunk=512):
    t, d = x.shape
    n_groups = w_s.shape[0]
    aw = n_groups * HEAD_DIM
    assert 2 * aw + qkv_width + gate_width == w_stack.shape[2]
    causal = jnp.tril(jnp.ones((SGU_CHUNK, SGU_CHUNK), dtype=bool))
    wcat = jnp.where(causal[None], w_s, 0.0).transpose(1, 0, 2).reshape(SGU_CHUNK, n_groups * SGU_CHUNK)
    bias = jnp.repeat(b_s.T, HEAD_DIM, axis=1)
    full = lambda shape: pl.BlockSpec(shape, lambda i: (0, 0))
    widths = (aw, qkv_width, gate_width)
    dtypes = (BF16, BF16, F32)
    return pl.pallas_call(
        functools.partial(_inproj_body, n_chunk=n_chunk),
        grid=(t // tm,),
        in_specs=[pl.BlockSpec((tm, d), lambda i: (i, 0)), full((1, d)), _layer_resident(w_stack, layer),
                  full((1, aw)), full((1, aw)), full(wcat.shape), full(bias.shape)],
        out_specs=[pl.BlockSpec((tm, n), lambda i: (i, 0)) for n in widths],
        out_shape=[jax.ShapeDtypeStruct((t, n), dt) for n, dt in zip(widths, dtypes)],
        scratch_shapes=[pltpu.VMEM(w_stack.shape[1:], BF16)],
        compiler_params=_params(("arbitrary",)),
        name="inproj_sgu",
    )(x, g.reshape(1, d), w_stack, ln_g.reshape(1, aw), ln_b.reshape(1, aw), wcat.astype(BF16), bias)


def _attn_bias():
    blk = ATTN_BLOCK
    qi = np.arange(2 * blk)[:, None] % blk
    ci = np.arange(2 * blk)[None, :]
    first = (ci < blk) & (ci <= qi)
    later = np.where(ci < blk, ci >= qi, ci - blk <= qi)
    return np.where(np.stack([first, later]), 0.0, -np.inf).astype(np.float32)


def _attn_body(q_ref, k_ref, v_ref, bias_ref, o_ref, qkv_scr, ob_scr, m_scr, l_scr, *, seq, unroll):
    blk = ATTN_BLOCK
    head0 = lax.broadcasted_iota(I32, (1, LANES), 1) < HEAD_DIM
    scale = HEAD_DIM ** -0.5 * np.log2(np.e)

    widen_rows = ATTN_SLAB

    def widen(i, carry):
        sl = pl.ds(pl.multiple_of(i * widen_rows, widen_rows), widen_rows)
        qkv_scr[0, 0, sl, :] = q_ref[0, sl, :].astype(F32) * scale
        qkv_scr[0, 1, sl, :] = k_ref[0, sl, :].astype(F32)
        qkv_scr[0, 2, sl, :] = v_ref[0, sl, :].astype(F32)
        return carry

    lax.fori_loop(0, seq // widen_rows, widen, 0)

    src = {DILATIONS[0]: qkv_scr.at[0]}
    for stage, d in enumerate(DILATIONS[1:], start=1):
        prev_d, ratio, slab = DILATIONS[stage - 1], d // DILATIONS[stage - 1], seq // d
        prev, cur = qkv_scr.at[stage - 1], qkv_scr.at[stage]
        for r_prev in range(prev_d):
            for c in range(ratio):
                r = r_prev + prev_d * c
                for a in range(3):
                    cur[a, r * slab:(r + 1) * slab, :] = prev[a, pl.ds(r_prev * (seq // prev_d) + c, slab,
                                                                     stride=ratio), :]
        src[d] = cur

    def rows(start, n, d):
        return pl.ds(start, n) if d == 1 else pl.ds(start, n, stride=d)

    def block(p, d, r, n):
        base = n * (blk * d) + r
        first = r * (seq // d) + n * blk
        kfirst = r * (seq // d) + jnp.maximum(n - 1, 0) * blk
        first, kfirst = pl.multiple_of(first, blk), pl.multiple_of(kfirst, blk)
        qb = src[d][0, pl.ds(first, blk), :]
        qs = jnp.concatenate([jnp.where(head0, qb, 0.0), jnp.where(head0, 0.0, qb)], axis=0).astype(BF16)
        kb = src[d][1, pl.ds(kfirst, 2 * blk), :].astype(BF16)
        vb = src[d][2, pl.ds(kfirst, 2 * blk), :].astype(BF16)
        va = jnp.concatenate([vb, jnp.ones_like(vb)], axis=1)
        s = lax.dot_general(qs, kb, (((1,), (1,)), ((), ())), preferred_element_type=F32)
        s = s + bias_ref[jnp.minimum(n, 1)]
        m = jnp.max(s, axis=-1, keepdims=True)
        e = jnp.exp2(s - m)
        oa = jnp.dot(e.astype(BF16), va, preferred_element_type=F32)
        ob_scr[p, rows(base, blk, d), :] = jnp.where(head0, oa[:blk, :LANES], oa[blk:, :LANES])
        l_scr[p, rows(base, blk, d), :] = jnp.where(head0, oa[:blk, LANES:], oa[blk:, LANES:])
        m_scr[p, rows(base, blk, d), :] = jnp.where(head0, m[:blk], m[blk:])

    for p, d in enumerate(DILATIONS):
        def group(i, carry, p=p, d=d):
            for u in range(unroll):
                b = i * unroll + u
                block(p, d, b & (d - 1), b >> (d.bit_length() - 1))
            return carry

        lax.fori_loop(0, seq // (blk * unroll), group, 0)

    step = ATTN_SLAB

    def mix(i, carry):
        sl = pl.ds(pl.multiple_of(i * step, step), step)
        ms = [m_scr[p, sl, :] for p in range(len(DILATIONS))]
        top = functools.reduce(jnp.maximum, ms)
        es = [jnp.exp2(m - top) for m in ms]
        num = sum(e * ob_scr[p, sl, :] for p, e in enumerate(es))
        den = sum(e * l_scr[p, sl, :] for p, e in enumerate(es))
        o_ref[0, sl, :] = (num / den).astype(o_ref.dtype)
        return carry

    lax.fori_loop(0, seq // step, mix, 0)


def _attn(z3, q_col, n_heads):
    b, s, _ = z3.shape
    n_pairs = n_heads * HEAD_DIM // LANES
    assert s % (ATTN_BLOCK * ATTN_UNROLL) == 0 and s % (2 * ATTN_BLOCK * max(DILATIONS)) == 0
    assert all(d & (d - 1) == 0 for d in DILATIONS)
    bias = jnp.asarray(_attn_bias())

    def spec(off):
        return pl.BlockSpec((1, s, LANES), lambda i, j: (i, 0, off + j))

    return pl.pallas_call(
        functools.partial(_attn_body, seq=s, unroll=ATTN_UNROLL),
        grid=(b, n_pairs),
        in_specs=[spec(q_col), spec(q_col + n_pairs), spec(q_col + 2 * n_pairs),
                  pl.BlockSpec(bias.shape, lambda i, j: (0, 0, 0))],
        out_specs=pl.BlockSpec((1, s, LANES), lambda i, j: (i, 0, j)),
        out_shape=jax.ShapeDtypeStruct((b, s, n_pairs * LANES), BF16),
        scratch_shapes=[pltpu.VMEM((len(DILATIONS), 3, s, LANES), F32),
                        pltpu.VMEM((len(DILATIONS), s, LANES), F32),
                        pltpu.VMEM((len(DILATIONS), s, LANES), F32),
                        pltpu.VMEM((len(DILATIONS), s, LANES), F32)],
        compiler_params=_params(("parallel", "parallel")),
        name="dilated_attn",
    )(z3, z3, z3, bias)


def _ret_body(q_ref, k_ref, v_ref, g_ref, cos_ref, sa_ref, sb_ref, dec_ref, qdec_ref, kdec_ref, cdec_ref,
              avg_ref, gn_ref, o_ref, state):
    cw = q_ref.shape[2]
    n_heads = cw // HEAD_DIM
    head = lax.broadcasted_iota(I32, (1, cw), 1) // HEAD_DIM
    hr = lax.broadcasted_iota(I32, (cw, cw), 0) // HEAD_DIM
    hc = lax.broadcasted_iota(I32, (cw, cw), 1) // HEAD_DIM
    same_head = hr == hc

    @pl.when(pl.program_id(1) == 0)
    def _():
        state[...] = jnp.zeros_like(state)

    def rotary(x, cos, sa, sb):
        half = HEAD_DIM // 2
        return x * cos + pltpu.roll(x, half, 1) * sa + pltpu.roll(x, cw - half, 1) * sb

    def head_mean(a):
        hi, lo = _split_bf16(a)
        return (jnp.dot(hi, avg_ref[...], preferred_element_type=F32)
                + jnp.dot(lo, avg_ref[...], preferred_element_type=F32))

    for c in range(q_ref.shape[1] // RET_CHUNK):
        rows = slice(c * RET_CHUNK, (c + 1) * RET_CHUNK)
        cos, sa, sb = cos_ref[rows, :], sa_ref[rows, :], sb_ref[rows, :]
        qr = rotary(q_ref[0, rows, :].astype(F32), cos, sa, sb)
        kr = rotary(k_ref[0, rows, :].astype(F32), cos, sa, sb) * (HEAD_DIM ** -0.5)
        v = v_ref[0, rows, :].astype(F32)
        krb = kr.astype(BF16)
        inner = [lax.dot_general(jnp.where(head == h, qr, 0.0).astype(BF16), krb, (((1,), (1,)), ((), ())),
                                 preferred_element_type=F32) for h in range(n_heads)]
        inner = jnp.concatenate(inner, axis=1) * dec_ref[...]
        vstack = jnp.concatenate([jnp.where(head == h, v, 0.0) for h in range(n_heads)], axis=0)
        st = state[...]
        o = (jnp.dot(inner.astype(BF16), vstack.astype(BF16), preferred_element_type=F32)
             + jnp.dot((qr * qdec_ref[...]).astype(BF16), st.astype(BF16), preferred_element_type=F32))
        ktv = lax.dot_general((kr * kdec_ref[...]).astype(BF16), v.astype(BF16), (((0,), (0,)), ((), ())),
                              preferred_element_type=F32)
        state[...] = st * cdec_ref[...] + jnp.where(same_head, ktv, 0.0)
        mu = head_mean(o)
        dd = o - mu
        var = head_mean(dd * dd)
        on = dd * lax.rsqrt(var + EPS) * gn_ref[...]
        gate = g_ref[0, rows, :]
        o_ref[0, rows, :] = (gate * jax.nn.sigmoid(gate) * on).astype(o_ref.dtype)


def _ret_tables(seq, n_heads):
    half = HEAD_DIM // 2
    inv = ROPE_BASE ** (-jnp.arange(half, dtype=F32) / half)
    ang = jnp.arange(seq, dtype=F32)[:, None] * inv[None]
    cos, sin = jnp.cos(ang), jnp.sin(ang)
    zero = jnp.zeros_like(sin)
    tile = lambda a: jnp.tile(a, (1, n_heads))
    cos_t = tile(jnp.concatenate([cos, cos], axis=1))
    sa_t = tile(jnp.concatenate([zero, sin], axis=1))
    sb_t = tile(jnp.concatenate([-sin, zero], axis=1))
    log_g = jnp.log(1.0 - 2.0 ** (-5.0 - jnp.arange(n_heads, dtype=F32)))
    c = RET_CHUNK
    idx = jnp.arange(c)
    diff = idx[:, None] - idx[None, :]
    decay_in = jnp.where(diff >= 0, jnp.exp(log_g[:, None, None] * jnp.maximum(diff, 0)[None]), 0.0)
    dec = decay_in.transpose(1, 0, 2).reshape(c, n_heads * c)
    lane_head = jnp.repeat(jnp.arange(n_heads), HEAD_DIM)
    qdec = jnp.exp(log_g[lane_head][None, :] * (idx + 1)[:, None].astype(F32))
    kdec = jnp.exp(log_g[lane_head][None, :] * (c - 1 - idx)[:, None].astype(F32))
    same = lane_head[:, None] == lane_head[None, :]
    cdec = jnp.where(same, jnp.exp(log_g * c)[lane_head][:, None], 0.0)
    avg = jnp.where(same, 1.0 / HEAD_DIM, 0.0).astype(BF16)
    return cos_t, sa_t, sb_t, dec, qdec, kdec, cdec, avg


def _retention(z3, col, gate3, n_heads, gn, tc=2048):
    b, s, _ = z3.shape
    cw = n_heads * HEAD_DIM
    cos_t, sa_t, sb_t, dec, qdec, kdec, cdec, avg = _ret_tables(s, n_heads)

    def zspec(off):
        return pl.BlockSpec((1, tc, cw), lambda i, j: (i, j, off))

    tab = pl.BlockSpec((tc, cw), lambda i, j: (j, 0))
    full = lambda a: pl.BlockSpec(a.shape, lambda i, j: (0, 0))
    gn2 = gn.reshape(1, cw)
    return pl.pallas_call(
        _ret_body,
        grid=(b, s // tc),
        in_specs=[zspec(col), zspec(col + 1), zspec(col + 2), zspec(0), tab, tab, tab,
                  full(dec), full(qdec), full(kdec), full(cdec), full(avg), full(gn2)],
        out_specs=pl.BlockSpec((1, tc, cw), lambda i, j: (i, j, 0)),
        out_shape=jax.ShapeDtypeStruct((b, s, cw), BF16),
        scratch_shapes=[pltpu.VMEM((cw, cw), F32)],
        compiler_params=_params(("parallel", "arbitrary")),
        name="retention",
    )(z3, z3, z3, gate3, cos_t, sa_t, sb_t, dec, qdec, kdec, cdec, avg, gn2)


def _outproj_body(x_ref, a_ref, b_ref, c_ref, wo_ref, g2_ref, rhi_ref, rlo_ref, rb_ref, x1_ref, route_ref, wb):
    @pl.when(pl.program_id(0) == 0)
    def _():
        _cast_rows(wb, wo_ref)

    n_chains = x_ref.shape[0] // OUTPROJ_CHAIN_ROWS
    for ch in range(n_chains):
        rows = slice(ch * OUTPROJ_CHAIN_ROWS, (ch + 1) * OUTPROJ_CHAIN_ROWS)
        _outproj_rows(rows, x_ref, a_ref, b_ref, c_ref, wb, g2_ref, rhi_ref, rlo_ref, rb_ref, x1_ref, route_ref)


def _outproj_rows(rows, x_ref, a_ref, b_ref, c_ref, wb, g2_ref, rhi_ref, rlo_ref, rb_ref, x1_ref, route_ref):
    aw, bw = a_ref.shape[1], b_ref.shape[1]
    y = (jnp.dot(a_ref[rows, :], wb[0:aw, :], preferred_element_type=F32)
         + jnp.dot(b_ref[rows, :], wb[aw:aw + bw, :], preferred_element_type=F32)
         + jnp.dot(c_ref[rows, :], wb[aw + bw:, :], preferred_element_type=F32))
    x1 = x_ref[rows, :] + y
    x1_ref[rows, :] = x1
    hi, lo = _split_bf16(_rms(x1, g2_ref[...]))
    nt = lambda r, h: lax.dot_general(r, h, (((1,), (1,)), ((), ())), preferred_element_type=F32)
    lt = nt(rhi_ref[...], hi) + nt(rhi_ref[...], lo) + nt(rlo_ref[...], hi) + rb_ref[...]
    row = lambda i: lt[i:i + 1, :]
    best, gi = row(0), jnp.zeros_like(row(0), dtype=I32)
    for i in range(1, N_GROUPS):
        up = row(i) > best
        best = jnp.where(up, row(i), best)
        gi = jnp.where(up, i, gi)
    g_w = 1.0 / sum(jnp.exp(row(i) - best) for i in range(N_GROUPS))
    el = []
    for j in range(EXPERTS_PER_GROUP):
        e = row(N_GROUPS + (N_GROUPS - 1) * EXPERTS_PER_GROUP + j)
        for g in range(N_GROUPS - 2, -1, -1):
            e = jnp.where(gi == g, row(N_GROUPS + g * EXPERTS_PER_GROUP + j), e)
        el.append(e)
    v1, i1 = el[0], jnp.zeros_like(gi)
    for j in range(1, EXPERTS_PER_GROUP):
        up = el[j] > v1
        v1 = jnp.where(up, el[j], v1)
        i1 = jnp.where(up, j, i1)
    v2, i2 = jnp.full_like(v1, -jnp.inf), jnp.zeros_like(gi)
    for j in range(EXPERTS_PER_GROUP):
        up = jnp.logical_and(i1 != j, el[j] > v2)
        v2 = jnp.where(up, el[j], v2)
        i2 = jnp.where(up, j, i2)
    e21 = jnp.exp(v2 - v1)
    w1 = g_w / (1.0 + e21)
    w2 = g_w * e21 / (1.0 + e21)
    swap = i2 < i1
    e_lo, e_hi = jnp.where(swap, i2, i1), jnp.where(swap, i1, i2)
    w_lo, w_hi = jnp.where(swap, w2, w1), jnp.where(swap, w1, w2)
    first_pair = jnp.zeros_like(e_lo)
    for lo in range(1, EXPERTS_PER_GROUP - 1):
        first_pair = jnp.where(e_lo >= lo, _PAIRS.index((lo, lo + 1)), first_pair)
    cls = (gi * len(_PAIRS) + first_pair + e_hi - e_lo - 1).astype(F32)
    zero = jnp.zeros_like(w1)
    route_ref[:, rows] = jnp.concatenate([cls, w_lo, w_hi, zero, zero, zero, zero, zero], axis=0)


def _outproj(x, a, b, c, wo_stack, layer, g2, router_g, router_gb, router_e, router_eb, tm=1024):
    t, d = x.shape
    n_logits = N_GROUPS + N_EXPERTS
    n_rows = -(-n_logits // PACKED_ROWS) * PACKED_ROWS
    r = jnp.pad(jnp.concatenate([router_g, router_e], axis=1).T, ((0, n_rows - n_logits), (0, 0)))
    rhi = r.astype(BF16)
    rlo = (r - rhi.astype(F32)).astype(BF16)
    rb = jnp.pad(jnp.concatenate([router_gb, router_eb]), (0, n_rows - n_logits)).reshape(n_rows, 1)
    row_spec = lambda w: pl.BlockSpec((tm, w), lambda i: (i, 0))
    full = lambda arr: pl.BlockSpec(arr.shape, lambda i: (0, 0))
    g2r = g2.reshape(1, d)
    return pl.pallas_call(
        _outproj_body,
        grid=(t // tm,),
        in_specs=[row_spec(d), row_spec(a.shape[1]), row_spec(b.shape[1]), row_spec(c.shape[1]),
                  _layer_resident(wo_stack, layer), full(g2r), full(rhi), full(rlo), full(rb)],
        out_specs=[row_spec(d), pl.BlockSpec((8, tm), lambda i: (0, i))],
        out_shape=[jax.ShapeDtypeStruct((t, d), F32), jax.ShapeDtypeStruct((8, t), F32)],
        scratch_shapes=[pltpu.VMEM(wo_stack.shape[1:], BF16)],
        compiler_params=_params(("arbitrary",)),
        name="outproj_route",
    )(x, a, b, c, wo_stack, g2r, rhi, rlo, rb)


def _moe_plan(cls, n_tiles):
    onehot = (cls[:, None] == jnp.arange(N_CLASSES, dtype=I32)[None, :]).astype(I32)
    csum = jnp.cumsum(onehot, axis=0)
    counts = csum[-1]
    padded = ((counts + MOE_TILE - 1) // MOE_TILE) * MOE_TILE
    ends = jnp.cumsum(padded)
    starts = ends - padded
    pos = jnp.sum(onehot * (csum - 1 + starts[None, :]), axis=1)
    tile_start = jnp.arange(n_tiles, dtype=I32) * MOE_TILE
    tile_cls = jnp.sum((ends[None, :] <= tile_start[:, None]).astype(I32), axis=1)
    tile_cls = jnp.minimum(tile_cls, N_CLASSES - 1)
    group, pair = np.divmod(np.arange(N_CLASSES), len(_PAIRS))
    pairs = np.asarray(_PAIRS)
    lo_expert = jnp.asarray(group * EXPERTS_PER_GROUP + pairs[pair, 0], I32)[tile_cls]
    hi_expert = jnp.asarray(group * EXPERTS_PER_GROUP + pairs[pair, 1], I32)[tile_cls]
    used = (ends[-1] // MOE_TILE).astype(I32).reshape(1)
    pad_start = (starts + counts).astype(I32)
    pad_count = (padded - counts).astype(I32)
    return pos.astype(I32), lo_expert, hi_expert, used, pad_start, pad_count


def _store_token_major(ref, x, tok0=0):
    n = x.shape[0]
    for c in range(SUBLANES):
        ref[pl.ds(tok0 * SUBLANES + c, n, stride=SUBLANES), :] = x[:, c * LANES:(c + 1) * LANES]


def _load_token_major(ref, n, tok0=0):
    return jnp.concatenate([ref[pl.ds(tok0 * SUBLANES + c, n, stride=SUBLANES), :] for c in range(SUBLANES)],
                           axis=1)


def _tokens(ref, tok, n=1):
    return ref.at[pl.ds(pl.multiple_of(tok * SUBLANES, SUBLANES), n * SUBLANES)]


def _dispatch_body(pad_start_ref, pad_count_ref, used_ref, pos_ref, x_ref, g2_ref, xs_hbm, xt, zeros, sem, fill_sem,
                   *, tm, n_tiles):
    i = pl.program_id(0)
    half_tile = MOE_TILE // 2

    def fill(act):
        def class_pad(e, c):
            off, n = pad_start_ref[e], pad_count_ref[e]
            bit = half_tile
            while bit:
                take = (n & bit) != 0

                @pl.when(take)
                def _(off=off, bit=bit):
                    act(pltpu.make_async_copy(_tokens(zeros, 0, bit), _tokens(xs_hbm, off, bit), fill_sem))

                off = off + jnp.where(take, bit, 0)
                bit //= 2
            return c

        lax.fori_loop(0, N_CLASSES, class_pad, 0)

        def unused_half_tile(j, c):
            act(pltpu.make_async_copy(zeros, _tokens(xs_hbm, j * half_tile, half_tile), fill_sem))
            return c

        lax.fori_loop(2 * used_ref[0], 2 * n_tiles, unused_half_tile, 0)

    @pl.when(i == 0)
    def _():
        zeros[...] = jnp.zeros_like(zeros)
        fill(lambda cp: cp.start())
        fill(lambda cp: cp.wait())

    _store_token_major(xt, _rms(x_ref[...], g2_ref[...]))

    def issue(g, c):
        base = i * tm + g * ROW_UNROLL
        for j in range(ROW_UNROLL):
            dst = _tokens(xs_hbm, pos_ref[base + j])
            pltpu.make_async_copy(_tokens(xt, g * ROW_UNROLL + j), dst, sem).start(priority=j % 2)
        return c

    lax.fori_loop(0, tm // ROW_UNROLL, issue, 0)
    pltpu.make_async_copy(xt, _tokens(xs_hbm, 0, tm), sem).wait()


def _dispatch(x1, g2, pos, n_tiles, pad_start, pad_count, used, tm=1024):
    t, d = x1.shape
    assert d == SUBLANES * LANES
    grid_spec = pltpu.PrefetchScalarGridSpec(
        num_scalar_prefetch=4,
        grid=(t // tm,),
        in_specs=[pl.BlockSpec((tm, d), lambda i, *_: (i, 0)),
                  pl.BlockSpec((1, d), lambda i, *_: (0, 0))],
        out_specs=pl.BlockSpec(memory_space=pl.ANY),
        scratch_shapes=[pltpu.VMEM((tm * SUBLANES, LANES), F32),
                        pltpu.VMEM((MOE_TILE // 2 * SUBLANES, LANES), F32),
                        pltpu.SemaphoreType.DMA, pltpu.SemaphoreType.DMA],
    )
    return pl.pallas_call(
        functools.partial(_dispatch_body, tm=tm, n_tiles=n_tiles),
        grid_spec=grid_spec,
        out_shape=jax.ShapeDtypeStruct((n_tiles * MOE_TILE * SUBLANES, LANES), F32),
        compiler_params=_params(("arbitrary",)),
        name="moe_dispatch",
    )(pad_start, pad_count, used, pos, x1, g2.reshape(1, d))


def _pack_bf16_pair(lo, hi):
    bits = lambda a: lax.bitcast_convert_type(a.astype(BF16).astype(F32), jnp.uint32)
    return (bits(lo) >> 16) | (bits(hi) & jnp.uint32(0xFFFF0000))


def _unpack_bf16_pair(word):
    lo = lax.bitcast_convert_type(word << 16, F32)
    hi = lax.bitcast_convert_type(word & jnp.uint32(0xFFFF0000), F32)
    return lo, hi


def _ffn_body(lo_ref, hi_ref, used_ref, xs_ref, *refs):
    w_refs, ys_ref, wgu_b, wd_b = refs[:6], refs[6], refs[7], refs[8]
    i = pl.program_id(0)
    live = i < used_ref[0]
    hid = wd_b.shape[1]
    prev = jnp.maximum(i - 1, 0)

    for k, e_ref in enumerate((lo_ref, hi_ref)):
        wg_ref, wu_ref, wd_ref = w_refs[3 * k:3 * k + 3]

        @pl.when(jnp.logical_and(live, jnp.logical_or(i == 0, e_ref[i] != e_ref[prev])))
        def _(k=k, wg_ref=wg_ref, wu_ref=wu_ref, wd_ref=wd_ref):
            _cast_rows(wgu_b.at[k, :, 0:hid], wg_ref)
            _cast_rows(wgu_b.at[k, :, hid:2 * hid], wu_ref)
            _cast_rows(wd_b.at[k], wd_ref)

    @pl.when(live)
    def _():
        hb = _load_token_major(xs_ref, MOE_TILE).astype(BF16)
        ys = []
        for k in range(2):
            gu = jnp.dot(hb, wgu_b[k], preferred_element_type=F32)
            gate, up = gu[:, :hid], gu[:, hid:]
            act = (gate * jax.nn.sigmoid(gate) * up).astype(BF16)
            ys.append(jnp.dot(act, wd_b[k], preferred_element_type=F32))
        _store_token_major(ys_ref, _pack_bf16_pair(*ys))

    @pl.when(jnp.logical_not(live))
    def _():
        ys_ref[...] = jnp.zeros_like(ys_ref)


def _ffn(xs, w_gate, w_up, w_down, layer, lo_expert, hi_expert, used):
    n_tiles = xs.shape[0] // (MOE_TILE * SUBLANES)
    d, hid = w_down.shape[3], w_down.shape[2]
    tile_rows = MOE_TILE * SUBLANES
    live = lambda i, used: jnp.minimum(i, used[0] - 1)

    def weights(pick):
        up = pl.BlockSpec((None, None, d, hid), lambda i, lo, hi, used: (layer, pick(lo, hi)[i], 0, 0))
        down = pl.BlockSpec((None, None, hid, d), lambda i, lo, hi, used: (layer, pick(lo, hi)[i], 0, 0))
        return [up, up, down]

    grid_spec = pltpu.PrefetchScalarGridSpec(
        num_scalar_prefetch=3,
        grid=(n_tiles,),
        in_specs=[pl.BlockSpec((tile_rows, LANES), lambda i, lo, hi, used: (live(i, used), 0))]
        + weights(lambda lo, hi: lo) + weights(lambda lo, hi: hi),
        out_specs=pl.BlockSpec((tile_rows, LANES), lambda i, lo, hi, used: (i, 0)),
        scratch_shapes=[pltpu.VMEM((2, d, 2 * hid), BF16), pltpu.VMEM((2, hid, d), BF16)],
    )
    return pl.pallas_call(
        _ffn_body,
        grid_spec=grid_spec,
        out_shape=jax.ShapeDtypeStruct(xs.shape, jnp.uint32),
        compiler_params=_params(("arbitrary",)),
        name="moe_experts",
    )(lo_expert, hi_expert, used, xs, w_gate, w_up, w_down, w_gate, w_up, w_down)


def _combine_body(pos_ref, ys_hbm, x1_ref, w_ref, gf_ref, o_ref, buf, sems, *, tm, final_norm):
    i = pl.program_id(0)
    slot = i % 2

    def gather(step, s):
        def issue(g, c):
            base = step * tm + g * ROW_UNROLL
            for j in range(ROW_UNROLL):
                src = _tokens(ys_hbm, pos_ref[base + j])
                pltpu.make_async_copy(src, _tokens(buf.at[s], g * ROW_UNROLL + j),
                                      sems.at[s]).start(priority=j % 2)
            return c

        lax.fori_loop(0, tm // ROW_UNROLL, issue, 0)

    @pl.when(i == 0)
    def _():
        gather(i, slot)

    @pl.when(i + 1 < pl.num_programs(0))
    def _():
        gather(i + 1, 1 - slot)

    pltpu.make_async_copy(_tokens(ys_hbm, 0, tm), buf.at[slot], sems.at[slot]).wait()
    w = w_ref[...]
    y_lo, y_hi = _unpack_bf16_pair(_load_token_major(buf.at[slot], tm))
    y = x1_ref[...] + w[:, 0:1] * y_lo + w[:, 1:2] * y_hi
    if final_norm:
        y = _rms(y, gf_ref[...])
    o_ref[...] = y


def _combine(ys, x1, pos, w2, gf, final_norm, tm=512):
    t, d = x1.shape
    grid_spec = pltpu.PrefetchScalarGridSpec(
        num_scalar_prefetch=1,
        grid=(t // tm,),
        in_specs=[pl.BlockSpec(memory_space=pl.ANY),
                  pl.BlockSpec((tm, d), lambda i, *_: (i, 0)),
                  pl.BlockSpec((tm, 2), lambda i, *_: (i, 0)),
                  pl.BlockSpec((1, d), lambda i, *_: (0, 0))],
        out_specs=pl.BlockSpec((tm, d), lambda i, *_: (i, 0)),
        scratch_shapes=[pltpu.VMEM((2, tm * SUBLANES, LANES), jnp.uint32), pltpu.SemaphoreType.DMA((2,))],
    )
    return pl.pallas_call(
        functools.partial(_combine_body, tm=tm, final_norm=final_norm),
        grid_spec=grid_spec,
        out_shape=jax.ShapeDtypeStruct((t, d), F32),
        compiler_params=_params(("arbitrary",)),
        name="moe_combine",
    )(pos, ys, x1, w2, gf.reshape(1, d))


def _layer(x, batch, seq, layer, norm1, w_in, a_ln_g, a_ln_b, a_ws, a_bs, ret_gn, w_out, norm2,
           router_g, router_gb, router_e, router_eb, w_gate, w_up, w_down, final_gain, is_last):
    t, d = x.shape
    a_groups = a_ws.shape[0]
    aw = a_groups * HEAD_DIM
    c_heads = ret_gn.shape[0] // HEAD_DIM
    cw = c_heads * HEAD_DIM
    bw = w_in.shape[2] - 2 * aw - 4 * cw
    b_heads = (bw // 3) // HEAD_DIM
    a_out, zqkv, zg = _inproj(x, norm1, w_in, layer, a_ln_g, a_ln_b, a_ws, a_bs, bw + 3 * cw, cw)
    zqkv3 = zqkv.reshape(batch, seq, zqkv.shape[1])
    b_out = _attn(zqkv3, 0, b_heads).reshape(t, bw // 3)
    c_out = _retention(zqkv3, bw // cw, zg.reshape(batch, seq, cw), c_heads, ret_gn).reshape(t, cw)
    x1, route = _outproj(x, a_out, b_out, c_out, w_out, layer, norm2,
                         router_g, router_gb, router_e, router_eb)
    cls = route[0].astype(I32)
    w2 = route[1:3].T
    n_tiles = t // MOE_TILE + N_CLASSES
    pos, lo_expert, hi_expert, used, pad_start, pad_count = _moe_plan(cls, n_tiles)
    xs = _dispatch(x1, norm2, pos, n_tiles, pad_start, pad_count, used)
    ys = _ffn(xs, w_gate, w_up, w_down, layer, lo_expert, hi_expert, used)
    return _combine(ys, x1, pos, w2, final_gain, is_last)


def kernel(x, norm1, w_in, a_ln_g, a_ln_b, a_ws, a_bs, ret_gn, w_out, norm2, router_g, router_gb, router_e,
           router_eb, w_gate, w_up, w_down, final_norm):
    batch, seq, d = x.shape
    depth = norm1.shape[0]
    h = x.reshape(batch * seq, d)
    for l in range(depth):
        h = _layer(h, batch, seq, l, norm1[l], w_in, a_ln_g[l], a_ln_b[l], a_ws[l], a_bs[l], ret_gn[l],
                   w_out, norm2[l], router_g[l], router_gb[l], router_e[l], router_eb[l],
                   w_gate, w_up, w_down, final_norm, l == depth - 1)
    return h.reshape(batch, seq, d)
```

```python
import functools
import itertools

import numpy as np
import jax
import jax.numpy as jnp
from jax import lax
from jax.experimental import pallas as pl
from jax.experimental.pallas import tpu as pltpu

F32 = jnp.float32
BF16 = jnp.bfloat16
I32 = jnp.int32

EPS = 1e-6
HEAD_DIM = 64
SGU_CHUNK = 128
RET_CHUNK = 128
ATTN_BLOCK = 128
DILATIONS = (1, 4, 16)
ATTN_UNROLL = 32
ATTN_SLAB = 256
ROPE_BASE = 10000.0
N_GROUPS = 4
EXPERTS_PER_GROUP = 4
N_EXPERTS = N_GROUPS * EXPERTS_PER_GROUP
_PAIRS = tuple(itertools.combinations(range(EXPERTS_PER_GROUP), 2))
N_CLASSES = N_GROUPS * len(_PAIRS)
LANES = 128
SUBLANES = 8
PACKED_ROWS = 16
MOE_TILE = 256
ROW_UNROLL = 8
DISPATCH_PARTS = 4
CAST_SLAB = 256
OUTPROJ_CHAIN_ROWS = 512
VMEM_LIMIT = 56 * 1024 * 1024


def _params(sem, vmem=VMEM_LIMIT):
    return pltpu.CompilerParams(dimension_semantics=sem, vmem_limit_bytes=vmem)


def _rms(x, g):
    return x * lax.rsqrt(jnp.mean(x * x, axis=-1, keepdims=True) + EPS) * g


def _split_bf16(a):
    hi = a.astype(BF16)
    lo = (a - hi.astype(F32)).astype(BF16)
    return hi, lo


def _cast_rows(dst_ref, src_ref, rows=CAST_SLAB):
    for r0 in range(0, src_ref.shape[0], rows):
        dst_ref[r0:r0 + rows, :] = src_ref[r0:r0 + rows, :].astype(dst_ref.dtype)


def _gelu_tanh(x):
    return 0.5 * x * (1.0 + jnp.tanh(np.sqrt(2.0 / np.pi).astype(np.float32) * (x + 0.044715 * (x * x * x))))


def _spatial_gating(za, lng_ref, lnb_ref, wcat_ref, bias_ref, o_ref):
    ga = _gelu_tanh(za)
    aw = ga.shape[1] // 2
    u = ga[:, :aw]
    v = ga[:, aw:]
    mu = jnp.mean(v, axis=-1, keepdims=True)
    dv = v - mu
    var = jnp.mean(dv * dv, axis=-1, keepdims=True)
    vn = dv * lax.rsqrt(var + EPS) * lng_ref[...] + lnb_ref[...]
    group = lax.broadcasted_iota(I32, (1, aw), 1) // HEAD_DIM
    n_groups = aw // HEAD_DIM
    for c in range(ga.shape[0] // SGU_CHUNK):
        rows = slice(c * SGU_CHUNK, (c + 1) * SGU_CHUNK)
        vc = vn[rows]
        stack = jnp.concatenate([jnp.where(group == g, vc, 0.0) for g in range(n_groups)], axis=0)
        s = jnp.dot(wcat_ref[...], stack.astype(BF16), preferred_element_type=F32) + bias_ref[...]
        o_ref[rows, :] = (u[rows] * s).astype(o_ref.dtype)


def _inproj_body(x_ref, g_ref, w_ref, lng_ref, lnb_ref, wcat_ref, bias_ref, a_ref, qkv_ref, gate_ref, wb,
                 *, n_chunk):
    @pl.when(pl.program_id(0) == 0)
    def _():
        _cast_rows(wb, w_ref)

    hb = _rms(x_ref[...], g_ref[...]).astype(BF16)
    col = 2 * a_ref.shape[1]
    _spatial_gating(jnp.dot(hb, wb[:, 0:col], preferred_element_type=F32), lng_ref, lnb_ref, wcat_ref, bias_ref,
                    a_ref)
    for o_ref in (qkv_ref, gate_ref):
        for n0 in range(0, o_ref.shape[1], n_chunk):
            n1 = min(n0 + n_chunk, o_ref.shape[1])
            acc = jnp.dot(hb, wb[:, col + n0:col + n1], preferred_element_type=F32)
            o_ref[:, n0:n1] = acc.astype(o_ref.dtype)
        col += o_ref.shape[1]


def _layer_resident(stacked, layer):
    rest = stacked.shape[1:]
    return pl.BlockSpec((None,) + rest, lambda *_: (layer,) + (0,) * len(rest), pipeline_mode=pl.Buffered(1))


def _inproj(x, g, w_stack, layer, ln_g, ln_b, w_s, b_s, qkv_width, gate_width, tm=1024, n_chunk=512):
    t, d = x.shape
    n_groups = w_s.shape[0]
    aw = n_groups * HEAD_DIM
    assert 2 * aw + qkv_width + gate_width == w_stack.shape[2]
    causal = jnp.tril(jnp.ones((SGU_CHUNK, SGU_CHUNK), dtype=bool))
    wcat = jnp.where(causal[None], w_s, 0.0).transpose(1, 0, 2).reshape(SGU_CHUNK, n_groups * SGU_CHUNK)
    bias = jnp.repeat(b_s.T, HEAD_DIM, axis=1)
    full = lambda shape: pl.BlockSpec(shape, lambda i: (0, 0))
    widths = (aw, qkv_width, gate_width)
    dtypes = (BF16, BF16, F32)
    return pl.pallas_call(
        functools.partial(_inproj_body, n_chunk=n_chunk),
        grid=(t // tm,),
        in_specs=[pl.BlockSpec((tm, d), lambda i: (i, 0)), full((1, d)), _layer_resident(w_stack, layer),
                  full((1, aw)), full((1, aw)), full(wcat.shape), full(bias.shape)],
        out_specs=[pl.BlockSpec((tm, n), lambda i: (i, 0)) for n in widths],
        out_shape=[jax.ShapeDtypeStruct((t, n), dt) for n, dt in zip(widths, dtypes)],
        scratch_shapes=[pltpu.VMEM(w_stack.shape[1:], BF16)],
        compiler_params=_params(("arbitrary",)),
        name="inproj_sgu",
    )(x, g.reshape(1, d), w_stack, ln_g.reshape(1, aw), ln_b.reshape(1, aw), wcat.astype(BF16), bias)


def _attn_bias():
    blk = ATTN_BLOCK
    qi = np.arange(2 * blk)[:, None] % blk
    ci = np.arange(2 * blk)[None, :]
    first = (ci < blk) & (ci <= qi)
    later = np.where(ci < blk, ci >= qi, ci - blk <= qi)
    return np.where(np.stack([first, later]), 0.0, -np.inf).astype(np.float32)


def _attn_body(q_ref, k_ref, v_ref, bias_ref, o_ref, qkv_scr, ob_scr, m_scr, l_scr, *, seq, unroll):
    blk = ATTN_BLOCK
    head0 = lax.broadcasted_iota(I32, (1, LANES), 1) < HEAD_DIM
    scale = HEAD_DIM ** -0.5 * np.log2(np.e)

    widen_rows = ATTN_SLAB

    def widen(i, carry):
        sl = pl.ds(pl.multiple_of(i * widen_rows, widen_rows), widen_rows)
        qkv_scr[0, 0, sl, :] = q_ref[0, sl, :].astype(F32) * scale
        qkv_scr[0, 1, sl, :] = k_ref[0, sl, :].astype(F32)
        qkv_scr[0, 2, sl, :] = v_ref[0, sl, :].astype(F32)
        return carry

    lax.fori_loop(0, seq // widen_rows, widen, 0)

    src = {DILATIONS[0]: qkv_scr.at[0]}
    for stage, d in enumerate(DILATIONS[1:], start=1):
        prev_d, ratio, slab = DILATIONS[stage - 1], d // DILATIONS[stage - 1], seq // d
        prev, cur = qkv_scr.at[stage - 1], qkv_scr.at[stage]
        for r_prev in range(prev_d):
            for c in range(ratio):
                r = r_prev + prev_d * c
                for a in range(3):
                    cur[a, r * slab:(r + 1) * slab, :] = prev[a, pl.ds(r_prev * (seq // prev_d) + c, slab,
                                                                     stride=ratio), :]
        src[d] = cur

    def rows(start, n, d):
        return pl.ds(start, n) if d == 1 else pl.ds(start, n, stride=d)

    def block(p, d, r, n):
        base = n * (blk * d) + r
        first = r * (seq // d) + n * blk
        kfirst = r * (seq // d) + jnp.maximum(n - 1, 0) * blk
        first, kfirst = pl.multiple_of(first, blk), pl.multiple_of(kfirst, blk)
        qb = src[d][0, pl.ds(first, blk), :]
        qs = jnp.concatenate([jnp.where(head0, qb, 0.0), jnp.where(head0, 0.0, qb)], axis=0).astype(BF16)
        kb = src[d][1, pl.ds(kfirst, 2 * blk), :].astype(BF16)
        vb = src[d][2, pl.ds(kfirst, 2 * blk), :].astype(BF16)
        va = jnp.concatenate([vb, jnp.ones_like(vb)], axis=1)
        s = lax.dot_general(qs, kb, (((1,), (1,)), ((), ())), preferred_element_type=F32)
        s = s + bias_ref[jnp.minimum(n, 1)]
        m = jnp.max(s, axis=-1, keepdims=True)
        e = jnp.exp2(s - m)
        oa = jnp.dot(e.astype(BF16), va, preferred_element_type=F32)
        ob_scr[p, rows(base, blk, d), :] = jnp.where(head0, oa[:blk, :LANES], oa[blk:, :LANES])
        l_scr[p, rows(base, blk, d), :] = jnp.where(head0, oa[:blk, LANES:], oa[blk:, LANES:])
        m_scr[p, rows(base, blk, d), :] = jnp.where(head0, m[:blk], m[blk:])

    for p, d in enumerate(DILATIONS):
        def group(i, carry, p=p, d=d):
            for u in range(unroll):
                b = i * unroll + u
                block(p, d, b & (d - 1), b >> (d.bit_length() - 1))
            return carry

        lax.fori_loop(0, seq // (blk * unroll), group, 0)

    step = ATTN_SLAB

    def mix(i, carry):
        sl = pl.ds(pl.multiple_of(i * step, step), step)
        ms = [m_scr[p, sl, :] for p in range(len(DILATIONS))]
        top = functools.reduce(jnp.maximum, ms)
        es = [jnp.exp2(m - top) for m in ms]
        num = sum(e * ob_scr[p, sl, :] for p, e in enumerate(es))
        den = sum(e * l_scr[p, sl, :] for p, e in enumerate(es))
        o_ref[0, sl, :] = (num / den).astype(o_ref.dtype)
        return carry

    lax.fori_loop(0, seq // step, mix, 0)


def _attn(z3, q_col, n_heads):
    b, s, _ = z3.shape
    n_pairs = n_heads * HEAD_DIM // LANES
    assert s % (ATTN_BLOCK * ATTN_UNROLL) == 0 and s % (2 * ATTN_BLOCK * max(DILATIONS)) == 0
    assert all(d & (d - 1) == 0 for d in DILATIONS)
    bias = jnp.asarray(_attn_bias())

    def spec(off):
        return pl.BlockSpec((1, s, LANES), lambda i, j: (i, 0, off + j))

    return pl.pallas_call(
        functools.partial(_attn_body, seq=s, unroll=ATTN_UNROLL),
        grid=(b, n_pairs),
        in_specs=[spec(q_col), spec(q_col + n_pairs), spec(q_col + 2 * n_pairs),
                  pl.BlockSpec(bias.shape, lambda i, j: (0, 0, 0))],
        out_specs=pl.BlockSpec((1, s, LANES), lambda i, j: (i, 0, j)),
        out_shape=jax.ShapeDtypeStruct((b, s, n_pairs * LANES), BF16),
        scratch_shapes=[pltpu.VMEM((len(DILATIONS), 3, s, LANES), F32),
                        pltpu.VMEM((len(DILATIONS), s, LANES), F32),
                        pltpu.VMEM((len(DILATIONS), s, LANES), F32),
                        pltpu.VMEM((len(DILATIONS), s, LANES), F32)],
        compiler_params=_params(("parallel", "parallel")),
        name="dilated_attn",
    )(z3, z3, z3, bias)


def _ret_body(q_ref, k_ref, v_ref, g_ref, cos_ref, sa_ref, sb_ref, dec_ref, qdec_ref, kdec_ref, cdec_ref,
              avg_ref, gn_ref, o_ref, state):
    cw = q_ref.shape[2]
    n_heads = cw // HEAD_DIM
    head = lax.broadcasted_iota(I32, (1, cw), 1) // HEAD_DIM
    hr = lax.broadcasted_iota(I32, (cw, cw), 0) // HEAD_DIM
    hc = lax.broadcasted_iota(I32, (cw, cw), 1) // HEAD_DIM
    same_head = hr == hc

    @pl.when(pl.program_id(1) == 0)
    def _():
        state[...] = jnp.zeros_like(state)

    def rotary(x, cos, sa, sb):
        half = HEAD_DIM // 2
        return x * cos + pltpu.roll(x, half, 1) * sa + pltpu.roll(x, cw - half, 1) * sb

    def head_mean(a):
        hi, lo = _split_bf16(a)
        return (jnp.dot(hi, avg_ref[...], preferred_element_type=F32)
                + jnp.dot(lo, avg_ref[...], preferred_element_type=F32))

    for c in range(q_ref.shape[1] // RET_CHUNK):
        rows = slice(c * RET_CHUNK, (c + 1) * RET_CHUNK)
        cos, sa, sb = cos_ref[rows, :], sa_ref[rows, :], sb_ref[rows, :]
        qr = rotary(q_ref[0, rows, :].astype(F32), cos, sa, sb)
        kr = rotary(k_ref[0, rows, :].astype(F32), cos, sa, sb) * (HEAD_DIM ** -0.5)
        v = v_ref[0, rows, :].astype(F32)
        krb = kr.astype(BF16)
        inner = [lax.dot_general(jnp.where(head == h, qr, 0.0).astype(BF16), krb, (((1,), (1,)), ((), ())),
                                 preferred_element_type=F32) for h in range(n_heads)]
        inner = jnp.concatenate(inner, axis=1) * dec_ref[...]
        vstack = jnp.concatenate([jnp.where(head == h, v, 0.0) for h in range(n_heads)], axis=0)
        st = state[...]
        o = (jnp.dot(inner.astype(BF16), vstack.astype(BF16), preferred_element_type=F32)
             + jnp.dot((qr * qdec_ref[...]).astype(BF16), st.astype(BF16), preferred_element_type=F32))
        ktv = lax.dot_general((kr * kdec_ref[...]).astype(BF16), v.astype(BF16), (((0,), (0,)), ((), ())),
                              preferred_element_type=F32)
        state[...] = st * cdec_ref[...] + jnp.where(same_head, ktv, 0.0)
        mu = head_mean(o)
        dd = o - mu
        var = head_mean(dd * dd)
        on = dd * lax.rsqrt(var + EPS) * gn_ref[...]
        gate = g_ref[0, rows, :]
        o_ref[0, rows, :] = (gate * jax.nn.sigmoid(gate) * on).astype(o_ref.dtype)


def _ret_tables(seq, n_heads):
    half = HEAD_DIM // 2
    inv = ROPE_BASE ** (-jnp.arange(half, dtype=F32) / half)
    ang = jnp.arange(seq, dtype=F32)[:, None] * inv[None]
    cos, sin = jnp.cos(ang), jnp.sin(ang)
    zero = jnp.zeros_like(sin)
    tile = lambda a: jnp.tile(a, (1, n_heads))
    cos_t = tile(jnp.concatenate([cos, cos], axis=1))
    sa_t = tile(jnp.concatenate([zero, sin], axis=1))
    sb_t = tile(jnp.concatenate([-sin, zero], axis=1))
    log_g = jnp.log(1.0 - 2.0 ** (-5.0 - jnp.arange(n_heads, dtype=F32)))
    c = RET_CHUNK
    idx = jnp.arange(c)
    diff = idx[:, None] - idx[None, :]
    decay_in = jnp.where(diff >= 0, jnp.exp(log_g[:, None, None] * jnp.maximum(diff, 0)[None]), 0.0)
    dec = decay_in.transpose(1, 0, 2).reshape(c, n_heads * c)
    lane_head = jnp.repeat(jnp.arange(n_heads), HEAD_DIM)
    qdec = jnp.exp(log_g[lane_head][None, :] * (idx + 1)[:, None].astype(F32))
    kdec = jnp.exp(log_g[lane_head][None, :] * (c - 1 - idx)[:, None].astype(F32))
    same = lane_head[:, None] == lane_head[None, :]
    cdec = jnp.where(same, jnp.exp(log_g * c)[lane_head][:, None], 0.0)
    avg = jnp.where(same, 1.0 / HEAD_DIM, 0.0).astype(BF16)
    return cos_t, sa_t, sb_t, dec, qdec, kdec, cdec, avg


def _retention(z3, col, gate3, n_heads, gn, tc=2048):
    b, s, _ = z3.shape
    cw = n_heads * HEAD_DIM
    cos_t, sa_t, sb_t, dec, qdec, kdec, cdec, avg = _ret_tables(s, n_heads)

    def zspec(off):
        return pl.BlockSpec((1, tc, cw), lambda i, j: (i, j, off))

    tab = pl.BlockSpec((tc, cw), lambda i, j: (j, 0))
    full = lambda a: pl.BlockSpec(a.shape, lambda i, j: (0, 0))
    gn2 = gn.reshape(1, cw)
    return pl.pallas_call(
        _ret_body,
        grid=(b, s // tc),
        in_specs=[zspec(col), zspec(col + 1), zspec(col + 2), zspec(0), tab, tab, tab,
                  full(dec), full(qdec), full(kdec), full(cdec), full(avg), full(gn2)],
        out_specs=pl.BlockSpec((1, tc, cw), lambda i, j: (i, j, 0)),
        out_shape=jax.ShapeDtypeStruct((b, s, cw), BF16),
        scratch_shapes=[pltpu.VMEM((cw, cw), F32)],
        compiler_params=_params(("parallel", "arbitrary")),
        name="retention",
    )(z3, z3, z3, gate3, cos_t, sa_t, sb_t, dec, qdec, kdec, cdec, avg, gn2)


def _outproj_body(x_ref, a_ref, b_ref, c_ref, wo_ref, g2_ref, rhi_ref, rlo_ref, rb_ref, x1_ref, route_ref, wb):
    @pl.when(pl.program_id(0) == 0)
    def _():
        _cast_rows(wb, wo_ref)

    n_chains = x_ref.shape[0] // OUTPROJ_CHAIN_ROWS
    for ch in range(n_chains):
        rows = slice(ch * OUTPROJ_CHAIN_ROWS, (ch + 1) * OUTPROJ_CHAIN_ROWS)
        _outproj_rows(rows, x_ref, a_ref, b_ref, c_ref, wb, g2_ref, rhi_ref, rlo_ref, rb_ref, x1_ref, route_ref)


def _outproj_rows(rows, x_ref, a_ref, b_ref, c_ref, wb, g2_ref, rhi_ref, rlo_ref, rb_ref, x1_ref, route_ref):
    aw, bw = a_ref.shape[1], b_ref.shape[1]
    y = (jnp.dot(a_ref[rows, :], wb[0:aw, :], preferred_element_type=F32)
         + jnp.dot(b_ref[rows, :], wb[aw:aw + bw, :], preferred_element_type=F32)
         + jnp.dot(c_ref[rows, :], wb[aw + bw:, :], preferred_element_type=F32))
    x1 = x_ref[rows, :] + y
    x1_ref[rows, :] = x1
    hi, lo = _split_bf16(_rms(x1, g2_ref[...]))
    nt = lambda r, h: lax.dot_general(r, h, (((1,), (1,)), ((), ())), preferred_element_type=F32)
    lt = nt(rhi_ref[...], hi) + nt(rhi_ref[...], lo) + nt(rlo_ref[...], hi) + rb_ref[...]
    row = lambda i: lt[i:i + 1, :]
    best, gi = row(0), jnp.zeros_like(row(0), dtype=I32)
    for i in range(1, N_GROUPS):
        up = row(i) > best
        best = jnp.where(up, row(i), best)
        gi = jnp.where(up, i, gi)
    g_w = 1.0 / sum(jnp.exp(row(i) - best) for i in range(N_GROUPS))
    el = []
    for j in range(EXPERTS_PER_GROUP):
        e = row(N_GROUPS + (N_GROUPS - 1) * EXPERTS_PER_GROUP + j)
        for g in range(N_GROUPS - 2, -1, -1):
            e = jnp.where(gi == g, row(N_GROUPS + g * EXPERTS_PER_GROUP + j), e)
        el.append(e)
    v1, i1 = el[0], jnp.zeros_like(gi)
    for j in range(1, EXPERTS_PER_GROUP):
        up = el[j] > v1
        v1 = jnp.where(up, el[j], v1)
        i1 = jnp.where(up, j, i1)
    v2, i2 = jnp.full_like(v1, -jnp.inf), jnp.zeros_like(gi)
    for j in range(EXPERTS_PER_GROUP):
        up = jnp.logical_and(i1 != j, el[j] > v2)
        v2 = jnp.where(up, el[j], v2)
        i2 = jnp.where(up, j, i2)
    e21 = jnp.exp(v2 - v1)
    w1 = g_w / (1.0 + e21)
    w2 = g_w * e21 / (1.0 + e21)
    swap = i2 < i1
    e_lo, e_hi = jnp.where(swap, i2, i1), jnp.where(swap, i1, i2)
    w_lo, w_hi = jnp.where(swap, w2, w1), jnp.where(swap, w1, w2)
    first_pair = jnp.zeros_like(e_lo)
    for lo in range(1, EXPERTS_PER_GROUP - 1):
        first_pair = jnp.where(e_lo >= lo, _PAIRS.index((lo, lo + 1)), first_pair)
    cls = (gi * len(_PAIRS) + first_pair + e_hi - e_lo - 1).astype(F32)
    zero = jnp.zeros_like(w1)
    route_ref[:, rows] = jnp.concatenate([cls, w_lo, w_hi, zero, zero, zero, zero, zero], axis=0)


def _outproj(x, a, b, c, wo_stack, layer, g2, router_g, router_gb, router_e, router_eb, tm=1024):
    t, d = x.shape
    n_logits = N_GROUPS + N_EXPERTS
    n_rows = -(-n_logits // PACKED_ROWS) * PACKED_ROWS
    r = jnp.pad(jnp.concatenate([router_g, router_e], axis=1).T, ((0, n_rows - n_logits), (0, 0)))
    rhi = r.astype(BF16)
    rlo = (r - rhi.astype(F32)).astype(BF16)
    rb = jnp.pad(jnp.concatenate([router_gb, router_eb]), (0, n_rows - n_logits)).reshape(n_rows, 1)
    row_spec = lambda w: pl.BlockSpec((tm, w), lambda i: (i, 0))
    full = lambda arr: pl.BlockSpec(arr.shape, lambda i: (0, 0))
    g2r = g2.reshape(1, d)
    return pl.pallas_call(
        _outproj_body,
        grid=(t // tm,),
        in_specs=[row_spec(d), row_spec(a.shape[1]), row_spec(b.shape[1]), row_spec(c.shape[1]),
                  _layer_resident(wo_stack, layer), full(g2r), full(rhi), full(rlo), full(rb)],
        out_specs=[row_spec(d), pl.BlockSpec((8, tm), lambda i: (0, i))],
        out_shape=[jax.ShapeDtypeStruct((t, d), F32), jax.ShapeDtypeStruct((8, t), F32)],
        scratch_shapes=[pltpu.VMEM(wo_stack.shape[1:], BF16)],
        compiler_params=_params(("arbitrary",)),
        name="outproj_route",
    )(x, a, b, c, wo_stack, g2r, rhi, rlo, rb)


def _moe_plan(cls, n_tiles):
    onehot = (cls[:, None] == jnp.arange(N_CLASSES, dtype=I32)[None, :]).astype(I32)
    csum = jnp.cumsum(onehot, axis=0)
    counts = csum[-1]
    padded = ((counts + MOE_TILE - 1) // MOE_TILE) * MOE_TILE
    ends = jnp.cumsum(padded)
    starts = ends - padded
    pos = jnp.sum(onehot * (csum - 1 + starts[None, :]), axis=1)
    tile_start = jnp.arange(n_tiles, dtype=I32) * MOE_TILE
    tile_cls = jnp.sum((ends[None, :] <= tile_start[:, None]).astype(I32), axis=1)
    tile_cls = jnp.minimum(tile_cls, N_CLASSES - 1)
    group, pair = np.divmod(np.arange(N_CLASSES), len(_PAIRS))
    pairs = np.asarray(_PAIRS)
    lo_expert = jnp.asarray(group * EXPERTS_PER_GROUP + pairs[pair, 0], I32)[tile_cls]
    hi_expert = jnp.asarray(group * EXPERTS_PER_GROUP + pairs[pair, 1], I32)[tile_cls]
    used = (ends[-1] // MOE_TILE).astype(I32).reshape(1)
    pad_start = (starts + counts).astype(I32)
    pad_count = (padded - counts).astype(I32)
    return pos.astype(I32), lo_expert, hi_expert, used, pad_start, pad_count


def _store_token_major(ref, x, tok0=0):
    n = x.shape[0]
    for c in range(SUBLANES):
        ref[pl.ds(tok0 * SUBLANES + c, n, stride=SUBLANES), :] = x[:, c * LANES:(c + 1) * LANES]


def _load_token_major(ref, n, tok0=0):
    return jnp.concatenate([ref[pl.ds(tok0 * SUBLANES + c, n, stride=SUBLANES), :] for c in range(SUBLANES)],
                           axis=1)


def _tokens(ref, tok, n=1):
    return ref.at[pl.ds(pl.multiple_of(tok * SUBLANES, SUBLANES), n * SUBLANES)]


def _dispatch_body(pad_start_ref, pad_count_ref, used_ref, pos_ref, x_ref, g2_ref, xs_hbm, xt, zeros, sem, fill_sem,
                   *, tm, n_tiles):
    i = pl.program_id(0)
    half_tile = MOE_TILE // 2

    def fill(act):
        def class_pad(e, c):
            off, n = pad_start_ref[e], pad_count_ref[e]
            bit = half_tile
            while bit:
                take = (n & bit) != 0

                @pl.when(take)
                def _(off=off, bit=bit):
                    act(pltpu.make_async_copy(_tokens(zeros, 0, bit), _tokens(xs_hbm, off, bit), fill_sem))

                off = off + jnp.where(take, bit, 0)
                bit //= 2
            return c

        lax.fori_loop(0, N_CLASSES, class_pad, 0)

        def unused_half_tile(j, c):
            act(pltpu.make_async_copy(zeros, _tokens(xs_hbm, j * half_tile, half_tile), fill_sem))
            return c

        lax.fori_loop(2 * used_ref[0], 2 * n_tiles, unused_half_tile, 0)

    @pl.when(i == 0)
    def _():
        zeros[...] = jnp.zeros_like(zeros)
        fill(lambda cp: cp.start())
        fill(lambda cp: cp.wait())

    part = tm // DISPATCH_PARTS
    for q in range(DISPATCH_PARTS):
        _store_token_major(xt, _rms(x_ref[q * part:(q + 1) * part, :], g2_ref[...]), q * part)

        def issue(g, c, q=q):
            row = q * part + g * ROW_UNROLL
            for j in range(ROW_UNROLL):
                dst = _tokens(xs_hbm, pos_ref[i * tm + row + j])
                pltpu.make_async_copy(_tokens(xt, row + j), dst, sem).start(priority=j % 2)
            return c

        lax.fori_loop(0, part // ROW_UNROLL, issue, 0)
    pltpu.make_async_copy(xt, _tokens(xs_hbm, 0, tm), sem).wait()


def _dispatch(x1, g2, pos, n_tiles, pad_start, pad_count, used, tm=1024):
    t, d = x1.shape
    assert d == SUBLANES * LANES
    grid_spec = pltpu.PrefetchScalarGridSpec(
        num_scalar_prefetch=4,
        grid=(t // tm,),
        in_specs=[pl.BlockSpec((tm, d), lambda i, *_: (i, 0)),
                  pl.BlockSpec((1, d), lambda i, *_: (0, 0))],
        out_specs=pl.BlockSpec(memory_space=pl.ANY),
        scratch_shapes=[pltpu.VMEM((tm * SUBLANES, LANES), F32),
                        pltpu.VMEM((MOE_TILE // 2 * SUBLANES, LANES), F32),
                        pltpu.SemaphoreType.DMA, pltpu.SemaphoreType.DMA],
    )
    return pl.pallas_call(
        functools.partial(_dispatch_body, tm=tm, n_tiles=n_tiles),
        grid_spec=grid_spec,
        out_shape=jax.ShapeDtypeStruct((n_tiles * MOE_TILE * SUBLANES, LANES), F32),
        compiler_params=_params(("arbitrary",)),
        name="moe_dispatch",
    )(pad_start, pad_count, used, pos, x1, g2.reshape(1, d))


def _pack_bf16_pair(lo, hi):
    bits = lambda a: lax.bitcast_convert_type(a.astype(BF16).astype(F32), jnp.uint32)
    return (bits(lo) >> 16) | (bits(hi) & jnp.uint32(0xFFFF0000))


def _unpack_bf16_pair(word):
    lo = lax.bitcast_convert_type(word << 16, F32)
    hi = lax.bitcast_convert_type(word & jnp.uint32(0xFFFF0000), F32)
    return lo, hi


def _ffn_body(lo_ref, hi_ref, used_ref, xs_ref, *refs):
    w_refs, ys_ref, wgu_b, wd_b = refs[:6], refs[6], refs[7], refs[8]
    i = pl.program_id(0)
    live = i < used_ref[0]
    hid = wd_b.shape[1]
    prev = jnp.maximum(i - 1, 0)

    for k, e_ref in enumerate((lo_ref, hi_ref)):
        wg_ref, wu_ref, wd_ref = w_refs[3 * k:3 * k + 3]

        @pl.when(jnp.logical_and(live, jnp.logical_or(i == 0, e_ref[i] != e_ref[prev])))
        def _(k=k, wg_ref=wg_ref, wu_ref=wu_ref, wd_ref=wd_ref):
            _cast_rows(wgu_b.at[k, :, 0:hid], wg_ref)
            _cast_rows(wgu_b.at[k, :, hid:2 * hid], wu_ref)
            _cast_rows(wd_b.at[k], wd_ref)

    @pl.when(live)
    def _():
        hb = _load_token_major(xs_ref, MOE_TILE).astype(BF16)
        ys = []
        for k in range(2):
            gu = jnp.dot(hb, wgu_b[k], preferred_element_type=F32)
            gate, up = gu[:, :hid], gu[:, hid:]
            act = (gate * jax.nn.sigmoid(gate) * up).astype(BF16)
            ys.append(jnp.dot(act, wd_b[k], preferred_element_type=F32))
        _store_token_major(ys_ref, _pack_bf16_pair(*ys))

    @pl.when(jnp.logical_not(live))
    def _():
        ys_ref[...] = jnp.zeros_like(ys_ref)


def _ffn(xs, w_gate, w_up, w_down, layer, lo_expert, hi_expert, used):
    n_tiles = xs.shape[0] // (MOE_TILE * SUBLANES)
    d, hid = w_down.shape[3], w_down.shape[2]
    tile_rows = MOE_TILE * SUBLANES
    live = lambda i, used: jnp.minimum(i, used[0] - 1)

    def weights(pick):
        up = pl.BlockSpec((None, None, d, hid), lambda i, lo, hi, used: (layer, pick(lo, hi)[i], 0, 0))
        down = pl.BlockSpec((None, None, hid, d), lambda i, lo, hi, used: (layer, pick(lo, hi)[i], 0, 0))
        return [up, up, down]

    grid_spec = pltpu.PrefetchScalarGridSpec(
        num_scalar_prefetch=3,
        grid=(n_tiles,),
        in_specs=[pl.BlockSpec((tile_rows, LANES), lambda i, lo, hi, used: (live(i, used), 0))]
        + weights(lambda lo, hi: lo) + weights(lambda lo, hi: hi),
        out_specs=pl.BlockSpec((tile_rows, LANES), lambda i, lo, hi, used: (i, 0)),
        scratch_shapes=[pltpu.VMEM((2, d, 2 * hid), BF16), pltpu.VMEM((2, hid, d), BF16)],
    )
    return pl.pallas_call(
        _ffn_body,
        grid_spec=grid_spec,
        out_shape=jax.ShapeDtypeStruct(xs.shape, jnp.uint32),
        compiler_params=_params(("arbitrary",)),
        name="moe_experts",
    )(lo_expert, hi_expert, used, xs, w_gate, w_up, w_down, w_gate, w_up, w_down)


def _combine_body(pos_ref, ys_hbm, x1_ref, w_ref, gf_ref, o_ref, buf, sems, *, tm, final_norm):
    i = pl.program_id(0)
    slot = i % 2

    def gather(step, s):
        def issue(g, c):
            base = step * tm + g * ROW_UNROLL
            for j in range(ROW_UNROLL):
                src = _tokens(ys_hbm, pos_ref[base + j])
                pltpu.make_async_copy(src, _tokens(buf.at[s], g * ROW_UNROLL + j),
                                      sems.at[s]).start(priority=j % 2)
            return c

        lax.fori_loop(0, tm // ROW_UNROLL, issue, 0)

    @pl.when(i == 0)
    def _():
        gather(i, slot)

    @pl.when(i + 1 < pl.num_programs(0))
    def _():
        gather(i + 1, 1 - slot)

    pltpu.make_async_copy(_tokens(ys_hbm, 0, tm), buf.at[slot], sems.at[slot]).wait()
    w = w_ref[...]
    y_lo, y_hi = _unpack_bf16_pair(_load_token_major(buf.at[slot], tm))
    y = x1_ref[...] + w[:, 0:1] * y_lo + w[:, 1:2] * y_hi
    if final_norm:
        y = _rms(y, gf_ref[...])
    o_ref[...] = y


def _combine(ys, x1, pos, w2, gf, final_norm, tm=1024):
    t, d = x1.shape
    grid_spec = pltpu.PrefetchScalarGridSpec(
        num_scalar_prefetch=1,
        grid=(t // tm,),
        in_specs=[pl.BlockSpec(memory_space=pl.ANY),
                  pl.BlockSpec((tm, d), lambda i, *_: (i, 0)),
                  pl.BlockSpec((tm, 2), lambda i, *_: (i, 0)),
                  pl.BlockSpec((1, d), lambda i, *_: (0, 0))],
        out_specs=pl.BlockSpec((tm, d), lambda i, *_: (i, 0)),
        scratch_shapes=[pltpu.VMEM((2, tm * SUBLANES, LANES), jnp.uint32), pltpu.SemaphoreType.DMA((2,))],
    )
    return pl.pallas_call(
        functools.partial(_combine_body, tm=tm, final_norm=final_norm),
        grid_spec=grid_spec,
        out_shape=jax.ShapeDtypeStruct((t, d), F32),
        compiler_params=_params(("arbitrary",)),
        name="moe_combine",
    )(pos, ys, x1, w2, gf.reshape(1, d))


def _layer(x, batch, seq, layer, norm1, w_in, a_ln_g, a_ln_b, a_ws, a_bs, ret_gn, w_out, norm2,
           router_g, router_gb, router_e, router_eb, w_gate, w_up, w_down, final_gain, is_last):
    t, d = x.shape
    a_groups = a_ws.shape[0]
    aw = a_groups * HEAD_DIM
    c_heads = ret_gn.shape[0] // HEAD_DIM
    cw = c_heads * HEAD_DIM
    bw = w_in.shape[2] - 2 * aw - 4 * cw
    b_heads = (bw // 3) // HEAD_DIM
    a_out, zqkv, zg = _inproj(x, norm1, w_in, layer, a_ln_g, a_ln_b, a_ws, a_bs, bw + 3 * cw, cw)
    zqkv3 = zqkv.reshape(batch, seq, zqkv.shape[1])
    b_out = _attn(zqkv3, 0, b_heads).reshape(t, bw // 3)
    c_out = _retention(zqkv3, bw // cw, zg.reshape(batch, seq, cw), c_heads, ret_gn).reshape(t, cw)
    x1, route = _outproj(x, a_out, b_out, c_out, w_out, layer, norm2,
                         router_g, router_gb, router_e, router_eb)
    cls = route[0].astype(I32)
    w2 = route[1:3].T
    n_tiles = t // MOE_TILE + N_CLASSES
    pos, lo_expert, hi_expert, used, pad_start, pad_count = _moe_plan(cls, n_tiles)
    xs = _dispatch(x1, norm2, pos, n_tiles, pad_start, pad_count, used)
    ys = _ffn(xs, w_gate, w_up, w_down, layer, lo_expert, hi_expert, used)
    return _combine(ys, x1, pos, w2, final_gain, is_last)


def kernel(x, norm1, w_in, a_ln_g, a_ln_b, a_ws, a_bs, ret_gn, w_out, norm2, router_g, router_gb, router_e,
           router_eb, w_gate, w_up, w_down, final_norm):
    batch, seq, d = x.shape
    depth = norm1.shape[0]
    h = x.reshape(batch * seq, d)
    for l in range(depth):
        h = _layer(h, batch, seq, l, norm1[l], w_in, a_ln_g[l], a_ln_b[l], a_ws[l], a_bs[l], ret_gn[l],
                   w_out, norm2[l], router_g[l], router_gb[l], router_e[l], router_eb[l],
                   w_gate, w_up, w_down, final_norm, l == depth - 1)
    return h.reshape(batch, seq, d)
```

```python
import functools
import itertools

import numpy as np
import jax
import jax.numpy as jnp
from jax import lax
from jax.experimental import pallas as pl
from jax.experimental.pallas import tpu as pltpu

F32 = jnp.float32
BF16 = jnp.bfloat16
I32 = jnp.int32

EPS = 1e-6
HEAD_DIM = 64
SGU_CHUNK = 128
RET_CHUNK = 128
RET_GROUP = 4
ATTN_BLOCK = 128
DILATIONS = (1, 4, 16)
ATTN_UNROLL = 32
ATTN_SLAB = 256
ROPE_BASE = 10000.0
N_GROUPS = 4
EXPERTS_PER_GROUP = 4
N_EXPERTS = N_GROUPS * EXPERTS_PER_GROUP
_PAIRS = tuple(itertools.combinations(range(EXPERTS_PER_GROUP), 2))
N_CLASSES = N_GROUPS * len(_PAIRS)
LANES = 128
SUBLANES = 8
PACKED_ROWS = 16
MOE_TILE = 256
ROW_UNROLL = 8
CAST_SLAB = 256
OUTPROJ_CHAIN_ROWS = 512
VMEM_LIMIT = 56 * 1024 * 1024


def _params(sem, vmem=VMEM_LIMIT):
    return pltpu.CompilerParams(dimension_semantics=sem, vmem_limit_bytes=vmem)


def _rms(x, g):
    return x * lax.rsqrt(jnp.mean(x * x, axis=-1, keepdims=True) + EPS) * g


def _split_bf16(a):
    hi = a.astype(BF16)
    lo = (a - hi.astype(F32)).astype(BF16)
    return hi, lo


def _cast_rows(dst_ref, src_ref, rows=CAST_SLAB):
    for r0 in range(0, src_ref.shape[0], rows):
        dst_ref[r0:r0 + rows, :] = src_ref[r0:r0 + rows, :].astype(dst_ref.dtype)


def _gelu_tanh(x):
    return 0.5 * x * (1.0 + jnp.tanh(np.sqrt(2.0 / np.pi).astype(np.float32) * (x + 0.044715 * (x * x * x))))


def _spatial_gating(za, lng_ref, lnb_ref, wcat_ref, bias_ref, o_ref):
    ga = _gelu_tanh(za)
    aw = ga.shape[1] // 2
    u = ga[:, :aw]
    v = ga[:, aw:]
    mu = jnp.mean(v, axis=-1, keepdims=True)
    dv = v - mu
    var = jnp.mean(dv * dv, axis=-1, keepdims=True)
    vn = dv * lax.rsqrt(var + EPS) * lng_ref[...] + lnb_ref[...]
    group = lax.broadcasted_iota(I32, (1, aw), 1) // HEAD_DIM
    n_groups = aw // HEAD_DIM
    for c in range(ga.shape[0] // SGU_CHUNK):
        rows = slice(c * SGU_CHUNK, (c + 1) * SGU_CHUNK)
        vc = vn[rows]
        stack = jnp.concatenate([jnp.where(group == g, vc, 0.0) for g in range(n_groups)], axis=0)
        s = jnp.dot(wcat_ref[...], stack.astype(BF16), preferred_element_type=F32) + bias_ref[...]
        o_ref[rows, :] = (u[rows] * s).astype(o_ref.dtype)


def _inproj_body(x_ref, g_ref, w_ref, lng_ref, lnb_ref, wcat_ref, bias_ref, a_ref, qkv_ref, gate_ref, wb,
                 *, n_chunk):
    @pl.when(pl.program_id(0) == 0)
    def _():
        _cast_rows(wb, w_ref)

    hb = _rms(x_ref[...], g_ref[...]).astype(BF16)
    col = 2 * a_ref.shape[1]
    _spatial_gating(jnp.dot(hb, wb[:, 0:col], preferred_element_type=F32), lng_ref, lnb_ref, wcat_ref, bias_ref,
                    a_ref)
    for o_ref in (qkv_ref, gate_ref):
        for n0 in range(0, o_ref.shape[1], n_chunk):
            n1 = min(n0 + n_chunk, o_ref.shape[1])
            acc = jnp.dot(hb, wb[:, col + n0:col + n1], preferred_element_type=F32)
            o_ref[:, n0:n1] = acc.astype(o_ref.dtype)
        col += o_ref.shape[1]


def _layer_resident(stacked, layer):
    rest = stacked.shape[1:]
    return pl.BlockSpec((None,) + rest, lambda *_: (layer,) + (0,) * len(rest), pipeline_mode=pl.Buffered(1))


def _inproj(x, g, w_stack, layer, ln_g, ln_b, w_s, b_s, qkv_width, gate_width, tm=1024, n_chunk=512):
    t, d = x.shape
    n_groups = w_s.shape[0]
    aw = n_groups * HEAD_DIM
    assert 2 * aw + qkv_width + gate_width == w_stack.shape[2]
    causal = jnp.tril(jnp.ones((SGU_CHUNK, SGU_CHUNK), dtype=bool))
    wcat = jnp.where(causal[None], w_s, 0.0).transpose(1, 0, 2).reshape(SGU_CHUNK, n_groups * SGU_CHUNK)
    bias = jnp.repeat(b_s.T, HEAD_DIM, axis=1)
    full = lambda shape: pl.BlockSpec(shape, lambda i: (0, 0))
    widths = (aw, qkv_width, gate_width)
    dtypes = (BF16, BF16, F32)
    return pl.pallas_call(
        functools.partial(_inproj_body, n_chunk=n_chunk),
        grid=(t // tm,),
        in_specs=[pl.BlockSpec((tm, d), lambda i: (i, 0)), full((1, d)), _layer_resident(w_stack, layer),
                  full((1, aw)), full((1, aw)), full(wcat.shape), full(bias.shape)],
        out_specs=[pl.BlockSpec((tm, n), lambda i: (i, 0)) for n in widths],
        out_shape=[jax.ShapeDtypeStruct((t, n), dt) for n, dt in zip(widths, dtypes)],
        scratch_shapes=[pltpu.VMEM(w_stack.shape[1:], BF16)],
        compiler_params=_params(("arbitrary",)),
        name="inproj_sgu",
    )(x, g.reshape(1, d), w_stack, ln_g.reshape(1, aw), ln_b.reshape(1, aw), wcat.astype(BF16), bias)


def _attn_bias():
    blk = ATTN_BLOCK
    qi = np.arange(2 * blk)[:, None] % blk
    ci = np.arange(2 * blk)[None, :]
    first = (ci < blk) & (ci <= qi)
    later = np.where(ci < blk, ci >= qi, ci - blk <= qi)
    return np.where(np.stack([first, later]), 0.0, -np.inf).astype(np.float32)


def _attn_body(q_ref, k_ref, v_ref, bias_ref, o_ref, qkv_scr, ob_scr, m_scr, l_scr, *, seq, unroll):
    blk = ATTN_BLOCK
    head0 = lax.broadcasted_iota(I32, (1, LANES), 1) < HEAD_DIM
    scale = HEAD_DIM ** -0.5 * np.log2(np.e)

    widen_rows = ATTN_SLAB

    def widen(i, carry):
        sl = pl.ds(pl.multiple_of(i * widen_rows, widen_rows), widen_rows)
        qkv_scr[0, 0, sl, :] = q_ref[0, sl, :].astype(F32) * scale
        qkv_scr[0, 1, sl, :] = k_ref[0, sl, :].astype(F32)
        qkv_scr[0, 2, sl, :] = v_ref[0, sl, :].astype(F32)
        return carry

    lax.fori_loop(0, seq // widen_rows, widen, 0)

    src = {DILATIONS[0]: qkv_scr.at[0]}
    for stage, d in enumerate(DILATIONS[1:], start=1):
        prev_d, ratio, slab = DILATIONS[stage - 1], d // DILATIONS[stage - 1], seq // d
        prev, cur = qkv_scr.at[stage - 1], qkv_scr.at[stage]
        for r_prev in range(prev_d):
            for c in range(ratio):
                r = r_prev + prev_d * c
                for a in range(3):
                    cur[a, r * slab:(r + 1) * slab, :] = prev[a, pl.ds(r_prev * (seq // prev_d) + c, slab,
                                                                     stride=ratio), :]
        src[d] = cur

    def rows(start, n, d):
        return pl.ds(start, n) if d == 1 else pl.ds(start, n, stride=d)

    def block(p, d, r, n):
        base = n * (blk * d) + r
        first = r * (seq // d) + n * blk
        kfirst = r * (seq // d) + jnp.maximum(n - 1, 0) * blk
        first, kfirst = pl.multiple_of(first, blk), pl.multiple_of(kfirst, blk)
        qb = src[d][0, pl.ds(first, blk), :]
        qs = jnp.concatenate([jnp.where(head0, qb, 0.0), jnp.where(head0, 0.0, qb)], axis=0).astype(BF16)
        kb = src[d][1, pl.ds(kfirst, 2 * blk), :].astype(BF16)
        vb = src[d][2, pl.ds(kfirst, 2 * blk), :].astype(BF16)
        va = jnp.concatenate([vb, jnp.ones_like(vb)], axis=1)
        s = lax.dot_general(qs, kb, (((1,), (1,)), ((), ())), preferred_element_type=F32)
        s = s + bias_ref[jnp.minimum(n, 1)]
        m = jnp.max(s, axis=-1, keepdims=True)
        e = jnp.exp2(s - m)
        oa = jnp.dot(e.astype(BF16), va, preferred_element_type=F32)
        ob_scr[p, rows(base, blk, d), :] = jnp.where(head0, oa[:blk, :LANES], oa[blk:, :LANES])
        l_scr[p, rows(base, blk, d), :] = jnp.where(head0, oa[:blk, LANES:], oa[blk:, LANES:])
        m_scr[p, rows(base, blk, d), :] = jnp.where(head0, m[:blk], m[blk:])

    for p, d in enumerate(DILATIONS):
        def group(i, carry, p=p, d=d):
            for u in range(unroll):
                b = i * unroll + u
                block(p, d, b & (d - 1), b >> (d.bit_length() - 1))
            return carry

        lax.fori_loop(0, seq // (blk * unroll), group, 0)

    step = ATTN_SLAB

    def mix(i, carry):
        sl = pl.ds(pl.multiple_of(i * step, step), step)
        ms = [m_scr[p, sl, :] for p in range(len(DILATIONS))]
        top = functools.reduce(jnp.maximum, ms)
        es = [jnp.exp2(m - top) for m in ms]
        num = sum(e * ob_scr[p, sl, :] for p, e in enumerate(es))
        den = sum(e * l_scr[p, sl, :] for p, e in enumerate(es))
        o_ref[0, sl, :] = (num / den).astype(o_ref.dtype)
        return carry

    lax.fori_loop(0, seq // step, mix, 0)


def _attn(z3, q_col, n_heads):
    b, s, _ = z3.shape
    n_pairs = n_heads * HEAD_DIM // LANES
    assert s % (ATTN_BLOCK * ATTN_UNROLL) == 0 and s % (2 * ATTN_BLOCK * max(DILATIONS)) == 0
    assert all(d & (d - 1) == 0 for d in DILATIONS)
    bias = jnp.asarray(_attn_bias())

    def spec(off):
        return pl.BlockSpec((1, s, LANES), lambda i, j: (i, 0, off + j))

    return pl.pallas_call(
        functools.partial(_attn_body, seq=s, unroll=ATTN_UNROLL),
        grid=(b, n_pairs),
        in_specs=[spec(q_col), spec(q_col + n_pairs), spec(q_col + 2 * n_pairs),
                  pl.BlockSpec(bias.shape, lambda i, j: (0, 0, 0))],
        out_specs=pl.BlockSpec((1, s, LANES), lambda i, j: (i, 0, j)),
        out_shape=jax.ShapeDtypeStruct((b, s, n_pairs * LANES), BF16),
        scratch_shapes=[pltpu.VMEM((len(DILATIONS), 3, s, LANES), F32),
                        pltpu.VMEM((len(DILATIONS), s, LANES), F32),
                        pltpu.VMEM((len(DILATIONS), s, LANES), F32),
                        pltpu.VMEM((len(DILATIONS), s, LANES), F32)],
        compiler_params=_params(("parallel", "parallel")),
        name="dilated_attn",
    )(z3, z3, z3, bias)


def _ret_body(q_ref, k_ref, v_ref, g_ref, cos_ref, sa_ref, sb_ref, dec_ref, qdec_ref, kdec_ref, cdec_ref,
              avg_ref, gn_ref, o_ref, state):
    cw = q_ref.shape[2]
    n_heads = cw // HEAD_DIM
    head = lax.broadcasted_iota(I32, (1, cw), 1) // HEAD_DIM
    hr = lax.broadcasted_iota(I32, (cw, cw), 0) // HEAD_DIM
    hc = lax.broadcasted_iota(I32, (cw, cw), 1) // HEAD_DIM
    same_head = hr == hc

    @pl.when(pl.program_id(1) == 0)
    def _():
        state[...] = jnp.zeros_like(state)

    def rotary(x, cos, sa, sb):
        half = HEAD_DIM // 2
        return x * cos + pltpu.roll(x, half, 1) * sa + pltpu.roll(x, cw - half, 1) * sb

    def head_mean(a):
        hi, lo = _split_bf16(a)
        return (jnp.dot(hi, avg_ref[...], preferred_element_type=F32)
                + jnp.dot(lo, avg_ref[...], preferred_element_type=F32))

    n_chunks = q_ref.shape[1] // RET_CHUNK
    for g0 in range(0, n_chunks, RET_GROUP):
        group = range(g0, min(g0 + RET_GROUP, n_chunks))
        rows = {c: slice(c * RET_CHUNK, (c + 1) * RET_CHUNK) for c in group}
        qr, kr, v, inner, vstack, o = {}, {}, {}, {}, {}, {}
        for c in group:
            cos, sa, sb = cos_ref[rows[c], :], sa_ref[rows[c], :], sb_ref[rows[c], :]
            qr[c] = rotary(q_ref[0, rows[c], :].astype(F32), cos, sa, sb)
            kr[c] = rotary(k_ref[0, rows[c], :].astype(F32), cos, sa, sb) * (HEAD_DIM ** -0.5)
            v[c] = v_ref[0, rows[c], :].astype(F32)
        for c in group:
            krb = kr[c].astype(BF16)
            parts = [lax.dot_general(jnp.where(head == h, qr[c], 0.0).astype(BF16), krb, (((1,), (1,)), ((), ())),
                                     preferred_element_type=F32) for h in range(n_heads)]
            inner[c] = jnp.concatenate(parts, axis=1) * dec_ref[...]
            vstack[c] = jnp.concatenate([jnp.where(head == h, v[c], 0.0) for h in range(n_heads)], axis=0)
        ktv = {c: lax.dot_general((kr[c] * kdec_ref[...]).astype(BF16), v[c].astype(BF16), (((0,), (0,)), ((), ())),
                                  preferred_element_type=F32) for c in group}
        st = state[...]
        for c in group:
            o[c] = jnp.dot((qr[c] * qdec_ref[...]).astype(BF16), st.astype(BF16), preferred_element_type=F32)
            st = st * cdec_ref[...] + jnp.where(same_head, ktv[c], 0.0)
        state[...] = st
        for c in group:
            o[c] = o[c] + jnp.dot(inner[c].astype(BF16), vstack[c].astype(BF16), preferred_element_type=F32)
        mu = {c: head_mean(o[c]) for c in group}
        dd = {c: o[c] - mu[c] for c in group}
        var = {c: head_mean(dd[c] * dd[c]) for c in group}
        for c in group:
            on = dd[c] * lax.rsqrt(var[c] + EPS) * gn_ref[...]
            gate = g_ref[0, rows[c], :]
            o_ref[0, rows[c], :] = (gate * jax.nn.sigmoid(gate) * on).astype(o_ref.dtype)


def _ret_tables(seq, n_heads):
    half = HEAD_DIM // 2
    inv = ROPE_BASE ** (-jnp.arange(half, dtype=F32) / half)
    ang = jnp.arange(seq, dtype=F32)[:, None] * inv[None]
    cos, sin = jnp.cos(ang), jnp.sin(ang)
    zero = jnp.zeros_like(sin)
    tile = lambda a: jnp.tile(a, (1, n_heads))
    cos_t = tile(jnp.concatenate([cos, cos], axis=1))
    sa_t = tile(jnp.concatenate([zero, sin], axis=1))
    sb_t = tile(jnp.concatenate([-sin, zero], axis=1))
    log_g = jnp.log(1.0 - 2.0 ** (-5.0 - jnp.arange(n_heads, dtype=F32)))
    c = RET_CHUNK
    idx = jnp.arange(c)
    diff = idx[:, None] - idx[None, :]
    decay_in = jnp.where(diff >= 0, jnp.exp(log_g[:, None, None] * jnp.maximum(diff, 0)[None]), 0.0)
    dec = decay_in.transpose(1, 0, 2).reshape(c, n_heads * c)
    lane_head = jnp.repeat(jnp.arange(n_heads), HEAD_DIM)
    qdec = jnp.exp(log_g[lane_head][None, :] * (idx + 1)[:, None].astype(F32))
    kdec = jnp.exp(log_g[lane_head][None, :] * (c - 1 - idx)[:, None].astype(F32))
    same = lane_head[:, None] == lane_head[None, :]
    cdec = jnp.where(same, jnp.exp(log_g * c)[lane_head][:, None], 0.0)
    avg = jnp.where(same, 1.0 / HEAD_DIM, 0.0).astype(BF16)
    return cos_t, sa_t, sb_t, dec, qdec, kdec, cdec, avg


def _retention(z3, col, gate3, n_heads, gn, tc=2048):
    b, s, _ = z3.shape
    cw = n_heads * HEAD_DIM
    cos_t, sa_t, sb_t, dec, qdec, kdec, cdec, avg = _ret_tables(s, n_heads)

    def zspec(off):
        return pl.BlockSpec((1, tc, cw), lambda i, j: (i, j, off))

    tab = pl.BlockSpec((tc, cw), lambda i, j: (j, 0))
    full = lambda a: pl.BlockSpec(a.shape, lambda i, j: (0, 0))
    gn2 = gn.reshape(1, cw)
    return pl.pallas_call(
        _ret_body,
        grid=(b, s // tc),
        in_specs=[zspec(col), zspec(col + 1), zspec(col + 2), zspec(0), tab, tab, tab,
                  full(dec), full(qdec), full(kdec), full(cdec), full(avg), full(gn2)],
        out_specs=pl.BlockSpec((1, tc, cw), lambda i, j: (i, j, 0)),
        out_shape=jax.ShapeDtypeStruct((b, s, cw), BF16),
        scratch_shapes=[pltpu.VMEM((cw, cw), F32)],
        compiler_params=_params(("parallel", "arbitrary")),
        name="retention",
    )(z3, z3, z3, gate3, cos_t, sa_t, sb_t, dec, qdec, kdec, cdec, avg, gn2)


def _outproj_body(x_ref, a_ref, b_ref, c_ref, wo_ref, g2_ref, rhi_ref, rlo_ref, rb_ref, x1_ref, route_ref, wb):
    @pl.when(pl.program_id(0) == 0)
    def _():
        _cast_rows(wb, wo_ref)

    aw, bw = a_ref.shape[1], b_ref.shape[1]
    n_chains = x_ref.shape[0] // OUTPROJ_CHAIN_ROWS
    slabs = [slice(ch * OUTPROJ_CHAIN_ROWS, (ch + 1) * OUTPROJ_CHAIN_ROWS) for ch in range(n_chains)]
    ys = [jnp.dot(a_ref[rows, :], wb[0:aw, :], preferred_element_type=F32)
          + jnp.dot(b_ref[rows, :], wb[aw:aw + bw, :], preferred_element_type=F32)
          + jnp.dot(c_ref[rows, :], wb[aw + bw:, :], preferred_element_type=F32) for rows in slabs]
    x1s = [x_ref[rows, :] + y for rows, y in zip(slabs, ys)]
    for rows, x1 in zip(slabs, x1s):
        x1_ref[rows, :] = x1
    splits = [_split_bf16(_rms(x1, g2_ref[...])) for x1 in x1s]
    nt = lambda r, h: lax.dot_general(r, h, (((1,), (1,)), ((), ())), preferred_element_type=F32)
    lts = [nt(rhi_ref[...], hi) + nt(rhi_ref[...], lo) + nt(rlo_ref[...], hi) + rb_ref[...] for hi, lo in splits]
    for rows, lt in zip(slabs, lts):
        route_ref[:, rows] = _route(lt)


def _route(lt):
    row = lambda i: lt[i:i + 1, :]
    best, gi = row(0), jnp.zeros_like(row(0), dtype=I32)
    for i in range(1, N_GROUPS):
        up = row(i) > best
        best = jnp.where(up, row(i), best)
        gi = jnp.where(up, i, gi)
    g_w = 1.0 / sum(jnp.exp(row(i) - best) for i in range(N_GROUPS))
    el = []
    for j in range(EXPERTS_PER_GROUP):
        e = row(N_GROUPS + (N_GROUPS - 1) * EXPERTS_PER_GROUP + j)
        for g in range(N_GROUPS - 2, -1, -1):
            e = jnp.where(gi == g, row(N_GROUPS + g * EXPERTS_PER_GROUP + j), e)
        el.append(e)
    v1, i1 = el[0], jnp.zeros_like(gi)
    for j in range(1, EXPERTS_PER_GROUP):
        up = el[j] > v1
        v1 = jnp.where(up, el[j], v1)
        i1 = jnp.where(up, j, i1)
    v2, i2 = jnp.full_like(v1, -jnp.inf), jnp.zeros_like(gi)
    for j in range(EXPERTS_PER_GROUP):
        up = jnp.logical_and(i1 != j, el[j] > v2)
        v2 = jnp.where(up, el[j], v2)
        i2 = jnp.where(up, j, i2)
    e21 = jnp.exp(v2 - v1)
    w1 = g_w / (1.0 + e21)
    w2 = g_w * e21 / (1.0 + e21)
    swap = i2 < i1
    e_lo, e_hi = jnp.where(swap, i2, i1), jnp.where(swap, i1, i2)
    w_lo, w_hi = jnp.where(swap, w2, w1), jnp.where(swap, w1, w2)
    first_pair = jnp.zeros_like(e_lo)
    for lo in range(1, EXPERTS_PER_GROUP - 1):
        first_pair = jnp.where(e_lo >= lo, _PAIRS.index((lo, lo + 1)), first_pair)
    cls = (gi * len(_PAIRS) + first_pair + e_hi - e_lo - 1).astype(F32)
    zero = jnp.zeros_like(w1)
    return jnp.concatenate([cls, w_lo, w_hi, zero, zero, zero, zero, zero], axis=0)


def _outproj(x, a, b, c, wo_stack, layer, g2, router_g, router_gb, router_e, router_eb, tm=2048):
    t, d = x.shape
    n_logits = N_GROUPS + N_EXPERTS
    n_rows = -(-n_logits // PACKED_ROWS) * PACKED_ROWS
    r = jnp.pad(jnp.concatenate([router_g, router_e], axis=1).T, ((0, n_rows - n_logits), (0, 0)))
    rhi = r.astype(BF16)
    rlo = (r - rhi.astype(F32)).astype(BF16)
    rb = jnp.pad(jnp.concatenate([router_gb, router_eb]), (0, n_rows - n_logits)).reshape(n_rows, 1)
    row_spec = lambda w: pl.BlockSpec((tm, w), lambda i: (i, 0))
    full = lambda arr: pl.BlockSpec(arr.shape, lambda i: (0, 0))
    g2r = g2.reshape(1, d)
    return pl.pallas_call(
        _outproj_body,
        grid=(t // tm,),
        in_specs=[row_spec(d), row_spec(a.shape[1]), row_spec(b.shape[1]), row_spec(c.shape[1]),
                  _layer_resident(wo_stack, layer), full(g2r), full(rhi), full(rlo), full(rb)],
        out_specs=[row_spec(d), pl.BlockSpec((8, tm), lambda i: (0, i))],
        out_shape=[jax.ShapeDtypeStruct((t, d), F32), jax.ShapeDtypeStruct((8, t), F32)],
        scratch_shapes=[pltpu.VMEM(wo_stack.shape[1:], BF16)],
        compiler_params=_params(("arbitrary",)),
        name="outproj_route",
    )(x, a, b, c, wo_stack, g2r, rhi, rlo, rb)


def _moe_plan(cls, n_tiles):
    onehot = (cls[:, None] == jnp.arange(N_CLASSES, dtype=I32)[None, :]).astype(I32)
    csum = jnp.cumsum(onehot, axis=0)
    counts = csum[-1]
    padded = ((counts + MOE_TILE - 1) // MOE_TILE) * MOE_TILE
    ends = jnp.cumsum(padded)
    starts = ends - padded
    pos = jnp.sum(onehot * (csum - 1 + starts[None, :]), axis=1)
    tile_start = jnp.arange(n_tiles, dtype=I32) * MOE_TILE
    tile_cls = jnp.sum((ends[None, :] <= tile_start[:, None]).astype(I32), axis=1)
    tile_cls = jnp.minimum(tile_cls, N_CLASSES - 1)
    group, pair = np.divmod(np.arange(N_CLASSES), len(_PAIRS))
    pairs = np.asarray(_PAIRS)
    lo_expert = jnp.asarray(group * EXPERTS_PER_GROUP + pairs[pair, 0], I32)[tile_cls]
    hi_expert = jnp.asarray(group * EXPERTS_PER_GROUP + pairs[pair, 1], I32)[tile_cls]
    used = (ends[-1] // MOE_TILE).astype(I32).reshape(1)
    pad_start = (starts + counts).astype(I32)
    pad_count = (padded - counts).astype(I32)
    return pos.astype(I32), lo_expert, hi_expert, used, pad_start, pad_count


def _store_token_major(ref, x, tok0=0):
    n = x.shape[0]
    for c in range(SUBLANES):
        ref[pl.ds(tok0 * SUBLANES + c, n, stride=SUBLANES), :] = x[:, c * LANES:(c + 1) * LANES]


def _load_token_major(ref, n, tok0=0):
    return jnp.concatenate([ref[pl.ds(tok0 * SUBLANES + c, n, stride=SUBLANES), :] for c in range(SUBLANES)],
                           axis=1)


def _tokens(ref, tok, n=1):
    return ref.at[pl.ds(pl.multiple_of(tok * SUBLANES, SUBLANES), n * SUBLANES)]


def _dispatch_body(pad_start_ref, pad_count_ref, used_ref, pos_ref, x_ref, g2_ref, xs_hbm, xt, zeros, sem, fill_sem,
                   *, tm, n_tiles):
    i = pl.program_id(0)
    half_tile = MOE_TILE // 2

    def fill(act):
        def class_pad(e, c):
            off, n = pad_start_ref[e], pad_count_ref[e]
            bit = half_tile
            while bit:
                take = (n & bit) != 0

                @pl.when(take)
                def _(off=off, bit=bit):
                    act(pltpu.make_async_copy(_tokens(zeros, 0, bit), _tokens(xs_hbm, off, bit), fill_sem))

                off = off + jnp.where(take, bit, 0)
                bit //= 2
            return c

        lax.fori_loop(0, N_CLASSES, class_pad, 0)

        def unused_half_tile(j, c):
            act(pltpu.make_async_copy(zeros, _tokens(xs_hbm, j * half_tile, half_tile), fill_sem))
            return c

        lax.fori_loop(2 * used_ref[0], 2 * n_tiles, unused_half_tile, 0)

    @pl.when(i == 0)
    def _():
        zeros[...] = jnp.zeros_like(zeros)
        fill(lambda cp: cp.start())
        fill(lambda cp: cp.wait())

    _store_token_major(xt, _rms(x_ref[...], g2_ref[...]))

    def issue(g, c):
        base = i * tm + g * ROW_UNROLL
        for j in range(ROW_UNROLL):
            dst = _tokens(xs_hbm, pos_ref[base + j])
            pltpu.make_async_copy(_tokens(xt, g * ROW_UNROLL + j), dst, sem).start(priority=j % 2)
        return c

    lax.fori_loop(0, tm // ROW_UNROLL, issue, 0)
    pltpu.make_async_copy(xt, _tokens(xs_hbm, 0, tm), sem).wait()


def _dispatch(x1, g2, pos, n_tiles, pad_start, pad_count, used, tm=1024):
    t, d = x1.shape
    assert d == SUBLANES * LANES
    grid_spec = pltpu.PrefetchScalarGridSpec(
        num_scalar_prefetch=4,
        grid=(t // tm,),
        in_specs=[pl.BlockSpec((tm, d), lambda i, *_: (i, 0)),
                  pl.BlockSpec((1, d), lambda i, *_: (0, 0))],
        out_specs=pl.BlockSpec(memory_space=pl.ANY),
        scratch_shapes=[pltpu.VMEM((tm * SUBLANES, LANES), F32),
                        pltpu.VMEM((MOE_TILE // 2 * SUBLANES, LANES), F32),
                        pltpu.SemaphoreType.DMA, pltpu.SemaphoreType.DMA],
    )
    return pl.pallas_call(
        functools.partial(_dispatch_body, tm=tm, n_tiles=n_tiles),
        grid_spec=grid_spec,
        out_shape=jax.ShapeDtypeStruct((n_tiles * MOE_TILE * SUBLANES, LANES), F32),
        compiler_params=_params(("arbitrary",)),
        name="moe_dispatch",
    )(pad_start, pad_count, used, pos, x1, g2.reshape(1, d))


def _pack_bf16_pair(lo, hi):
    bits = lambda a: lax.bitcast_convert_type(a.astype(BF16).astype(F32), jnp.uint32)
    return (bits(lo) >> 16) | (bits(hi) & jnp.uint32(0xFFFF0000))


def _unpack_bf16_pair(word):
    lo = lax.bitcast_convert_type(word << 16, F32)
    hi = lax.bitcast_convert_type(word & jnp.uint32(0xFFFF0000), F32)
    return lo, hi


def _ffn_body(lo_ref, hi_ref, used_ref, xs_ref, *refs):
    w_refs, ys_ref, wgu_b, wd_b = refs[:6], refs[6], refs[7], refs[8]
    i = pl.program_id(0)
    live = i < used_ref[0]
    hid = wd_b.shape[1]
    prev = jnp.maximum(i - 1, 0)

    for k, e_ref in enumerate((lo_ref, hi_ref)):
        wg_ref, wu_ref, wd_ref = w_refs[3 * k:3 * k + 3]

        @pl.when(jnp.logical_and(live, jnp.logical_or(i == 0, e_ref[i] != e_ref[prev])))
        def _(k=k, wg_ref=wg_ref, wu_ref=wu_ref, wd_ref=wd_ref):
            _cast_rows(wgu_b.at[k, :, 0:hid], wg_ref)
            _cast_rows(wgu_b.at[k, :, hid:2 * hid], wu_ref)
            _cast_rows(wd_b.at[k], wd_ref)

    @pl.when(live)
    def _():
        hb = _load_token_major(xs_ref, MOE_TILE).astype(BF16)
        gus = [jnp.dot(hb, wgu_b[k], preferred_element_type=F32) for k in range(2)]
        acts = [(gu[:, :hid] * jax.nn.sigmoid(gu[:, :hid]) * gu[:, hid:]).astype(BF16) for gu in gus]
        ys = [jnp.dot(acts[k], wd_b[k], preferred_element_type=F32) for k in range(2)]
        _store_token_major(ys_ref, _pack_bf16_pair(*ys))

    @pl.when(jnp.logical_not(live))
    def _():
        ys_ref[...] = jnp.zeros_like(ys_ref)


def _ffn(xs, w_gate, w_up, w_down, layer, lo_expert, hi_expert, used):
    n_tiles = xs.shape[0] // (MOE_TILE * SUBLANES)
    d, hid = w_down.shape[3], w_down.shape[2]
    tile_rows = MOE_TILE * SUBLANES
    live = lambda i, used: jnp.minimum(i, used[0] - 1)

    def weights(pick):
        up = pl.BlockSpec((None, None, d, hid), lambda i, lo, hi, used: (layer, pick(lo, hi)[i], 0, 0))
        down = pl.BlockSpec((None, None, hid, d), lambda i, lo, hi, used: (layer, pick(lo, hi)[i], 0, 0))
        return [up, up, down]

    grid_spec = pltpu.PrefetchScalarGridSpec(
        num_scalar_prefetch=3,
        grid=(n_tiles,),
        in_specs=[pl.BlockSpec((tile_rows, LANES), lambda i, lo, hi, used: (live(i, used), 0))]
        + weights(lambda lo, hi: lo) + weights(lambda lo, hi: hi),
        out_specs=pl.BlockSpec((tile_rows, LANES), lambda i, lo, hi, used: (i, 0)),
        scratch_shapes=[pltpu.VMEM((2, d, 2 * hid), BF16), pltpu.VMEM((2, hid, d), BF16)],
    )
    return pl.pallas_call(
        _ffn_body,
        grid_spec=grid_spec,
        out_shape=jax.ShapeDtypeStruct(xs.shape, jnp.uint32),
        compiler_params=_params(("arbitrary",)),
        name="moe_experts",
    )(lo_expert, hi_expert, used, xs, w_gate, w_up, w_down, w_gate, w_up, w_down)


def _combine_body(pos_ref, ys_hbm, x1_ref, w_ref, gf_ref, o_ref, buf, sems, *, tm, final_norm):
    i = pl.program_id(0)
    slot = i % 2

    def gather(step, s):
        def issue(g, c):
            base = step * tm + g * ROW_UNROLL
            for j in range(ROW_UNROLL):
                src = _tokens(ys_hbm, pos_ref[base + j])
                pltpu.make_async_copy(src, _tokens(buf.at[s], g * ROW_UNROLL + j),
                                      sems.at[s]).start(priority=j % 2)
            return c

        lax.fori_loop(0, tm // ROW_UNROLL, issue, 0)

    @pl.when(i == 0)
    def _():
        gather(i, slot)

    @pl.when(i + 1 < pl.num_programs(0))
    def _():
        gather(i + 1, 1 - slot)

    pltpu.make_async_copy(_tokens(ys_hbm, 0, tm), buf.at[slot], sems.at[slot]).wait()
    w = w_ref[...]
    y_lo, y_hi = _unpack_bf16_pair(_load_token_major(buf.at[slot], tm))
    y = x1_ref[...] + w[:, 0:1] * y_lo + w[:, 1:2] * y_hi
    if final_norm:
        y = _rms(y, gf_ref[...])
    o_ref[...] = y


def _combine(ys, x1, pos, w2, gf, final_norm, tm=512):
    t, d = x1.shape
    grid_spec = pltpu.PrefetchScalarGridSpec(
        num_scalar_prefetch=1,
        grid=(t // tm,),
        in_specs=[pl.BlockSpec(memory_space=pl.ANY),
                  pl.BlockSpec((tm, d), lambda i, *_: (i, 0)),
                  pl.BlockSpec((tm, 2), lambda i, *_: (i, 0)),
                  pl.BlockSpec((1, d), lambda i, *_: (0, 0))],
        out_specs=pl.BlockSpec((tm, d), lambda i, *_: (i, 0)),
        scratch_shapes=[pltpu.VMEM((2, tm * SUBLANES, LANES), jnp.uint32), pltpu.SemaphoreType.DMA((2,))],
    )
    return pl.pallas_call(
        functools.partial(_combine_body, tm=tm, final_norm=final_norm),
        grid_spec=grid_spec,
        out_shape=jax.ShapeDtypeStruct((t, d), F32),
        compiler_params=_params(("arbitrary",)),
        name="moe_combine",
    )(pos, ys, x1, w2, gf.reshape(1, d))


def _layer(x, batch, seq, layer, norm1, w_in, a_ln_g, a_ln_b, a_ws, a_bs, ret_gn, w_out, norm2,
           router_g, router_gb, router_e, router_eb, w_gate, w_up, w_down, final_gain, is_last):
    t, d = x.shape
    a_groups = a_ws.shape[0]
    aw = a_groups * HEAD_DIM
    c_heads = ret_gn.shape[0] // HEAD_DIM
    cw = c_heads * HEAD_DIM
    bw = w_in.shape[2] - 2 * aw - 4 * cw
    b_heads = (bw // 3) // HEAD_DIM
    a_out, zqkv, zg = _inproj(x, norm1, w_in, layer, a_ln_g, a_ln_b, a_ws, a_bs, bw + 3 * cw, cw)
    zqkv3 = zqkv.reshape(batch, seq, zqkv.shape[1])
    b_out = _attn(zqkv3, 0, b_heads).reshape(t, bw // 3)
    c_out = _retention(zqkv3, bw // cw, zg.reshape(batch, seq, cw), c_heads, ret_gn).reshape(t, cw)
    x1, route = _outproj(x, a_out, b_out, c_out, w_out, layer, norm2,
                         router_g, router_gb, router_e, router_eb)
    cls = route[0].astype(I32)
    w2 = route[1:3].T
    n_tiles = t // MOE_TILE + N_CLASSES
    pos, lo_expert, hi_expert, used, pad_start, pad_count = _moe_plan(cls, n_tiles)
    xs = _dispatch(x1, norm2, pos, n_tiles, pad_start, pad_count, used)
    ys = _ffn(xs, w_gate, w_up, w_down, layer, lo_expert, hi_expert, used)
    return _combine(ys, x1, pos, w2, final_gain, is_last)


def kernel(x, norm1, w_in, a_ln_g, a_ln_b, a_ws, a_bs, ret_gn, w_out, norm2, router_g, router_gb, router_e,
           router_eb, w_gate, w_up, w_down, final_norm):
    batch, seq, d = x.shape
    depth = norm1.shape[0]
    h = x.reshape(batch * seq, d)
    for l in range(depth):
        h = _layer(h, batch, seq, l, norm1[l], w_in, a_ln_g[l], a_ln_b[l], a_ws[l], a_bs[l], ret_gn[l],
                   w_out, norm2[l], router_g[l], router_gb[l], router_e[l], router_eb[l],
                   w_gate, w_up, w_down, final_norm, l == depth - 1)
    return h.reshape(batch, seq, d)
```

```python
import functools
import itertools

import numpy as np
import jax
import jax.numpy as jnp
from jax import lax
from jax.experimental import pallas as pl
from jax.experimental.pallas import tpu as pltpu

F32 = jnp.float32
BF16 = jnp.bfloat16
I32 = jnp.int32

EPS = 1e-6
HEAD_DIM = 64
SGU_CHUNK = 128
RET_CHUNK = 128
RET_GROUP = 4
ATTN_BLOCK = 128
DILATIONS = (1, 4, 16)
ATTN_UNROLL = 32
ATTN_SLAB = 256
ROPE_BASE = 10000.0
N_GROUPS = 4
EXPERTS_PER_GROUP = 4
N_EXPERTS = N_GROUPS * EXPERTS_PER_GROUP
_PAIRS = tuple(itertools.combinations(range(EXPERTS_PER_GROUP), 2))
N_CLASSES = N_GROUPS * len(_PAIRS)
LANES = 128
SUBLANES = 8
PACKED_ROWS = 16
MOE_TILE = 256
ROW_UNROLL = 8
CAST_SLAB = 256
OUTPROJ_CHAIN_ROWS = 512
VMEM_LIMIT = 56 * 1024 * 1024


def _params(sem, vmem=VMEM_LIMIT):
    return pltpu.CompilerParams(dimension_semantics=sem, vmem_limit_bytes=vmem)


def _rms(x, g):
    return x * lax.rsqrt(jnp.mean(x * x, axis=-1, keepdims=True) + EPS) * g


def _split_bf16(a):
    hi = a.astype(BF16)
    lo = (a - hi.astype(F32)).astype(BF16)
    return hi, lo


def _cast_rows(dst_ref, src_ref, rows=CAST_SLAB):
    for r0 in range(0, src_ref.shape[0], rows):
        dst_ref[r0:r0 + rows, :] = src_ref[r0:r0 + rows, :].astype(dst_ref.dtype)


def _gelu_tanh(x):
    return 0.5 * x * (1.0 + jnp.tanh(np.sqrt(2.0 / np.pi).astype(np.float32) * (x + 0.044715 * (x * x * x))))


def _spatial_gating_chunk(za, lng_ref, lnb_ref, wcat_ref, bias_ref):
    ga = _gelu_tanh(za)
    aw = ga.shape[1] // 2
    u = ga[:, :aw]
    v = ga[:, aw:]
    mu = jnp.mean(v, axis=-1, keepdims=True)
    dv = v - mu
    var = jnp.mean(dv * dv, axis=-1, keepdims=True)
    vn = dv * lax.rsqrt(var + EPS) * lng_ref[...] + lnb_ref[...]
    group = lax.broadcasted_iota(I32, (1, aw), 1) // HEAD_DIM
    stack = jnp.concatenate([jnp.where(group == g, vn, 0.0) for g in range(aw // HEAD_DIM)], axis=0)
    s = jnp.dot(wcat_ref[...], stack.astype(BF16), preferred_element_type=F32) + bias_ref[...]
    return u * s


def _inproj_body(x_ref, g_ref, w_ref, lng_ref, lnb_ref, wcat_ref, bias_ref, a_ref, qkv_ref, gate_ref, wb,
                 *, n_chunk):
    @pl.when(pl.program_id(0) == 0)
    def _():
        _cast_rows(wb, w_ref)

    hb = _rms(x_ref[...], g_ref[...]).astype(BF16)
    col = 2 * a_ref.shape[1]
    za = jnp.dot(hb, wb[:, 0:col], preferred_element_type=F32)
    pieces = [(o_ref, n0, min(n0 + n_chunk, o_ref.shape[1]))
              for o_ref in (qkv_ref, gate_ref) for n0 in range(0, o_ref.shape[1], n_chunk)]
    n_gating = za.shape[0] // SGU_CHUNK
    done = 0
    for p, (o_ref, n0, n1) in enumerate(pieces):
        base = col if o_ref is qkv_ref else col + qkv_ref.shape[1]
        acc = jnp.dot(hb, wb[:, base + n0:base + n1], preferred_element_type=F32)
        o_ref[:, n0:n1] = acc.astype(o_ref.dtype)
        upto = n_gating * (p + 1) // len(pieces)
        for c in range(done, upto):
            rows = slice(c * SGU_CHUNK, (c + 1) * SGU_CHUNK)
            a_ref[rows, :] = _spatial_gating_chunk(za[rows], lng_ref, lnb_ref, wcat_ref, bias_ref).astype(
                a_ref.dtype)
        done = upto


def _layer_resident(stacked, layer):
    rest = stacked.shape[1:]
    return pl.BlockSpec((None,) + rest, lambda *_: (layer,) + (0,) * len(rest), pipeline_mode=pl.Buffered(1))


def _inproj(x, g, w_stack, layer, ln_g, ln_b, w_s, b_s, qkv_width, gate_width, tm=1024, n_chunk=512):
    t, d = x.shape
    n_groups = w_s.shape[0]
    aw = n_groups * HEAD_DIM
    assert 2 * aw + qkv_width + gate_width == w_stack.shape[2]
    causal = jnp.tril(jnp.ones((SGU_CHUNK, SGU_CHUNK), dtype=bool))
    wcat = jnp.where(causal[None], w_s, 0.0).transpose(1, 0, 2).reshape(SGU_CHUNK, n_groups * SGU_CHUNK)
    bias = jnp.repeat(b_s.T, HEAD_DIM, axis=1)
    full = lambda shape: pl.BlockSpec(shape, lambda i: (0, 0))
    widths = (aw, qkv_width, gate_width)
    dtypes = (BF16, BF16, F32)
    return pl.pallas_call(
        functools.partial(_inproj_body, n_chunk=n_chunk),
        grid=(t // tm,),
        in_specs=[pl.BlockSpec((tm, d), lambda i: (i, 0)), full((1, d)), _layer_resident(w_stack, layer),
                  full((1, aw)), full((1, aw)), full(wcat.shape), full(bias.shape)],
        out_specs=[pl.BlockSpec((tm, n), lambda i: (i, 0)) for n in widths],
        out_shape=[jax.ShapeDtypeStruct((t, n), dt) for n, dt in zip(widths, dtypes)],
        scratch_shapes=[pltpu.VMEM(w_stack.shape[1:], BF16)],
        compiler_params=_params(("arbitrary",)),
        name="inproj_sgu",
    )(x, g.reshape(1, d), w_stack, ln_g.reshape(1, aw), ln_b.reshape(1, aw), wcat.astype(BF16), bias)


def _attn_bias():
    blk = ATTN_BLOCK
    qi = np.arange(2 * blk)[:, None] % blk
    ci = np.arange(2 * blk)[None, :]
    first = (ci < blk) & (ci <= qi)
    later = np.where(ci < blk, ci >= qi, ci - blk <= qi)
    return np.where(np.stack([first, later]), 0.0, -np.inf).astype(np.float32)


def _attn_body(q_ref, k_ref, v_ref, bias_ref, o_ref, qkv_scr, ob_scr, m_scr, l_scr, *, seq, unroll):
    blk = ATTN_BLOCK
    head0 = lax.broadcasted_iota(I32, (1, LANES), 1) < HEAD_DIM
    scale = HEAD_DIM ** -0.5 * np.log2(np.e)

    widen_rows = ATTN_SLAB

    def widen(i, carry):
        sl = pl.ds(pl.multiple_of(i * widen_rows, widen_rows), widen_rows)
        qkv_scr[0, 0, sl, :] = q_ref[0, sl, :].astype(F32) * scale
        qkv_scr[0, 1, sl, :] = k_ref[0, sl, :].astype(F32)
        qkv_scr[0, 2, sl, :] = v_ref[0, sl, :].astype(F32)
        return carry

    lax.fori_loop(0, seq // widen_rows, widen, 0)

    src = {DILATIONS[0]: qkv_scr.at[0]}
    for stage, d in enumerate(DILATIONS[1:], start=1):
        prev_d, ratio, slab = DILATIONS[stage - 1], d // DILATIONS[stage - 1], seq // d
        prev, cur = qkv_scr.at[stage - 1], qkv_scr.at[stage]
        for r_prev in range(prev_d):
            for c in range(ratio):
                r = r_prev + prev_d * c
                for a in range(3):
                    cur[a, r * slab:(r + 1) * slab, :] = prev[a, pl.ds(r_prev * (seq // prev_d) + c, slab,
                                                                     stride=ratio), :]
        src[d] = cur

    def rows(start, n, d):
        return pl.ds(start, n) if d == 1 else pl.ds(start, n, stride=d)

    def block(p, d, r, n):
        base = n * (blk * d) + r
        first = r * (seq // d) + n * blk
        kfirst = r * (seq // d) + jnp.maximum(n - 1, 0) * blk
        first, kfirst = pl.multiple_of(first, blk), pl.multiple_of(kfirst, blk)
        qb = src[d][0, pl.ds(first, blk), :]
        qs = jnp.concatenate([jnp.where(head0, qb, 0.0), jnp.where(head0, 0.0, qb)], axis=0).astype(BF16)
        kb = src[d][1, pl.ds(kfirst, 2 * blk), :].astype(BF16)
        vb = src[d][2, pl.ds(kfirst, 2 * blk), :].astype(BF16)
        va = jnp.concatenate([vb, jnp.ones_like(vb)], axis=1)
        s = lax.dot_general(qs, kb, (((1,), (1,)), ((), ())), preferred_element_type=F32)
        s = s + bias_ref[jnp.minimum(n, 1)]
        m = jnp.max(s, axis=-1, keepdims=True)
        e = jnp.exp2(s - m)
        oa = jnp.dot(e.astype(BF16), va, preferred_element_type=F32)
        ob_scr[p, rows(base, blk, d), :] = jnp.where(head0, oa[:blk, :LANES], oa[blk:, :LANES])
        l_scr[p, rows(base, blk, d), :] = jnp.where(head0, oa[:blk, LANES:], oa[blk:, LANES:])
        m_scr[p, rows(base, blk, d), :] = jnp.where(head0, m[:blk], m[blk:])

    for p, d in enumerate(DILATIONS):
        def group(i, carry, p=p, d=d):
            for u in range(unroll):
                b = i * unroll + u
                block(p, d, b & (d - 1), b >> (d.bit_length() - 1))
            return carry

        lax.fori_loop(0, seq // (blk * unroll), group, 0)

    step = ATTN_SLAB

    def mix(i, carry):
        sl = pl.ds(pl.multiple_of(i * step, step), step)
        ms = [m_scr[p, sl, :] for p in range(len(DILATIONS))]
        top = functools.reduce(jnp.maximum, ms)
        es = [jnp.exp2(m - top) for m in ms]
        num = sum(e * ob_scr[p, sl, :] for p, e in enumerate(es))
        den = sum(e * l_scr[p, sl, :] for p, e in enumerate(es))
        o_ref[0, sl, :] = (num / den).astype(o_ref.dtype)
        return carry

    lax.fori_loop(0, seq // step, mix, 0)


def _attn(z3, q_col, n_heads):
    b, s, _ = z3.shape
    n_pairs = n_heads * HEAD_DIM // LANES
    assert s % (ATTN_BLOCK * ATTN_UNROLL) == 0 and s % (2 * ATTN_BLOCK * max(DILATIONS)) == 0
    assert all(d & (d - 1) == 0 for d in DILATIONS)
    bias = jnp.asarray(_attn_bias())

    def spec(off):
        return pl.BlockSpec((1, s, LANES), lambda i, j: (i, 0, off + j))

    return pl.pallas_call(
        functools.partial(_attn_body, seq=s, unroll=ATTN_UNROLL),
        grid=(b, n_pairs),
        in_specs=[spec(q_col), spec(q_col + n_pairs), spec(q_col + 2 * n_pairs),
                  pl.BlockSpec(bias.shape, lambda i, j: (0, 0, 0))],
        out_specs=pl.BlockSpec((1, s, LANES), lambda i, j: (i, 0, j)),
        out_shape=jax.ShapeDtypeStruct((b, s, n_pairs * LANES), BF16),
        scratch_shapes=[pltpu.VMEM((len(DILATIONS), 3, s, LANES), F32),
                        pltpu.VMEM((len(DILATIONS), s, LANES), F32),
                        pltpu.VMEM((len(DILATIONS), s, LANES), F32),
                        pltpu.VMEM((len(DILATIONS), s, LANES), F32)],
        compiler_params=_params(("parallel", "parallel")),
        name="dilated_attn",
    )(z3, z3, z3, bias)


def _ret_body(q_ref, k_ref, v_ref, g_ref, cos_ref, sa_ref, sb_ref, dec_ref, qdec_ref, kdec_ref, cdec_ref,
              avg_ref, gn_ref, o_ref, state):
    cw = q_ref.shape[2]
    n_heads = cw // HEAD_DIM
    head = lax.broadcasted_iota(I32, (1, cw), 1) // HEAD_DIM
    hr = lax.broadcasted_iota(I32, (cw, cw), 0) // HEAD_DIM
    hc = lax.broadcasted_iota(I32, (cw, cw), 1) // HEAD_DIM
    same_head = hr == hc

    @pl.when(pl.program_id(1) == 0)
    def _():
        state[...] = jnp.zeros_like(state)

    def rotary(x, cos, sa, sb):
        half = HEAD_DIM // 2
        return x * cos + pltpu.roll(x, half, 1) * sa + pltpu.roll(x, cw - half, 1) * sb

    def head_mean(a):
        hi, lo = _split_bf16(a)
        return (jnp.dot(hi, avg_ref[...], preferred_element_type=F32)
                + jnp.dot(lo, avg_ref[...], preferred_element_type=F32))

    n_chunks = q_ref.shape[1] // RET_CHUNK
    for g0 in range(0, n_chunks, RET_GROUP):
        group = range(g0, min(g0 + RET_GROUP, n_chunks))
        rows = {c: slice(c * RET_CHUNK, (c + 1) * RET_CHUNK) for c in group}
        qr, kr, v, inner, vstack, o = {}, {}, {}, {}, {}, {}
        for c in group:
            cos, sa, sb = cos_ref[rows[c], :], sa_ref[rows[c], :], sb_ref[rows[c], :]
            qr[c] = rotary(q_ref[0, rows[c], :].astype(F32), cos, sa, sb)
            kr[c] = rotary(k_ref[0, rows[c], :].astype(F32), cos, sa, sb) * (HEAD_DIM ** -0.5)
            v[c] = v_ref[0, rows[c], :].astype(F32)
        for c in group:
            krb = kr[c].astype(BF16)
            parts = [lax.dot_general(jnp.where(head == h, qr[c], 0.0).astype(BF16), krb, (((1,), (1,)), ((), ())),
                                     preferred_element_type=F32) for h in range(n_heads)]
            inner[c] = jnp.concatenate(parts, axis=1) * dec_ref[...]
            vstack[c] = jnp.concatenate([jnp.where(head == h, v[c], 0.0) for h in range(n_heads)], axis=0)
        ktv = {c: lax.dot_general((kr[c] * kdec_ref[...]).astype(BF16), v[c].astype(BF16), (((0,), (0,)), ((), ())),
                                  preferred_element_type=F32) for c in group}
        st = state[...]
        for c in group:
            o[c] = jnp.dot((qr[c] * qdec_ref[...]).astype(BF16), st.astype(BF16), preferred_element_type=F32)
            st = st * cdec_ref[...] + jnp.where(same_head, ktv[c], 0.0)
        state[...] = st
        for c in group:
            o[c] = o[c] + jnp.dot(inner[c].astype(BF16), vstack[c].astype(BF16), preferred_element_type=F32)
        mu = {c: head_mean(o[c]) for c in group}
        dd = {c: o[c] - mu[c] for c in group}
        var = {c: head_mean(dd[c] * dd[c]) for c in group}
        for c in group:
            on = dd[c] * lax.rsqrt(var[c] + EPS) * gn_ref[...]
            gate = g_ref[0, rows[c], :]
            o_ref[0, rows[c], :] = (gate * jax.nn.sigmoid(gate) * on).astype(o_ref.dtype)


def _ret_tables(seq, n_heads):
    half = HEAD_DIM // 2
    inv = ROPE_BASE ** (-jnp.arange(half, dtype=F32) / half)
    ang = jnp.arange(seq, dtype=F32)[:, None] * inv[None]
    cos, sin = jnp.cos(ang), jnp.sin(ang)
    zero = jnp.zeros_like(sin)
    tile = lambda a: jnp.tile(a, (1, n_heads))
    cos_t = tile(jnp.concatenate([cos, cos], axis=1))
    sa_t = tile(jnp.concatenate([zero, sin], axis=1))
    sb_t = tile(jnp.concatenate([-sin, zero], axis=1))
    log_g = jnp.log(1.0 - 2.0 ** (-5.0 - jnp.arange(n_heads, dtype=F32)))
    c = RET_CHUNK
    idx = jnp.arange(c)
    diff = idx[:, None] - idx[None, :]
    decay_in = jnp.where(diff >= 0, jnp.exp(log_g[:, None, None] * jnp.maximum(diff, 0)[None]), 0.0)
    dec = decay_in.transpose(1, 0, 2).reshape(c, n_heads * c)
    lane_head = jnp.repeat(jnp.arange(n_heads), HEAD_DIM)
    qdec = jnp.exp(log_g[lane_head][None, :] * (idx + 1)[:, None].astype(F32))
    kdec = jnp.exp(log_g[lane_head][None, :] * (c - 1 - idx)[:, None].astype(F32))
    same = lane_head[:, None] == lane_head[None, :]
    cdec = jnp.where(same, jnp.exp(log_g * c)[lane_head][:, None], 0.0)
    avg = jnp.where(same, 1.0 / HEAD_DIM, 0.0).astype(BF16)
    return cos_t, sa_t, sb_t, dec, qdec, kdec, cdec, avg


def _retention(z3, col, gate3, n_heads, gn, tc=2048):
    b, s, _ = z3.shape
    cw = n_heads * HEAD_DIM
    cos_t, sa_t, sb_t, dec, qdec, kdec, cdec, avg = _ret_tables(s, n_heads)

    def zspec(off):
        return pl.BlockSpec((1, tc, cw), lambda i, j: (i, j, off))

    tab = pl.BlockSpec((tc, cw), lambda i, j: (j, 0))
    full = lambda a: pl.BlockSpec(a.shape, lambda i, j: (0, 0))
    gn2 = gn.reshape(1, cw)
    return pl.pallas_call(
        _ret_body,
        grid=(b, s // tc),
        in_specs=[zspec(col), zspec(col + 1), zspec(col + 2), zspec(0), tab, tab, tab,
                  full(dec), full(qdec), full(kdec), full(cdec), full(avg), full(gn2)],
        out_specs=pl.BlockSpec((1, tc, cw), lambda i, j: (i, j, 0)),
        out_shape=jax.ShapeDtypeStruct((b, s, cw), BF16),
        scratch_shapes=[pltpu.VMEM((cw, cw), F32)],
        compiler_params=_params(("parallel", "arbitrary")),
        name="retention",
    )(z3, z3, z3, gate3, cos_t, sa_t, sb_t, dec, qdec, kdec, cdec, avg, gn2)


def _outproj_body(x_ref, a_ref, b_ref, c_ref, wo_ref, g2_ref, rhi_ref, rlo_ref, rb_ref, x1_ref, route_ref, wb):
    @pl.when(pl.program_id(0) == 0)
    def _():
        _cast_rows(wb, wo_ref)

    aw, bw = a_ref.shape[1], b_ref.shape[1]
    n_chains = x_ref.shape[0] // OUTPROJ_CHAIN_ROWS
    slabs = [slice(ch * OUTPROJ_CHAIN_ROWS, (ch + 1) * OUTPROJ_CHAIN_ROWS) for ch in range(n_chains)]
    ys = [jnp.dot(a_ref[rows, :], wb[0:aw, :], preferred_element_type=F32)
          + jnp.dot(b_ref[rows, :], wb[aw:aw + bw, :], preferred_element_type=F32)
          + jnp.dot(c_ref[rows, :], wb[aw + bw:, :], preferred_element_type=F32) for rows in slabs]
    x1s = [x_ref[rows, :] + y for rows, y in zip(slabs, ys)]
    for rows, x1 in zip(slabs, x1s):
        x1_ref[rows, :] = x1
    splits = [_split_bf16(_rms(x1, g2_ref[...])) for x1 in x1s]
    nt = lambda r, h: lax.dot_general(r, h, (((1,), (1,)), ((), ())), preferred_element_type=F32)
    lts = [nt(rhi_ref[...], hi) + nt(rhi_ref[...], lo) + nt(rlo_ref[...], hi) + rb_ref[...] for hi, lo in splits]
    for rows, lt in zip(slabs, lts):
        route_ref[:, rows] = _route(lt)


def _route(lt):
    row = lambda i: lt[i:i + 1, :]
    best, gi = row(0), jnp.zeros_like(row(0), dtype=I32)
    for i in range(1, N_GROUPS):
        up = row(i) > best
        best = jnp.where(up, row(i), best)
        gi = jnp.where(up, i, gi)
    g_w = 1.0 / sum(jnp.exp(row(i) - best) for i in range(N_GROUPS))
    el = []
    for j in range(EXPERTS_PER_GROUP):
        e = row(N_GROUPS + (N_GROUPS - 1) * EXPERTS_PER_GROUP + j)
        for g in range(N_GROUPS - 2, -1, -1):
            e = jnp.where(gi == g, row(N_GROUPS + g * EXPERTS_PER_GROUP + j), e)
        el.append(e)
    v1, i1 = el[0], jnp.zeros_like(gi)
    for j in range(1, EXPERTS_PER_GROUP):
        up = el[j] > v1
        v1 = jnp.where(up, el[j], v1)
        i1 = jnp.where(up, j, i1)
    v2, i2 = jnp.full_like(v1, -jnp.inf), jnp.zeros_like(gi)
    for j in range(EXPERTS_PER_GROUP):
        up = jnp.logical_and(i1 != j, el[j] > v2)
        v2 = jnp.where(up, el[j], v2)
        i2 = jnp.where(up, j, i2)
    e21 = jnp.exp(v2 - v1)
    w1 = g_w / (1.0 + e21)
    w2 = g_w * e21 / (1.0 + e21)
    swap = i2 < i1
    e_lo, e_hi = jnp.where(swap, i2, i1), jnp.where(swap, i1, i2)
    w_lo, w_hi = jnp.where(swap, w2, w1), jnp.where(swap, w1, w2)
    first_pair = jnp.zeros_like(e_lo)
    for lo in range(1, EXPERTS_PER_GROUP - 1):
        first_pair = jnp.where(e_lo >= lo, _PAIRS.index((lo, lo + 1)), first_pair)
    cls = (gi * len(_PAIRS) + first_pair + e_hi - e_lo - 1).astype(F32)
    zero = jnp.zeros_like(w1)
    return jnp.concatenate([cls, w_lo, w_hi, zero, zero, zero, zero, zero], axis=0)


def _outproj(x, a, b, c, wo_stack, layer, g2, router_g, router_gb, router_e, router_eb, tm=2048):
    t, d = x.shape
    n_logits = N_GROUPS + N_EXPERTS
    n_rows = -(-n_logits // PACKED_ROWS) * PACKED_ROWS
    r = jnp.pad(jnp.concatenate([router_g, router_e], axis=1).T, ((0, n_rows - n_logits), (0, 0)))
    rhi = r.astype(BF16)
    rlo = (r - rhi.astype(F32)).astype(BF16)
    rb = jnp.pad(jnp.concatenate([router_gb, router_eb]), (0, n_rows - n_logits)).reshape(n_rows, 1)
    row_spec = lambda w: pl.BlockSpec((tm, w), lambda i: (i, 0))
    full = lambda arr: pl.BlockSpec(arr.shape, lambda i: (0, 0))
    g2r = g2.reshape(1, d)
    return pl.pallas_call(
        _outproj_body,
        grid=(t // tm,),
        in_specs=[row_spec(d), row_spec(a.shape[1]), row_spec(b.shape[1]), row_spec(c.shape[1]),
                  _layer_resident(wo_stack, layer), full(g2r), full(rhi), full(rlo), full(rb)],
        out_specs=[row_spec(d), pl.BlockSpec((8, tm), lambda i: (0, i))],
        out_shape=[jax.ShapeDtypeStruct((t, d), F32), jax.ShapeDtypeStruct((8, t), F32)],
        scratch_shapes=[pltpu.VMEM(wo_stack.shape[1:], BF16)],
        compiler_params=_params(("arbitrary",)),
        name="outproj_route",
    )(x, a, b, c, wo_stack, g2r, rhi, rlo, rb)


def _moe_plan(cls, n_tiles):
    onehot = (cls[:, None] == jnp.arange(N_CLASSES, dtype=I32)[None, :]).astype(I32)
    csum = jnp.cumsum(onehot, axis=0)
    counts = csum[-1]
    padded = ((counts + MOE_TILE - 1) // MOE_TILE) * MOE_TILE
    ends = jnp.cumsum(padded)
    starts = ends - padded
    pos = jnp.sum(onehot * (csum - 1 + starts[None, :]), axis=1)
    tile_start = jnp.arange(n_tiles, dtype=I32) * MOE_TILE
    tile_cls = jnp.sum((ends[None, :] <= tile_start[:, None]).astype(I32), axis=1)
    tile_cls = jnp.minimum(tile_cls, N_CLASSES - 1)
    group, pair = np.divmod(np.arange(N_CLASSES), len(_PAIRS))
    pairs = np.asarray(_PAIRS)
    lo_expert = jnp.asarray(group * EXPERTS_PER_GROUP + pairs[pair, 0], I32)[tile_cls]
    hi_expert = jnp.asarray(group * EXPERTS_PER_GROUP + pairs[pair, 1], I32)[tile_cls]
    used = (ends[-1] // MOE_TILE).astype(I32).reshape(1)
    pad_start = (starts + counts).astype(I32)
    pad_count = (padded - counts).astype(I32)
    return pos.astype(I32), lo_expert, hi_expert, used, pad_start, pad_count


def _store_token_major(ref, x, tok0=0):
    n = x.shape[0]
    for c in range(SUBLANES):
        ref[pl.ds(tok0 * SUBLANES + c, n, stride=SUBLANES), :] = x[:, c * LANES:(c + 1) * LANES]


def _load_token_major(ref, n, tok0=0):
    return jnp.concatenate([ref[pl.ds(tok0 * SUBLANES + c, n, stride=SUBLANES), :] for c in range(SUBLANES)],
                           axis=1)


def _tokens(ref, tok, n=1):
    return ref.at[pl.ds(pl.multiple_of(tok * SUBLANES, SUBLANES), n * SUBLANES)]


def _dispatch_body(pad_start_ref, pad_count_ref, used_ref, pos_ref, x_ref, g2_ref, xs_hbm, xt, zeros, sem, fill_sem,
                   *, tm, n_tiles):
    i = pl.program_id(0)
    half_tile = MOE_TILE // 2

    def fill(act):
        def class_pad(e, c):
            off, n = pad_start_ref[e], pad_count_ref[e]
            bit = half_tile
            while bit:
                take = (n & bit) != 0

                @pl.when(take)
                def _(off=off, bit=bit):
                    act(pltpu.make_async_copy(_tokens(zeros, 0, bit), _tokens(xs_hbm, off, bit), fill_sem))

                off = off + jnp.where(take, bit, 0)
                bit //= 2
            return c

        lax.fori_loop(0, N_CLASSES, class_pad, 0)

        def unused_half_tile(j, c):
            act(pltpu.make_async_copy(zeros, _tokens(xs_hbm, j * half_tile, half_tile), fill_sem))
            return c

        lax.fori_loop(2 * used_ref[0], 2 * n_tiles, unused_half_tile, 0)

    @pl.when(i == 0)
    def _():
        zeros[...] = jnp.zeros_like(zeros)
        fill(lambda cp: cp.start())
        fill(lambda cp: cp.wait())

    _store_token_major(xt, _rms(x_ref[...], g2_ref[...]))

    def issue(g, c):
        base = i * tm + g * ROW_UNROLL
        for j in range(ROW_UNROLL):
            dst = _tokens(xs_hbm, pos_ref[base + j])
            pltpu.make_async_copy(_tokens(xt, g * ROW_UNROLL + j), dst, sem).start(priority=j % 2)
        return c

    lax.fori_loop(0, tm // ROW_UNROLL, issue, 0)
    pltpu.make_async_copy(xt, _tokens(xs_hbm, 0, tm), sem).wait()


def _dispatch(x1, g2, pos, n_tiles, pad_start, pad_count, used, tm=1024):
    t, d = x1.shape
    assert d == SUBLANES * LANES
    grid_spec = pltpu.PrefetchScalarGridSpec(
        num_scalar_prefetch=4,
        grid=(t // tm,),
        in_specs=[pl.BlockSpec((tm, d), lambda i, *_: (i, 0)),
                  pl.BlockSpec((1, d), lambda i, *_: (0, 0))],
        out_specs=pl.BlockSpec(memory_space=pl.ANY),
        scratch_shapes=[pltpu.VMEM((tm * SUBLANES, LANES), F32),
                        pltpu.VMEM((MOE_TILE // 2 * SUBLANES, LANES), F32),
                        pltpu.SemaphoreType.DMA, pltpu.SemaphoreType.DMA],
    )
    return pl.pallas_call(
        functools.partial(_dispatch_body, tm=tm, n_tiles=n_tiles),
        grid_spec=grid_spec,
        out_shape=jax.ShapeDtypeStruct((n_tiles * MOE_TILE * SUBLANES, LANES), F32),
        compiler_params=_params(("arbitrary",)),
        name="moe_dispatch",
    )(pad_start, pad_count, used, pos, x1, g2.reshape(1, d))


def _pack_bf16_pair(lo, hi):
    bits = lambda a: lax.bitcast_convert_type(a.astype(BF16).astype(F32), jnp.uint32)
    return (bits(lo) >> 16) | (bits(hi) & jnp.uint32(0xFFFF0000))


def _unpack_bf16_pair(word):
    lo = lax.bitcast_convert_type(word << 16, F32)
    hi = lax.bitcast_convert_type(word & jnp.uint32(0xFFFF0000), F32)
    return lo, hi


def _ffn_body(lo_ref, hi_ref, used_ref, xs_ref, *refs):
    w_refs, ys_ref, wgu_b, wd_b = refs[:6], refs[6], refs[7], refs[8]
    i = pl.program_id(0)
    live = i < used_ref[0]
    hid = wd_b.shape[1]
    prev = jnp.maximum(i - 1, 0)

    for k, e_ref in enumerate((lo_ref, hi_ref)):
        wg_ref, wu_ref, wd_ref = w_refs[3 * k:3 * k + 3]

        @pl.when(jnp.logical_and(live, jnp.logical_or(i == 0, e_ref[i] != e_ref[prev])))
        def _(k=k, wg_ref=wg_ref, wu_ref=wu_ref, wd_ref=wd_ref):
            _cast_rows(wgu_b.at[k, :, 0:hid], wg_ref)
            _cast_rows(wgu_b.at[k, :, hid:2 * hid], wu_ref)
            _cast_rows(wd_b.at[k], wd_ref)

    @pl.when(live)
    def _():
        hb = _load_token_major(xs_ref, MOE_TILE).astype(BF16)
        gus = [jnp.dot(hb, wgu_b[k], preferred_element_type=F32) for k in range(2)]
        acts = [(gu[:, :hid] * jax.nn.sigmoid(gu[:, :hid]) * gu[:, hid:]).astype(BF16) for gu in gus]
        ys = [jnp.dot(acts[k], wd_b[k], preferred_element_type=F32) for k in range(2)]
        _store_token_major(ys_ref, _pack_bf16_pair(*ys))

    @pl.when(jnp.logical_not(live))
    def _():
        ys_ref[...] = jnp.zeros_like(ys_ref)


def _ffn(xs, w_gate, w_up, w_down, layer, lo_expert, hi_expert, used):
    n_tiles = xs.shape[0] // (MOE_TILE * SUBLANES)
    d, hid = w_down.shape[3], w_down.shape[2]
    tile_rows = MOE_TILE * SUBLANES
    live = lambda i, used: jnp.minimum(i, used[0] - 1)

    def weights(pick):
        up = pl.BlockSpec((None, None, d, hid), lambda i, lo, hi, used: (layer, pick(lo, hi)[i], 0, 0))
        down = pl.BlockSpec((None, None, hid, d), lambda i, lo, hi, used: (layer, pick(lo, hi)[i], 0, 0))
        return [up, up, down]

    grid_spec = pltpu.PrefetchScalarGridSpec(
        num_scalar_prefetch=3,
        grid=(n_tiles,),
        in_specs=[pl.BlockSpec((tile_rows, LANES), lambda i, lo, hi, used: (live(i, used), 0))]
        + weights(lambda lo, hi: lo) + weights(lambda lo, hi: hi),
        out_specs=pl.BlockSpec((tile_rows, LANES), lambda i, lo, hi, used: (i, 0)),
        scratch_shapes=[pltpu.VMEM((2, d, 2 * hid), BF16), pltpu.VMEM((2, hid, d), BF16)],
    )
    return pl.pallas_call(
        _ffn_body,
        grid_spec=grid_spec,
        out_shape=jax.ShapeDtypeStruct(xs.shape, jnp.uint32),
        compiler_params=_params(("arbitrary",)),
        name="moe_experts",
    )(lo_expert, hi_expert, used, xs, w_gate, w_up, w_down, w_gate, w_up, w_down)


def _combine_body(pos_ref, ys_hbm, x1_ref, w_ref, gf_ref, o_ref, buf, sems, *, tm, final_norm):
    i = pl.program_id(0)
    slot = i % 2

    def gather(step, s):
        def issue(g, c):
            base = step * tm + g * ROW_UNROLL
            for j in range(ROW_UNROLL):
                src = _tokens(ys_hbm, pos_ref[base + j])
                pltpu.make_async_copy(src, _tokens(buf.at[s], g * ROW_UNROLL + j),
                                      sems.at[s]).start(priority=j % 2)
            return c

        lax.fori_loop(0, tm // ROW_UNROLL, issue, 0)

    @pl.when(i == 0)
    def _():
        gather(i, slot)

    @pl.when(i + 1 < pl.num_programs(0))
    def _():
        gather(i + 1, 1 - slot)

    pltpu.make_async_copy(_tokens(ys_hbm, 0, tm), buf.at[slot], sems.at[slot]).wait()
    w = w_ref[...]
    y_lo, y_hi = _unpack_bf16_pair(_load_token_major(buf.at[slot], tm))
    y = x1_ref[...] + w[:, 0:1] * y_lo + w[:, 1:2] * y_hi
    if final_norm:
        y = _rms(y, gf_ref[...])
    o_ref[...] = y


def _combine(ys, x1, pos, w2, gf, final_norm, tm=512):
    t, d = x1.shape
    grid_spec = pltpu.PrefetchScalarGridSpec(
        num_scalar_prefetch=1,
        grid=(t // tm,),
        in_specs=[pl.BlockSpec(memory_space=pl.ANY),
                  pl.BlockSpec((tm, d), lambda i, *_: (i, 0)),
                  pl.BlockSpec((tm, 2), lambda i, *_: (i, 0)),
                  pl.BlockSpec((1, d), lambda i, *_: (0, 0))],
        out_specs=pl.BlockSpec((tm, d), lambda i, *_: (i, 0)),
        scratch_shapes=[pltpu.VMEM((2, tm * SUBLANES, LANES), jnp.uint32), pltpu.SemaphoreType.DMA((2,))],
    )
    return pl.pallas_call(
        functools.partial(_combine_body, tm=tm, final_norm=final_norm),
        grid_spec=grid_spec,
        out_shape=jax.ShapeDtypeStruct((t, d), F32),
        compiler_params=_params(("arbitrary",)),
        name="moe_combine",
    )(pos, ys, x1, w2, gf.reshape(1, d))


def _layer(x, batch, seq, layer, norm1, w_in, a_ln_g, a_ln_b, a_ws, a_bs, ret_gn, w_out, norm2,
           router_g, router_gb, router_e, router_eb, w_gate, w_up, w_down, final_gain, is_last):
    t, d = x.shape
    a_groups = a_ws.shape[0]
    aw = a_groups * HEAD_DIM
    c_heads = ret_gn.shape[0] // HEAD_DIM
    cw = c_heads * HEAD_DIM
    bw = w_in.shape[2] - 2 * aw - 4 * cw
    b_heads = (bw // 3) // HEAD_DIM
    a_out, zqkv, zg = _inproj(x, norm1, w_in, layer, a_ln_g, a_ln_b, a_ws, a_bs, bw + 3 * cw, cw)
    zqkv3 = zqkv.reshape(batch, seq, zqkv.shape[1])
    b_out = _attn(zqkv3, 0, b_heads).reshape(t, bw // 3)
    c_out = _retention(zqkv3, bw // cw, zg.reshape(batch, seq, cw), c_heads, ret_gn).reshape(t, cw)
    x1, route = _outproj(x, a_out, b_out, c_out, w_out, layer, norm2,
                         router_g, router_gb, router_e, router_eb)
    cls = route[0].astype(I32)
    w2 = route[1:3].T
    n_tiles = t // MOE_TILE + N_CLASSES
    pos, lo_expert, hi_expert, used, pad_start, pad_count = _moe_plan(cls, n_tiles)
    xs = _dispatch(x1, norm2, pos, n_tiles, pad_start, pad_count, used)
    ys = _ffn(xs, w_gate, w_up, w_down, layer, lo_expert, hi_expert, used)
    return _combine(ys, x1, pos, w2, final_gain, is_last)


def kernel(x, norm1, w_in, a_ln_g, a_ln_b, a_ws, a_bs, ret_gn, w_out, norm2, router_g, router_gb, router_e,
           router_eb, w_gate, w_up, w_down, final_norm):
    batch, seq, d = x.shape
    depth = norm1.shape[0]
    h = x.reshape(batch * seq, d)
    for l in range(depth):
        h = _layer(h, batch, seq, l, norm1[l], w_in, a_ln_g[l], a_ln_b[l], a_ws[l], a_bs[l], ret_gn[l],
                   w_out, norm2[l], router_g[l], router_gb[l], router_e[l], router_eb[l],
                   w_gate, w_up, w_down, final_norm, l == depth - 1)
    return h.reshape(batch, seq, d)
```
